```python
import jax
import jax.numpy as jnp
from jax import lax
import numpy as np

D_MODEL = 4096
BATCH = 2
SEQ = 4096
DEPTH = 2

N_MIXERS = 2
N_MLSTM = (DEPTH + 1) // 2
N_RWKV = DEPTH // 2

ML_HEADS = 8
ML_DV = D_MODEL // ML_HEADS
ML_DQK = ML_DV // 2
ML_CHUNK = 128
ML_GATE_CAP = 15.0
ML_QK = ML_HEADS * ML_DQK
ML_V = ML_HEADS * ML_DV
ML_IN = 2 * ML_QK + 2 * ML_V + 2 * ML_HEADS
ML_NORM_EPS = 1e-6

RW_HEAD = 64
RW_HEADS = D_MODEL // RW_HEAD
RW_DECAY_LORA = max(32, int(round(1.8 * D_MODEL ** 0.5 / 32)) * 32)
RW_AAA_LORA = max(32, int(round(1.8 * D_MODEL ** 0.5 / 32)) * 32)
RW_GATE_LORA = max(32, int(round(0.6 * D_MODEL ** 0.8 / 32)) * 32)
RW_GN_EPS = 64e-5

FFN_HIDDEN = -(-8 * D_MODEL // (3 * 256)) * 256

ALPHA = (2 * DEPTH) ** 0.25
BETA = (8 * DEPTH) ** -0.25
LN_EPS = 1e-5

kernel_name = 'hybrid_mlstm_rwkv7_adaln_deepnorm'


def layer_norm(x, g, b):
    xf = x.astype(jnp.float32)
    mu = jnp.mean(xf, axis=-1, keepdims=True)
    var = jnp.mean(jnp.square(xf - mu), axis=-1, keepdims=True)
    return ((xf - mu) * lax.rsqrt(var + LN_EPS) * g + b).astype(x.dtype)


def swiglu_ffn(u, w_in, w_out):
    gate, up = jnp.split(u @ w_in, [FFN_HIDDEN], axis=-1)
    return (jax.nn.silu(gate) * up) @ w_out


def _to_chunks(t):
    b, h, s = t.shape[:3]
    t = t.reshape((b, h, s // ML_CHUNK, ML_CHUNK) + t.shape[3:])
    return jnp.moveaxis(t, 2, 0)


def mlstm_chunkwise(q, k, v, i_pre, log_f):
    b, h, s, _ = q.shape
    causal = jnp.tril(jnp.ones((ML_CHUNK, ML_CHUNK), dtype=bool))
    g = jnp.cumsum(_to_chunks(log_f), axis=-1)
    xs = (_to_chunks(q), _to_chunks(k), _to_chunks(v), _to_chunks(i_pre), g)

    def step(carry, inp):
        c_st, n_st, m_st = carry
        qc, kc, vc, ic, gc = inp
        log_d = jnp.where(causal, gc[..., :, None] - gc[..., None, :] + ic[..., None, :], -jnp.inf)
        log_inter = gc + m_st[..., None]
        m_row = jnp.maximum(jnp.max(log_d, axis=-1), log_inter)
        scores = jnp.einsum('bhld,bhmd->bhlm', qc, kc) * jnp.exp(log_d - m_row[..., None])
        inter = jnp.exp(log_inter - m_row)
        num = jnp.einsum('bhlm,bhmv->bhlv', scores, vc) + inter[..., None] * jnp.einsum('bhld,bhdv->bhlv', qc, c_st)
        den = jnp.sum(scores, axis=-1) + inter * jnp.einsum('bhld,bhd->bhl', qc, n_st)
        h_out = num / jnp.maximum(jnp.abs(den), jnp.exp(-m_row))[..., None]
        g_last = gc[..., -1]
        log_w = g_last[..., None] - gc + ic
        m_new = jnp.maximum(g_last + m_st, jnp.max(log_w, axis=-1))
        wk = kc * jnp.exp(log_w - m_new[..., None])[..., None]
        carry_decay = jnp.exp(g_last + m_st - m_new)
        c_new = carry_decay[..., None, None] * c_st + jnp.einsum('bhld,bhlv->bhdv', wk, vc)
        n_new = carry_decay[..., None] * n_st + jnp.sum(wk, axis=2)
        return (c_new, n_new, m_new), h_out

    init = (jnp.zeros((b, h, ML_DQK, ML_DV), jnp.float32),
            jnp.zeros((b, h, ML_DQK), jnp.float32),
            jnp.zeros((b, h), jnp.float32))
    _, hs = lax.scan(step, init, xs)
    return jnp.moveaxis(hs, 0, 2).reshape(b, h, s, ML_DV)


def mlstm_mixer(u, w_in, b_i, b_f, norm_g, w_out):
    b, s, _ = u.shape
    proj = u @ w_in

    def heads(t, d):
        return t.reshape(b, s, ML_HEADS, d).transpose(0, 2, 1, 3).astype(jnp.float32)

    q = heads(proj[..., :ML_QK], ML_DQK)
    k = heads(proj[..., ML_QK:2 * ML_QK], ML_DQK) * ML_DQK ** -0.5
    v = heads(proj[..., 2 * ML_QK:2 * ML_QK + ML_V], ML_DV)
    o = proj[..., 2 * ML_QK + ML_V:2 * ML_QK + 2 * ML_V]
    gates = proj[..., 2 * ML_QK + 2 * ML_V:].astype(jnp.float32)
    i_pre = ML_GATE_CAP * jnp.tanh((gates[..., :ML_HEADS] + b_i) / ML_GATE_CAP)
    log_f = jax.nn.log_sigmoid(gates[..., ML_HEADS:] + b_f)
    h = mlstm_chunkwise(q, k, v, i_pre.transpose(0, 2, 1), log_f.transpose(0, 2, 1))
    h = h * lax.rsqrt(jnp.mean(jnp.square(h), axis=-1, keepdims=True) + ML_NORM_EPS)
    h = h.transpose(0, 2, 1, 3).reshape(b, s, ML_V) * norm_g * jax.nn.sigmoid(o.astype(jnp.float32))
    return h.astype(u.dtype) @ w_out


def rwkv7_scan(r, decay, k, v, kk, a):
    b, s, h, n = r.shape

    def step(state, inp):
        r_t, w_t, k_t, v_t, kk_t, a_t = inp
        sa = jnp.einsum('bhvk,bhk->bhv', state, -kk_t)
        state = (state * w_t[:, :, None, :] + sa[..., None] * (kk_t * a_t)[:, :, None, :]
                 + v_t[..., None] * k_t[:, :, None, :])
        return state, jnp.einsum('bhvk,bhk->bhv', state, r_t)

    xs = tuple(jnp.moveaxis(t, 1, 0) for t in (r, decay, k, v, kk, a))
    _, ys = lax.scan(step, jnp.zeros((b, h, n, n), jnp.float32), xs)
    return jnp.moveaxis(ys, 0, 1)


def rwkv7_mixer(u, mu, w_r, w_k, w_v, w0, w1, w2, a0, a1, a2, g1, g2, k_k, k_a, r_k, lnx_g, lnx_b, w_o):
    b, s, d = u.shape
    xx = jnp.pad(u, ((0, 0), (1, 0), (0, 0)))[:, :s] - u
    xr, xw, xk, xv, xa, xg = [u + xx * mu[j] for j in range(6)]
    r = xr @ w_r
    k = xk @ w_k
    v = xv @ w_v
    w = -jax.nn.softplus(-(w0 + jnp.tanh(xw @ w1) @ w2)) - 0.5
    a = jax.nn.sigmoid(a0 + (xa @ a1) @ a2)
    g = jax.nn.sigmoid(xg @ g1) @ g2

    def heads(t):
        return t.astype(jnp.float32).reshape(b, s, RW_HEADS, RW_HEAD)

    kk = heads(k * k_k)
    kk = kk / jnp.maximum(jnp.linalg.norm(kk, axis=-1, keepdims=True), 1e-12)
    k = heads(k * (1 + (a - 1) * k_a))
    r, v, a = heads(r), heads(v), heads(a)
    decay = jnp.exp(-jnp.exp(heads(w)))
    y = rwkv7_scan(r, decay, k, v, kk, a)
    mean = jnp.mean(y, axis=-1, keepdims=True)
    var = jnp.mean(jnp.square(y - mean), axis=-1, keepdims=True)
    y = ((y - mean) * lax.rsqrt(var + RW_GN_EPS)).reshape(b, s, d) * lnx_g + lnx_b
    bonus = (jnp.sum(r * k * r_k, axis=-1, keepdims=True) * v).reshape(b, s, d)
    return ((y + bonus) * g.astype(jnp.float32)).astype(u.dtype) @ w_o


def setup_inputs(seed: int = 0) -> dict:
    key = jax.random.key(seed)
    ks = iter(jax.random.split(key, 40))
    D = D_MODEL

    def nrm(shape, scale):
        return jax.random.normal(next(ks), shape, jnp.float32) * scale

    x = nrm((BATCH, SEQ, D), 1.0)
    c = nrm((BATCH, D), 1.0)
    ada_w = nrm((DEPTH, D, 6 * D), D ** -0.5)
    ada_b = nrm((DEPTH, 6 * D), 0.02)
    mix_ln_g = 1.0 + nrm((DEPTH, D), 0.02)
    mix_ln_b = nrm((DEPTH, D), 0.02)
    ffn_ln_g = 1.0 + nrm((DEPTH, D), 0.02)
    ffn_ln_b = nrm((DEPTH, D), 0.02)
    ffn_w_in = nrm((DEPTH, D, 2 * FFN_HIDDEN), D ** -0.5)
    ffn_w_out = nrm((DEPTH, FFN_HIDDEN, D), FFN_HIDDEN ** -0.5 * BETA)

    ml_w_in = jnp.concatenate([
        nrm((N_MLSTM, D, 2 * ML_QK), D ** -0.5),
        nrm((N_MLSTM, D, ML_V), D ** -0.5 * BETA),
        nrm((N_MLSTM, D, ML_V + 2 * ML_HEADS), D ** -0.5)], axis=-1)
    ml_b_i = nrm((N_MLSTM, ML_HEADS), 0.1)
    ml_b_f = jnp.linspace(3.0, 6.0, ML_HEADS)[None, :] + nrm((N_MLSTM, ML_HEADS), 0.1)
    ml_norm_g = 1.0 + nrm((N_MLSTM, ML_V), 0.02)
    ml_w_out = nrm((N_MLSTM, ML_V, D), ML_V ** -0.5 * BETA)

    rw_mu = jax.random.uniform(next(ks), (N_RWKV, 6, D), jnp.float32)
    rw_w_r = nrm((N_RWKV, D, D), D ** -0.5)
    rw_w_k = nrm((N_RWKV, D, D), D ** -0.5)
    rw_w_v = nrm((N_RWKV, D, D), D ** -0.5 * BETA)
    lin = jnp.linspace(0.0, 1.0, D)
    rw_w0 = (-6.5 + 5.0 * lin ** 0.9)[None, :] + nrm((N_RWKV, D), 0.1)
    rw_w1 = nrm((N_RWKV, D, RW_DECAY_LORA), D ** -0.5)
    rw_w2 = nrm((N_RWKV, RW_DECAY_LORA, D), 0.1 * RW_DECAY_LORA ** -0.5)
    rw_a0 = nrm((N_RWKV, D), 0.1)
    rw_a1 = nrm((N_RWKV, D, RW_AAA_LORA), D ** -0.5)
    rw_a2 = nrm((N_RWKV, RW_AAA_LORA, D), 0.5 * RW_AAA_LORA ** -0.5)
    rw_g1 = nrm((N_RWKV, D, RW_GATE_LORA), D ** -0.5)
    rw_g2 = nrm((N_RWKV, RW_GATE_LORA, D), RW_GATE_LORA ** -0.5)
    rw_k_k = 0.85 + nrm((N_RWKV, D), 0.02)
    rw_k_a = 1.0 + nrm((N_RWKV, D), 0.02)
    rw_r_k = nrm((N_RWKV, RW_HEADS, RW_HEAD), 0.1)
    rw_lnx_g = 1.0 + nrm((N_RWKV, D), 0.02)
    rw_lnx_b = nrm((N_RWKV, D), 0.02)
    rw_w_o = nrm((N_RWKV, D, D), D ** -0.5 * BETA)
    return {'x': x, 'c': c, 'ada_w': ada_w, 'ada_b': ada_b,
            'mix_ln_g': mix_ln_g, 'mix_ln_b': mix_ln_b, 'ffn_ln_g': ffn_ln_g, 'ffn_ln_b': ffn_ln_b,
            'ffn_w_in': ffn_w_in, 'ffn_w_out': ffn_w_out,
            'ml_w_in': ml_w_in, 'ml_b_i': ml_b_i, 'ml_b_f': ml_b_f, 'ml_norm_g': ml_norm_g, 'ml_w_out': ml_w_out,
            'rw_mu': rw_mu, 'rw_w_r': rw_w_r, 'rw_w_k': rw_w_k, 'rw_w_v': rw_w_v,
            'rw_w0': rw_w0, 'rw_w1': rw_w1, 'rw_w2': rw_w2,
            'rw_a0': rw_a0, 'rw_a1': rw_a1, 'rw_a2': rw_a2, 'rw_g1': rw_g1, 'rw_g2': rw_g2,
            'rw_k_k': rw_k_k, 'rw_k_a': rw_k_a, 'rw_r_k': rw_r_k,
            'rw_lnx_g': rw_lnx_g, 'rw_lnx_b': rw_lnx_b, 'rw_w_o': rw_w_o}


def reference(x, c, ada_w, ada_b, mix_ln_g, mix_ln_b, ffn_ln_g, ffn_ln_b, ffn_w_in, ffn_w_out,
              ml_w_in, ml_b_i, ml_b_f, ml_norm_g, ml_w_out,
              rw_mu, rw_w_r, rw_w_k, rw_w_v, rw_w0, rw_w1, rw_w2, rw_a0, rw_a1, rw_a2, rw_g1, rw_g2,
              rw_k_k, rw_k_a, rw_r_k, rw_lnx_g, rw_lnx_b, rw_w_o):
    c_act = jax.nn.silu(c)
    for layer in range(DEPTH):
        mod = c_act @ ada_w[layer] + ada_b[layer]
        sh_m, sc_m, gt_m, sh_f, sc_f, gt_f = [t[:, None, :] for t in jnp.split(mod, 6, axis=-1)]
        u = x * (1 + sc_m) + sh_m
        j = layer // N_MIXERS
        if layer % N_MIXERS == 0:
            y = mlstm_mixer(u, ml_w_in[j], ml_b_i[j], ml_b_f[j], ml_norm_g[j], ml_w_out[j])
        else:
            y = rwkv7_mixer(u, rw_mu[j], rw_w_r[j], rw_w_k[j], rw_w_v[j], rw_w0[j], rw_w1[j], rw_w2[j],
                            rw_a0[j], rw_a1[j], rw_a2[j], rw_g1[j], rw_g2[j], rw_k_k[j], rw_k_a[j],
                            rw_r_k[j], rw_lnx_g[j], rw_lnx_b[j], rw_w_o[j])
        x = layer_norm(ALPHA * x + gt_m * y, mix_ln_g[layer], mix_ln_b[layer])
        u = x * (1 + sc_f) + sh_f
        x = layer_norm(ALPHA * x + gt_f * swiglu_ffn(u, ffn_w_in[layer], ffn_w_out[layer]),
                       ffn_ln_g[layer], ffn_ln_b[layer])
    return x
```

```python
import functools

import jax
import jax.numpy as jnp
from jax import lax
from jax.experimental import pallas as pl
from jax.experimental.pallas import tpu as pltpu

F32 = jnp.float32
BF16 = jnp.bfloat16

ML_HEADS = 8
ML_CHUNK = 128
ML_GATE_CAP = 15.0
ML_NORM_EPS = 1e-6
RW_HEAD = 64
RW_CHUNK = 64
RW_GN_EPS = 64e-5
LN_EPS = 1e-5
LANES = 128
VMEM_LIMIT_BYTES = 56 * 1024 * 1024


def _params(*semantics):
    return pltpu.CompilerParams(dimension_semantics=semantics, vmem_limit_bytes=VMEM_LIMIT_BYTES)


def _pick(n, candidates):
    for c in candidates:
        if n % c == 0:
            return c
    return n


def _dot(a, b):
    return jnp.dot(a, b, preferred_element_type=F32)


def _dot_nt(a, b):
    return lax.dot_general(a, b, (((1,), (1,)), ((), ())), preferred_element_type=F32)


def _dot_tn(a, b):
    return lax.dot_general(a, b, (((0,), (0,)), ((), ())), preferred_element_type=F32)


def _split2(x):
    hi = x.astype(BF16)
    lo = (x - hi.astype(F32)).astype(BF16)
    return hi, lo


def _split3(x):
    hi = x.astype(BF16)
    r = x - hi.astype(F32)
    mid = r.astype(BF16)
    lo = (r - mid.astype(F32)).astype(BF16)
    return hi, mid, lo


def _dot_hp(a, b, dot=_dot):
    ah, al = _split2(a)
    bh, bl = _split2(b)
    return dot(ah, bh) + (dot(ah, bl) + dot(al, bh))


def _dot_01(m, x):
    hi, mid, lo = _split3(x)
    return _dot(m, hi) + (_dot(m, mid) + _dot(m, lo))


def _sigmoid(x):
    return 1.0 / (1.0 + jnp.exp(-x))


def _softplus(x):
    return jnp.maximum(x, 0.0) + jnp.log1p(jnp.exp(-jnp.abs(x)))


def _ada_kernel(c_ref, w_ref, b_ref, o_ref):
    c = c_ref[...]
    c_act = (c * _sigmoid(c)).astype(BF16)
    o_ref[0] = _dot(c_act, w_ref[0].astype(BF16)) + b_ref[0]


def _ada(c, ada_w, ada_b):
    depth, d, n = ada_w.shape
    b = c.shape[0]
    tn = _pick(n, (512, 256, 128))
    return pl.pallas_call(
        _ada_kernel,
        out_shape=jax.ShapeDtypeStruct((depth, b, n), F32),
        grid=(depth, n // tn),
        in_specs=[pl.BlockSpec((b, d), lambda l, j: (0, 0)),
                  pl.BlockSpec((1, d, tn), lambda l, j: (l, 0, j)),
                  pl.BlockSpec((1, 1, tn), lambda l, j: (l, 0, j))],
        out_specs=pl.BlockSpec((1, b, tn), lambda l, j: (l, 0, j)),
        compiler_params=_params("parallel", "parallel"),
        name="ada_mod",
    )(c, ada_w, ada_b.reshape(depth, 1, n))


def _modulate_kernel(x_ref, sc_ref, sh_ref, o_ref):
    o_ref[...] = (x_ref[...] * (1.0 + sc_ref[0]) + sh_ref[0]).astype(o_ref.dtype)


def _modulate(x2, sc, sh, seq):
    t, d = x2.shape
    ts = _pick(seq, (512, 256, 128))
    vec = pl.BlockSpec((1, 1, d), lambda i: ((i * ts) // seq, 0, 0))
    return pl.pallas_call(
        _modulate_kernel,
        out_shape=jax.ShapeDtypeStruct((t, d), BF16),
        grid=(t // ts,),
        in_specs=[pl.BlockSpec((ts, d), lambda i: (i, 0)), vec, vec],
        out_specs=pl.BlockSpec((ts, d), lambda i: (i, 0)),
        compiler_params=_params("parallel"),
        name="modulate",
    )(x2, sc, sh)


def _mm_kernel(a_ref, w_ref, b_ref, o_ref, *, act):
    y = _dot(a_ref[...], w_ref[...]) + b_ref[...]
    o_ref[...] = act(y).astype(o_ref.dtype)


def _mm(a, w, bias, act, out_dtype, name):
    t, k = a.shape
    n = w.shape[1]
    tm = _pick(t, (1024, 512, 256, 128))
    tn = _pick(n, (1024, 512, 256, 128))
    return pl.pallas_call(
        functools.partial(_mm_kernel, act=act),
        out_shape=jax.ShapeDtypeStruct((t, n), out_dtype),
        grid=(t // tm, n // tn),
        in_specs=[pl.BlockSpec((tm, k), lambda i, j: (i, 0)),
                  pl.BlockSpec((k, tn), lambda i, j: (0, j)),
                  pl.BlockSpec((1, tn), lambda i, j: (0, j))],
        out_specs=pl.BlockSpec((tm, tn), lambda i, j: (i, j)),
        compiler_params=_params("parallel", "parallel"),
        name=name,
    )(a, w, bias)


def _identity(y):
    return y


def _swiglu_kernel(a_ref, wg_ref, wu_ref, o_ref):
    a = a_ref[...]
    gate = _dot(a, wg_ref[...])
    up = _dot(a, wu_ref[...])
    o_ref[...] = (gate * _sigmoid(gate) * up).astype(o_ref.dtype)


def _mm_swiglu(a, w_in, hidden):
    t, k = a.shape
    tm = _pick(t, (1024, 512, 256, 128))
    tn = _pick(hidden, (256, 128))
    nj = hidden // tn
    return pl.pallas_call(
        _swiglu_kernel,
        out_shape=jax.ShapeDtypeStruct((t, hidden), BF16),
        grid=(t // tm, nj),
        in_specs=[pl.BlockSpec((tm, k), lambda i, j: (i, 0)),
                  pl.BlockSpec((k, tn), lambda i, j: (0, j)),
                  pl.BlockSpec((k, tn), lambda i, j: (0, j + nj))],
        out_specs=pl.BlockSpec((tm, tn), lambda i, j: (i, j)),
        compiler_params=_params("parallel", "parallel"),
        name="ffn_in_swiglu",
    )(a, w_in, w_in)


def _resid_kernel(a_ref, w_ref, x_ref, gt_ref, o_ref, *scratch, alpha, nk):
    def finish(y):
        o_ref[...] = alpha * x_ref[...] + gt_ref[0] * y

    if nk == 1:
        finish(_dot(a_ref[...], w_ref[...]))
        return
    acc_ref, = scratch
    kk = pl.program_id(2)

    @pl.when(kk == 0)
    def _():
        acc_ref[...] = _dot(a_ref[...], w_ref[...])

    @pl.when(jnp.logical_and(kk > 0, kk < nk - 1))
    def _():
        acc_ref[...] += _dot(a_ref[...], w_ref[...])

    @pl.when(kk == nk - 1)
    def _():
        finish(acc_ref[...] + _dot(a_ref[...], w_ref[...]))


def _mm_resid(a, w, x2, gt, alpha, seq, name):
    t, k = a.shape
    n = w.shape[1]
    tm = _pick(min(t, seq), (1024, 512, 256, 128))
    tk = k if k <= 4096 else _pick(k, (k // 2,))
    nk = k // tk
    tn = _pick(n, (1024, 512, 256, 128)) if nk == 1 else _pick(n, (512, 256, 128))
    scratch = [] if nk == 1 else [pltpu.VMEM((tm, tn), F32)]
    return pl.pallas_call(
        functools.partial(_resid_kernel, alpha=alpha, nk=nk),
        out_shape=jax.ShapeDtypeStruct((t, n), F32),
        grid=(t // tm, n // tn, nk),
        in_specs=[pl.BlockSpec((tm, tk), lambda i, j, kk: (i, kk)),
                  pl.BlockSpec((tk, tn), lambda i, j, kk: (kk, j)),
                  pl.BlockSpec((tm, tn), lambda i, j, kk: (i, j)),
                  pl.BlockSpec((1, 1, tn), lambda i, j, kk: ((i * tm) // seq, 0, j))],
        out_specs=pl.BlockSpec((tm, tn), lambda i, j, kk: (i, j)),
        scratch_shapes=scratch,
        compiler_params=_params("parallel", "parallel", "arbitrary"),
        name=name,
    )(a, w, x2, gt)


def _ln_rows(z, g, b):
    mu = jnp.mean(z, axis=-1, keepdims=True)
    zc = z - mu
    var = jnp.mean(zc * zc, axis=-1, keepdims=True)
    return zc * lax.rsqrt(var + LN_EPS) * g + b


def _ln_kernel(z_ref, g_ref, b_ref, x_ref):
    x_ref[...] = _ln_rows(z_ref[...], g_ref[...], b_ref[...])


def _ln_mod_kernel(z_ref, g_ref, b_ref, sc_ref, sh_ref, x_ref, u_ref):
    x = _ln_rows(z_ref[...], g_ref[...], b_ref[...])
    x_ref[...] = x
    u_ref[...] = (x * (1.0 + sc_ref[0]) + sh_ref[0]).astype(u_ref.dtype)


def _ln_shift_kernel(z_ref, zp_ref, g_ref, b_ref, sc_ref, sh_ref, mu_ref, x_ref, *mix_refs, ts, seq):
    g, b = g_ref[...], b_ref[...]
    scale, shift = 1.0 + sc_ref[0], sh_ref[0]
    x = _ln_rows(z_ref[...], g, b)
    x_ref[...] = x
    u = x * scale + shift
    prev = _ln_rows(zp_ref[...], g, b)[7:8, :] * scale + shift
    first = (pl.program_id(0) * ts) % seq == 0
    prev = jnp.where(first, 0.0, prev)
    row = lax.broadcasted_iota(jnp.int32, u.shape, 0)
    u_prev = jnp.where(row == 0, prev, pltpu.roll(u, shift=1, axis=0))
    xx = u_prev - u
    for j, ref in enumerate(mix_refs):
        ref[...] = (u + xx * mu_ref[j:j + 1, :]).astype(ref.dtype)


def _layer_norm(z, g, b, seq, nxt=None):
    t, d = z.shape
    row = pl.BlockSpec((1, d), lambda i: (0, 0))
    if nxt is None:
        ts = _pick(seq, (256, 128))
        tile = pl.BlockSpec((ts, d), lambda i: (i, 0))
        return pl.pallas_call(
            _ln_kernel, out_shape=jax.ShapeDtypeStruct((t, d), F32), grid=(t // ts,),
            in_specs=[tile, row, row], out_specs=tile,
            compiler_params=_params("parallel"), name="layer_norm",
        )(z, g, b)
    if nxt[0] == "mod":
        ts = _pick(seq, (256, 128))
        tile = pl.BlockSpec((ts, d), lambda i: (i, 0))
        vec = pl.BlockSpec((1, 1, d), lambda i: ((i * ts) // seq, 0, 0))
        return pl.pallas_call(
            _ln_mod_kernel,
            out_shape=(jax.ShapeDtypeStruct((t, d), F32), jax.ShapeDtypeStruct((t, d), BF16)),
            grid=(t // ts,),
            in_specs=[tile, row, row, vec, vec], out_specs=(tile, tile),
            compiler_params=_params("parallel"), name="layer_norm_mod",
        )(z, g, b, nxt[1], nxt[2])
    _, sc, sh, mu = nxt
    nmix = mu.shape[0]
    ts = _pick(seq, (128,))
    sub = 8
    tile = pl.BlockSpec((ts, d), lambda i: (i, 0))
    prev = pl.BlockSpec((sub, d), lambda i: (jnp.maximum(i * (ts // sub) - 1, 0), 0))
    vec = pl.BlockSpec((1, 1, d), lambda i: ((i * ts) // seq, 0, 0))
    return pl.pallas_call(
        functools.partial(_ln_shift_kernel, ts=ts, seq=seq),
        out_shape=(jax.ShapeDtypeStruct((t, d), F32),) + (jax.ShapeDtypeStruct((t, d), BF16),) * nmix,
        grid=(t // ts,),
        in_specs=[tile, prev, row, row, vec, vec, pl.BlockSpec((nmix, d), lambda i: (0, 0))],
        out_specs=(tile,) * (1 + nmix),
        compiler_params=_params("parallel"), name="layer_norm_shift",
    )(z, z, g, b, sc, sh, mu)


def _ml_gate_act(y):
    lane = lax.broadcasted_iota(jnp.int32, y.shape, 1)
    i_pre = ML_GATE_CAP * jnp.tanh(y / ML_GATE_CAP)
    log_f = -_softplus(-y)
    return jnp.where(lane < ML_HEADS, i_pre, log_f)


def _mlstm_kernel(q_ref, k_ref, v_ref, o_ref, icol_ref, fcol_ref, irow_ref, frow_ref, ng_ref, out_ref,
                  c_sc, n_sc, m_sc):
    L = ML_CHUNK

    @pl.when(pl.program_id(2) == 0)
    def _():
        c_sc[...] = jnp.zeros_like(c_sc)
        n_sc[...] = jnp.zeros_like(n_sc)
        m_sc[...] = jnp.zeros_like(m_sc)

    q, k, v = q_ref[...], k_ref[...], v_ref[...]
    dqk = q.shape[1]
    k_scale = dqk ** -0.5
    i_col, f_col = icol_ref[...], fcol_ref[...]
    i_row, f_row = irow_ref[0], frow_ref[0]
    c_st, n_st, m_st = c_sc[...], n_sc[...], m_sc[...]

    r_idx = lax.broadcasted_iota(jnp.int32, (L, L), 0)
    c_idx = lax.broadcasted_iota(jnp.int32, (L, L), 1)
    causal = r_idx >= c_idx
    g_col = jnp.sum(jnp.where(causal, f_row, 0.0), axis=1, keepdims=True)
    g_row = jnp.sum(jnp.where(r_idx <= c_idx, f_col, 0.0), axis=0, keepdims=True)
    g_last = jnp.sum(f_row, axis=1, keepdims=True)

    log_d = jnp.where(causal, g_col - g_row + i_row, -jnp.inf)
    log_inter = g_col + m_st
    m_row = jnp.maximum(jnp.max(log_d, axis=1, keepdims=True), log_inter)
    scores = _dot_nt(q, k) * k_scale * jnp.exp(log_d - m_row)
    inter = jnp.exp(log_inter - m_row)
    num = _dot(scores.astype(BF16), v) + inter * _dot(q, c_st.astype(BF16))
    qn = jnp.sum(q.astype(F32) * n_st, axis=1, keepdims=True)
    den = jnp.sum(scores, axis=1, keepdims=True) + inter * qn
    h = num / jnp.maximum(jnp.abs(den), jnp.exp(-m_row))
    h = h * lax.rsqrt(jnp.mean(h * h, axis=1, keepdims=True) + ML_NORM_EPS)
    out_ref[...] = (h * ng_ref[...] * _sigmoid(o_ref[...].astype(F32))).astype(out_ref.dtype)

    log_w = g_last - g_col + i_col
    m_new = jnp.maximum(g_last + m_st, jnp.max(log_w, axis=0, keepdims=True))
    wk = k.astype(F32) * (k_scale * jnp.exp(log_w - m_new))
    decay = jnp.exp(g_last + m_st - m_new)
    c_sc[...] = decay * c_st + _dot_tn(wk.astype(BF16), v)
    n_sc[...] = decay * n_st + jnp.sum(wk, axis=0, keepdims=True)
    m_sc[...] = m_new


def _mlstm(proj, gates, norm_g, batch, seq):
    t = proj.shape[0]
    H, L = ML_HEADS, ML_CHUNK
    dv = norm_g.shape[1] // H
    dqk = dv // 2
    nc = seq // L
    g = gates[:, :2 * H].reshape(batch, seq, 2 * H).transpose(0, 2, 1)
    i_g, f_g = g[:, :H], g[:, H:]
    col = lambda a: a.reshape(batch, H, seq, 1)
    rowv = lambda a: a.reshape(batch * H * nc, 1, L)
    col_spec = pl.BlockSpec((None, None, L, 1), lambda b, h, c: (b, h, c, 0))
    row_spec = pl.BlockSpec((1, 1, L), lambda b, h, c: ((b * H + h) * nc + c, 0, 0))
    tok = lambda b, c: b * nc + c
    return pl.pallas_call(
        _mlstm_kernel,
        out_shape=jax.ShapeDtypeStruct((t, H * dv), BF16),
        grid=(batch, H, nc),
        in_specs=[pl.BlockSpec((L, dqk), lambda b, h, c: (tok(b, c), h)),
                  pl.BlockSpec((L, dqk), lambda b, h, c: (tok(b, c), H + h)),
                  pl.BlockSpec((L, dv), lambda b, h, c: (tok(b, c), H + h)),
                  pl.BlockSpec((L, dv), lambda b, h, c: (tok(b, c), 2 * H + h)),
                  col_spec, col_spec, row_spec, row_spec,
                  pl.BlockSpec((1, dv), lambda b, h, c: (0, h))],
        out_specs=pl.BlockSpec((L, dv), lambda b, h, c: (tok(b, c), h)),
        scratch_shapes=[pltpu.VMEM((dqk, dv), F32), pltpu.VMEM((1, dqk), F32), pltpu.VMEM((1, 1), F32)],
        compiler_params=_params("parallel", "parallel", "arbitrary"),
        name="mlstm_chunk",
    )(proj, proj, proj, proj, col(i_g), col(f_g), rowv(i_g), rowv(f_g), norm_g)


def _rwkv_kernel(r_ref, k_ref, v_ref, w_ref, a_ref, g_ref, kk_ref, ka_ref, rk_ref, lg_ref, lb_ref, o_ref, s_sc,
                 *, heads):
    L, N = RW_CHUNK, RW_HEAD
    gw = heads * N

    @pl.when(pl.program_id(2) == 0)
    def _():
        s_sc[...] = jnp.zeros_like(s_sc)

    r, k, v = r_ref[...], k_ref[...], v_ref[...]
    log_decay = -jnp.exp(-_softplus(-w_ref[...]) - 0.5)
    a = _sigmoid(a_ref[...])

    li = lax.broadcasted_iota(jnp.int32, (gw, gw), 0) // N
    lj = lax.broadcasted_iota(jnp.int32, (gw, gw), 1) // N
    head_ones = jnp.where(li == lj, 1.0, 0.0).astype(BF16)
    t_i = lax.broadcasted_iota(jnp.int32, (L, L), 0)
    t_j = lax.broadcasted_iota(jnp.int32, (L, L), 1)
    incl = t_i >= t_j
    strict = t_i > t_j
    tri = jnp.where(incl, 1.0, 0.0).astype(BF16)

    kkr = k * kk_ref[...]
    kk = kkr / jnp.maximum(jnp.sqrt(_dot_01_rhs(kkr * kkr, head_ones)), 1e-12)
    k = k * (1.0 + (a - 1.0) * ka_ref[...])
    bonus = _dot_01_rhs(r * k * rk_ref[...], head_ones) * v

    cl = _dot_01(tri, log_decay)
    cl_last = cl[L - 1:L, :]
    inv_gam = jnp.exp(-cl)
    to_end = jnp.exp(cl_last - cl)
    r_hat = r * jnp.exp(cl)
    a_hat = -kk * jnp.exp(cl - log_decay)
    kka = kk * a
    b_hat = kka * inv_gam
    k_hat = k * inv_gam
    b_end = kka * to_end
    k_end = k * to_end
    gam_end = jnp.exp(cl_last)

    eye = jnp.where(t_i == t_j, 1.0, 0.0)
    ys = []
    for h in range(heads):
        sl = slice(h * N, (h + 1) * N)
        s0 = s_sc[h]
        vh = v[:, sl]
        lhs = jnp.concatenate([a_hat[:, sl], r_hat[:, sl]], axis=0)
        rhs = jnp.concatenate([b_hat[:, sl], k_hat[:, sl]], axis=0)
        m4 = _dot_hp(lhs, rhs, _dot_nt)
        a_ab = jnp.where(strict, m4[:L, :L], 0.0)
        a_ak = jnp.where(strict, m4[:L, L:], 0.0)
        a_rb = jnp.where(incl, m4[L:, :L], 0.0)
        a_rk = jnp.where(incl, m4[L:, L:], 0.0)
        x = eye + jnp.where((t_i // 2 == t_j // 2), a_ab, 0.0)
        s = 2
        while s < L:
            blk = jnp.logical_and(t_i // (2 * s) == t_j // (2 * s),
                                  jnp.logical_and(t_i % (2 * s) >= s, t_j % (2 * s) < s))
            x = x + _dot_hp(_dot_hp(x, jnp.where(blk, a_ab, 0.0)), x)
            s *= 2
        su = _dot_hp(x, _dot_hp(a_hat[:, sl], s0, _dot_nt) + _dot_hp(a_ak, vh))
        ys.append(_dot_hp(r_hat[:, sl], s0, _dot_nt) + _dot_hp(a_rb, su) + _dot_hp(a_rk, vh))
        s_sc[h] = s0 * gam_end[:, sl] + _dot_hp(su, b_end[:, sl], _dot_tn) + _dot_hp(vh, k_end[:, sl], _dot_tn)
    y = jnp.concatenate(ys, axis=1)

    inv_n = 1.0 / N
    mean = _dot_01_rhs(y, head_ones) * inv_n
    yc = y - mean
    var = _dot_01_rhs(yc * yc, head_ones) * inv_n
    yn = yc * lax.rsqrt(var + RW_GN_EPS) * lg_ref[...] + lb_ref[...]
    o_ref[...] = ((yn + bonus) * g_ref[...]).astype(o_ref.dtype)


def _dot_01_rhs(x, m):
    hi, mid, lo = _split3(x)
    return _dot(hi, m) + (_dot(mid, m) + _dot(lo, m))


def _rwkv(r, k, v, w, a, g, k_k, k_a, r_k, lnx_g, lnx_b, batch, seq):
    t, d = r.shape
    L = RW_CHUNK
    heads = 4
    gw = heads * RW_HEAD
    nc = seq // L
    tile = pl.BlockSpec((L, gw), lambda b, h, c: (b * nc + c, h))
    row = pl.BlockSpec((1, gw), lambda b, h, c: (0, h))
    return pl.pallas_call(
        functools.partial(_rwkv_kernel, heads=heads),
        out_shape=jax.ShapeDtypeStruct((t, d), BF16),
        grid=(batch, d // gw, nc),
        in_specs=[tile] * 6 + [row] * 5,
        out_specs=tile,
        scratch_shapes=[pltpu.VMEM((heads, RW_HEAD, RW_HEAD), F32)],
        compiler_params=_params("parallel", "parallel", "arbitrary"),
        name="rwkv7_chunk",
    )(r, k, v, w, a, g, k_k, k_a, r_k, lnx_g, lnx_b)


def kernel(x, c, ada_w, ada_b, mix_ln_g, mix_ln_b, ffn_ln_g, ffn_ln_b, ffn_w_in, ffn_w_out, ml_w_in, ml_b_i, ml_b_f, ml_norm_g, ml_w_out, rw_mu, rw_w_r, rw_w_k, rw_w_v, rw_w0, rw_w1, rw_w2, rw_a0, rw_a1, rw_a2, rw_g1, rw_g2, rw_k_k, rw_k_a, rw_r_k, rw_lnx_g, rw_lnx_b, rw_w_o):
    batch, seq, d = x.shape
    depth = ada_w.shape[0]
    t = batch * seq
    alpha = (2 * depth) ** 0.25
    hidden = ffn_w_out.shape[1]
    n_mixers = 2

    mod = _ada(c, ada_w, ada_b).reshape(depth, batch, 6, 1, d)
    row = lambda p: p.reshape(1, -1)
    zero_bias = lambda n: jnp.zeros((1, n), F32)

    x2 = x.reshape(t, d)
    u = None
    for layer in range(depth):
        sh_m, sc_m, gt_m, sh_f, sc_f, gt_f = [mod[layer, :, i] for i in range(6)]
        j = layer // n_mixers
        if layer % n_mixers == 0:
            if u is None:
                u = _modulate(x2, sc_m, sh_m, seq)
            qkvo = ml_w_in.shape[2] - 2 * ML_HEADS
            proj = _mm(u, ml_w_in[j, :, :qkvo].astype(BF16), zero_bias(qkvo), _identity, BF16, "mlstm_in")
            w_gate = jnp.pad(ml_w_in[j, :, qkvo:], ((0, 0), (0, LANES - 2 * ML_HEADS))).astype(BF16)
            b_gate = jnp.pad(jnp.concatenate([ml_b_i[j], ml_b_f[j]]), (0, LANES - 2 * ML_HEADS)).reshape(1, LANES)
            gates = _mm(u, w_gate, b_gate, _ml_gate_act, F32, "mlstm_gates")
            y = _mlstm(proj, gates, row(ml_norm_g[j]), batch, seq)
            z = _mm_resid(y, ml_w_out[j].astype(BF16), x2, gt_m, alpha, seq, "mlstm_out")
            x2, u = _layer_norm(z, row(mix_ln_g[layer]), row(mix_ln_b[layer]), seq, ("mod", sc_f, sh_f))
        else:
            xr, xw, xk, xv, xa, xg = u
            nb = zero_bias
            r = _mm(xr, rw_w_r[j].astype(BF16), nb(d), _identity, F32, "rwkv_r")
            k = _mm(xk, rw_w_k[j].astype(BF16), nb(d), _identity, F32, "rwkv_k")
            v = _mm(xv, rw_w_v[j].astype(BF16), nb(d), _identity, F32, "rwkv_v")
            lw = _mm(xw, rw_w1[j].astype(BF16), nb(rw_w1.shape[2]), jnp.tanh, BF16, "rwkv_w1")
            la = _mm(xa, rw_a1[j].astype(BF16), nb(rw_a1.shape[2]), _identity, BF16, "rwkv_a1")
            lg = _mm(xg, rw_g1[j].astype(BF16), nb(rw_g1.shape[2]), _sigmoid, BF16, "rwkv_g1")
            w = _mm(lw, rw_w2[j].astype(BF16), row(rw_w0[j]), _identity, F32, "rwkv_w2")
            a = _mm(la, rw_a2[j].astype(BF16), row(rw_a0[j]), _identity, F32, "rwkv_a2")
            g = _mm(lg, rw_g2[j].astype(BF16), nb(d), _identity, F32, "rwkv_g2")
            y = _rwkv(r, k, v, w, a, g, row(rw_k_k[j]), row(rw_k_a[j]), row(rw_r_k[j]),
                      row(rw_lnx_g[j]), row(rw_lnx_b[j]), batch, seq)
            z = _mm_resid(y, rw_w_o[j].astype(BF16), x2, gt_m, alpha, seq, "rwkv_out")
            x2, u = _layer_norm(z, row(mix_ln_g[layer]), row(mix_ln_b[layer]), seq, ("mod", sc_f, sh_f))
        hid = _mm_swiglu(u, ffn_w_in[layer].astype(BF16), hidden)
        z = _mm_resid(hid, ffn_w_out[layer].astype(BF16), x2, gt_f, alpha, seq, "ffn_out")
        g_f, b_f = row(ffn_ln_g[layer]), row(ffn_ln_b[layer])
        if layer + 1 == depth:
            x2, u = _layer_norm(z, g_f, b_f, seq), None
        else:
            nsh, nsc = mod[layer + 1, :, 0], mod[layer + 1, :, 1]
            if (layer + 1) % n_mixers == 0:
                x2, u = _layer_norm(z, g_f, b_f, seq, ("mod", nsc, nsh))
            else:
                out = _layer_norm(z, g_f, b_f, seq, ("shift", nsc, nsh, rw_mu[(layer + 1) // n_mixers]))
                x2, u = out[0], out[1:]
    return x2.reshape(batch, seq, d)
```

```python
import functools

import jax
import jax.numpy as jnp
from jax import lax
from jax.experimental import pallas as pl
from jax.experimental.pallas import tpu as pltpu

F32 = jnp.float32
BF16 = jnp.bfloat16

ML_HEADS = 8
ML_CHUNK = 128
ML_GATE_CAP = 15.0
ML_NORM_EPS = 1e-6
RW_HEAD = 64
RW_CHUNK = 64
RW_GN_EPS = 64e-5
LN_EPS = 1e-5
LANES = 128
VMEM_LIMIT_BYTES = 56 * 1024 * 1024


def _params(*semantics):
    return pltpu.CompilerParams(dimension_semantics=semantics, vmem_limit_bytes=VMEM_LIMIT_BYTES)


def _pick(n, candidates):
    for c in candidates:
        if n % c == 0:
            return c
    return n


def _dot(a, b):
    return jnp.dot(a, b, preferred_element_type=F32)


def _dot_nt(a, b):
    return lax.dot_general(a, b, (((1,), (1,)), ((), ())), preferred_element_type=F32)


def _dot_tn(a, b):
    return lax.dot_general(a, b, (((0,), (0,)), ((), ())), preferred_element_type=F32)


def _split2(x):
    hi = x.astype(BF16)
    lo = (x - hi.astype(F32)).astype(BF16)
    return hi, lo


def _split3(x):
    hi = x.astype(BF16)
    r = x - hi.astype(F32)
    mid = r.astype(BF16)
    lo = (r - mid.astype(F32)).astype(BF16)
    return hi, mid, lo


def _dot_hp(a, b, dot=_dot):
    ah, al = _split2(a)
    bh, bl = _split2(b)
    return dot(ah, bh) + (dot(ah, bl) + dot(al, bh))


def _dot_01(m, x):
    hi, mid, lo = _split3(x)
    return _dot(m, hi) + (_dot(m, mid) + _dot(m, lo))


def _sigmoid(x):
    return 1.0 / (1.0 + jnp.exp(-x))


def _softplus(x):
    return jnp.maximum(x, 0.0) + jnp.log1p(jnp.exp(-jnp.abs(x)))


def _ada_kernel(c_ref, w_ref, b_ref, o_ref):
    c = c_ref[...]
    c_act = (c * _sigmoid(c)).astype(BF16)
    o_ref[0] = _dot(c_act, w_ref[0].astype(BF16)) + b_ref[0]


def _ada(c, ada_w, ada_b):
    depth, d, n = ada_w.shape
    b = c.shape[0]
    tn = _pick(n, (512, 256, 128))
    return pl.pallas_call(
        _ada_kernel,
        out_shape=jax.ShapeDtypeStruct((depth, b, n), F32),
        grid=(depth, n // tn),
        in_specs=[pl.BlockSpec((b, d), lambda l, j: (0, 0)),
                  pl.BlockSpec((1, d, tn), lambda l, j: (l, 0, j)),
                  pl.BlockSpec((1, 1, tn), lambda l, j: (l, 0, j))],
        out_specs=pl.BlockSpec((1, b, tn), lambda l, j: (l, 0, j)),
        compiler_params=_params("parallel", "parallel"),
        name="ada_mod",
    )(c, ada_w, ada_b.reshape(depth, 1, n))


def _modulate_kernel(x_ref, sc_ref, sh_ref, o_ref):
    o_ref[...] = (x_ref[...] * (1.0 + sc_ref[0]) + sh_ref[0]).astype(o_ref.dtype)


def _modulate(x2, sc, sh, seq):
    t, d = x2.shape
    ts = _pick(seq, (512, 256, 128))
    vec = pl.BlockSpec((1, 1, d), lambda i: ((i * ts) // seq, 0, 0))
    return pl.pallas_call(
        _modulate_kernel,
        out_shape=jax.ShapeDtypeStruct((t, d), BF16),
        grid=(t // ts,),
        in_specs=[pl.BlockSpec((ts, d), lambda i: (i, 0)), vec, vec],
        out_specs=pl.BlockSpec((ts, d), lambda i: (i, 0)),
        compiler_params=_params("parallel"),
        name="modulate",
    )(x2, sc, sh)


def _mm_kernel(a_ref, w_ref, b_ref, o_ref, wb_ref, *, act):
    @pl.when(pl.program_id(1) == 0)
    def _():
        wb_ref[...] = w_ref[...].astype(BF16)

    y = _dot(a_ref[...], wb_ref[...]) + b_ref[...]
    o_ref[...] = act(y).astype(o_ref.dtype)


def _mm(a, w, layer, n, bias, act, out_dtype, name):
    t, k = a.shape
    tm = _pick(t, (1024, 512, 256, 128))
    tn = _pick(n, (512, 256, 128))
    return pl.pallas_call(
        functools.partial(_mm_kernel, act=act),
        out_shape=jax.ShapeDtypeStruct((t, n), out_dtype),
        grid=(n // tn, t // tm),
        in_specs=[pl.BlockSpec((tm, k), lambda j, i: (i, 0)),
                  pl.BlockSpec((None, k, tn), lambda j, i: (layer, 0, j)),
                  pl.BlockSpec((1, tn), lambda j, i: (0, j))],
        out_specs=pl.BlockSpec((tm, tn), lambda j, i: (i, j)),
        scratch_shapes=[pltpu.VMEM((k, tn), BF16)],
        compiler_params=_params("parallel", "arbitrary"),
        name=name,
    )(a, w, bias)


def _identity(y):
    return y


def _swiglu_kernel(a_ref, wg_ref, wu_ref, o_ref, wgb_ref, wub_ref):
    @pl.when(pl.program_id(1) == 0)
    def _():
        wgb_ref[...] = wg_ref[...].astype(BF16)
        wub_ref[...] = wu_ref[...].astype(BF16)

    a = a_ref[...]
    gate = _dot(a, wgb_ref[...])
    up = _dot(a, wub_ref[...])
    o_ref[...] = (gate * _sigmoid(gate) * up).astype(o_ref.dtype)


def _mm_swiglu(a, w_in, layer, hidden):
    t, k = a.shape
    tm = _pick(t, (1024, 512, 256, 128))
    tn = _pick(hidden, (256, 128))
    nj = hidden // tn
    return pl.pallas_call(
        _swiglu_kernel,
        out_shape=jax.ShapeDtypeStruct((t, hidden), BF16),
        grid=(nj, t // tm),
        in_specs=[pl.BlockSpec((tm, k), lambda j, i: (i, 0)),
                  pl.BlockSpec((None, k, tn), lambda j, i: (layer, 0, j)),
                  pl.BlockSpec((None, k, tn), lambda j, i: (layer, 0, j + nj))],
        out_specs=pl.BlockSpec((tm, tn), lambda j, i: (i, j)),
        scratch_shapes=[pltpu.VMEM((k, tn), BF16), pltpu.VMEM((k, tn), BF16)],
        compiler_params=_params("parallel", "arbitrary"),
        name="ffn_in_swiglu",
    )(a, w_in, w_in)


def _resid_kernel(a_ref, w_ref, x_ref, gt_ref, o_ref, *scratch, alpha, nk):
    def finish(y):
        o_ref[...] = alpha * x_ref[...] + gt_ref[0] * y

    if nk == 1:
        finish(_dot(a_ref[...], w_ref[...]))
        return
    acc_ref, = scratch
    kk = pl.program_id(2)

    @pl.when(kk == 0)
    def _():
        acc_ref[...] = _dot(a_ref[...], w_ref[...])

    @pl.when(jnp.logical_and(kk > 0, kk < nk - 1))
    def _():
        acc_ref[...] += _dot(a_ref[...], w_ref[...])

    @pl.when(kk == nk - 1)
    def _():
        finish(acc_ref[...] + _dot(a_ref[...], w_ref[...]))


def _mm_resid(a, w, x2, gt, alpha, seq, name):
    t, k = a.shape
    n = w.shape[1]
    tm = _pick(min(t, seq), (1024, 512, 256, 128))
    tk = k if k <= 4096 else _pick(k, (k // 2,))
    nk = k // tk
    tn = _pick(n, (1024, 512, 256, 128)) if nk == 1 else _pick(n, (512, 256, 128))
    scratch = [] if nk == 1 else [pltpu.VMEM((tm, tn), F32)]
    return pl.pallas_call(
        functools.partial(_resid_kernel, alpha=alpha, nk=nk),
        out_shape=jax.ShapeDtypeStruct((t, n), F32),
        grid=(t // tm, n // tn, nk),
        in_specs=[pl.BlockSpec((tm, tk), lambda i, j, kk: (i, kk)),
                  pl.BlockSpec((tk, tn), lambda i, j, kk: (kk, j)),
                  pl.BlockSpec((tm, tn), lambda i, j, kk: (i, j)),
                  pl.BlockSpec((1, 1, tn), lambda i, j, kk: ((i * tm) // seq, 0, j))],
        out_specs=pl.BlockSpec((tm, tn), lambda i, j, kk: (i, j)),
        scratch_shapes=scratch,
        compiler_params=_params("parallel", "parallel", "arbitrary"),
        name=name,
    )(a, w, x2, gt)


def _ln_rows(z, g, b):
    mu = jnp.mean(z, axis=-1, keepdims=True)
    zc = z - mu
    var = jnp.mean(zc * zc, axis=-1, keepdims=True)
    return zc * lax.rsqrt(var + LN_EPS) * g + b


def _ln_kernel(z_ref, g_ref, b_ref, x_ref):
    x_ref[...] = _ln_rows(z_ref[...], g_ref[...], b_ref[...])


def _ln_mod_kernel(z_ref, g_ref, b_ref, sc_ref, sh_ref, x_ref, u_ref):
    x = _ln_rows(z_ref[...], g_ref[...], b_ref[...])
    x_ref[...] = x
    u_ref[...] = (x * (1.0 + sc_ref[0]) + sh_ref[0]).astype(u_ref.dtype)


def _ln_shift_kernel(z_ref, zp_ref, g_ref, b_ref, sc_ref, sh_ref, mu_ref, x_ref, *mix_refs, ts, seq):
    g, b = g_ref[...], b_ref[...]
    scale, shift = 1.0 + sc_ref[0], sh_ref[0]
    x = _ln_rows(z_ref[...], g, b)
    x_ref[...] = x
    u = x * scale + shift
    prev = _ln_rows(zp_ref[...], g, b)[7:8, :] * scale + shift
    first = (pl.program_id(0) * ts) % seq == 0
    prev = jnp.where(first, 0.0, prev)
    row = lax.broadcasted_iota(jnp.int32, u.shape, 0)
    u_prev = jnp.where(row == 0, prev, pltpu.roll(u, shift=1, axis=0))
    xx = u_prev - u
    for j, ref in enumerate(mix_refs):
        ref[...] = (u + xx * mu_ref[j:j + 1, :]).astype(ref.dtype)


def _layer_norm(z, g, b, seq, nxt=None):
    t, d = z.shape
    row = pl.BlockSpec((1, d), lambda i: (0, 0))
    if nxt is None:
        ts = _pick(seq, (256, 128))
        tile = pl.BlockSpec((ts, d), lambda i: (i, 0))
        return pl.pallas_call(
            _ln_kernel, out_shape=jax.ShapeDtypeStruct((t, d), F32), grid=(t // ts,),
            in_specs=[tile, row, row], out_specs=tile,
            compiler_params=_params("parallel"), name="layer_norm",
        )(z, g, b)
    if nxt[0] == "mod":
        ts = _pick(seq, (256, 128))
        tile = pl.BlockSpec((ts, d), lambda i: (i, 0))
        vec = pl.BlockSpec((1, 1, d), lambda i: ((i * ts) // seq, 0, 0))
        return pl.pallas_call(
            _ln_mod_kernel,
            out_shape=(jax.ShapeDtypeStruct((t, d), F32), jax.ShapeDtypeStruct((t, d), BF16)),
            grid=(t // ts,),
            in_specs=[tile, row, row, vec, vec], out_specs=(tile, tile),
            compiler_params=_params("parallel"), name="layer_norm_mod",
        )(z, g, b, nxt[1], nxt[2])
    _, sc, sh, mu = nxt
    nmix = mu.shape[0]
    ts = _pick(seq, (128,))
    sub = 8
    tile = pl.BlockSpec((ts, d), lambda i: (i, 0))
    prev = pl.BlockSpec((sub, d), lambda i: (jnp.maximum(i * (ts // sub) - 1, 0), 0))
    vec = pl.BlockSpec((1, 1, d), lambda i: ((i * ts) // seq, 0, 0))
    return pl.pallas_call(
        functools.partial(_ln_shift_kernel, ts=ts, seq=seq),
        out_shape=(jax.ShapeDtypeStruct((t, d), F32),) + (jax.ShapeDtypeStruct((t, d), BF16),) * nmix,
        grid=(t // ts,),
        in_specs=[tile, prev, row, row, vec, vec, pl.BlockSpec((nmix, d), lambda i: (0, 0))],
        out_specs=(tile,) * (1 + nmix),
        compiler_params=_params("parallel"), name="layer_norm_shift",
    )(z, z, g, b, sc, sh, mu)


def _ml_gate_act(y):
    lane = lax.broadcasted_iota(jnp.int32, y.shape, 1)
    i_pre = ML_GATE_CAP * jnp.tanh(y / ML_GATE_CAP)
    log_f = -_softplus(-y)
    return jnp.where(lane < ML_HEADS, i_pre, log_f)


def _mlstm_kernel(q_ref, k_ref, v_ref, o_ref, icol_ref, fcol_ref, irow_ref, frow_ref, ng_ref, out_ref,
                  c_sc, n_sc, m_sc):
    L = ML_CHUNK

    @pl.when(pl.program_id(2) == 0)
    def _():
        c_sc[...] = jnp.zeros_like(c_sc)
        n_sc[...] = jnp.zeros_like(n_sc)
        m_sc[...] = jnp.zeros_like(m_sc)

    q, k, v = q_ref[...], k_ref[...], v_ref[...]
    dqk = q.shape[1]
    k_scale = dqk ** -0.5
    i_col, f_col = icol_ref[...], fcol_ref[...]
    i_row, f_row = irow_ref[0], frow_ref[0]
    c_st, n_st, m_st = c_sc[...], n_sc[...], m_sc[...]

    r_idx = lax.broadcasted_iota(jnp.int32, (L, L), 0)
    c_idx = lax.broadcasted_iota(jnp.int32, (L, L), 1)
    causal = r_idx >= c_idx
    g_col = jnp.sum(jnp.where(causal, f_row, 0.0), axis=1, keepdims=True)
    g_row = jnp.sum(jnp.where(r_idx <= c_idx, f_col, 0.0), axis=0, keepdims=True)
    g_last = jnp.sum(f_row, axis=1, keepdims=True)

    log_d = jnp.where(causal, g_col - g_row + i_row, -jnp.inf)
    log_inter = g_col + m_st
    m_row = jnp.maximum(jnp.max(log_d, axis=1, keepdims=True), log_inter)
    scores = _dot_nt(q, k) * k_scale * jnp.exp(log_d - m_row)
    inter = jnp.exp(log_inter - m_row)
    num = _dot(scores.astype(BF16), v) + inter * _dot(q, c_st.astype(BF16))
    qn = jnp.sum(q.astype(F32) * n_st, axis=1, keepdims=True)
    den = jnp.sum(scores, axis=1, keepdims=True) + inter * qn
    h = num / jnp.maximum(jnp.abs(den), jnp.exp(-m_row))
    h = h * lax.rsqrt(jnp.mean(h * h, axis=1, keepdims=True) + ML_NORM_EPS)
    out_ref[...] = (h * ng_ref[...] * _sigmoid(o_ref[...].astype(F32))).astype(out_ref.dtype)

    log_w = g_last - g_col + i_col
    m_new = jnp.maximum(g_last + m_st, jnp.max(log_w, axis=0, keepdims=True))
    wk = k.astype(F32) * (k_scale * jnp.exp(log_w - m_new))
    decay = jnp.exp(g_last + m_st - m_new)
    c_sc[...] = decay * c_st + _dot_tn(wk.astype(BF16), v)
    n_sc[...] = decay * n_st + jnp.sum(wk, axis=0, keepdims=True)
    m_sc[...] = m_new


def _mlstm(proj, gates, norm_g, batch, seq):
    t = proj.shape[0]
    H, L = ML_HEADS, ML_CHUNK
    dv = norm_g.shape[1] // H
    dqk = dv // 2
    nc = seq // L
    g = gates[:, :2 * H].reshape(batch, seq, 2 * H).transpose(0, 2, 1)
    i_g, f_g = g[:, :H], g[:, H:]
    col = lambda a: a.reshape(batch, H, seq, 1)
    rowv = lambda a: a.reshape(batch * H * nc, 1, L)
    col_spec = pl.BlockSpec((None, None, L, 1), lambda b, h, c: (b, h, c, 0))
    row_spec = pl.BlockSpec((1, 1, L), lambda b, h, c: ((b * H + h) * nc + c, 0, 0))
    tok = lambda b, c: b * nc + c
    return pl.pallas_call(
        _mlstm_kernel,
        out_shape=jax.ShapeDtypeStruct((t, H * dv), BF16),
        grid=(batch, H, nc),
        in_specs=[pl.BlockSpec((L, dqk), lambda b, h, c: (tok(b, c), h)),
                  pl.BlockSpec((L, dqk), lambda b, h, c: (tok(b, c), H + h)),
                  pl.BlockSpec((L, dv), lambda b, h, c: (tok(b, c), H + h)),
                  pl.BlockSpec((L, dv), lambda b, h, c: (tok(b, c), 2 * H + h)),
                  col_spec, col_spec, row_spec, row_spec,
                  pl.BlockSpec((1, dv), lambda b, h, c: (0, h))],
        out_specs=pl.BlockSpec((L, dv), lambda b, h, c: (tok(b, c), h)),
        scratch_shapes=[pltpu.VMEM((dqk, dv), F32), pltpu.VMEM((1, dqk), F32), pltpu.VMEM((1, 1), F32)],
        compiler_params=_params("parallel", "parallel", "arbitrary"),
        name="mlstm_chunk",
    )(proj, proj, proj, proj, col(i_g), col(f_g), rowv(i_g), rowv(f_g), norm_g)


def _rwkv_kernel(r_ref, k_ref, v_ref, w_ref, a_ref, g_ref, kk_ref, ka_ref, rk_ref, lg_ref, lb_ref, o_ref, s_sc,
                 *, heads):
    L, N = RW_CHUNK, RW_HEAD
    gw = heads * N

    @pl.when(pl.program_id(2) == 0)
    def _():
        s_sc[...] = jnp.zeros_like(s_sc)

    P = 2 * L
    pairs = heads // 2

    r, k, v = r_ref[...], k_ref[...], v_ref[...]
    log_decay = -jnp.exp(-_softplus(-w_ref[...]) - 0.5)
    a = _sigmoid(a_ref[...])

    sw = min(gw, 256)
    li = lax.broadcasted_iota(jnp.int32, (sw, sw), 0) // N
    lj = lax.broadcasted_iota(jnp.int32, (sw, sw), 1) // N
    head_ones = jnp.where(li == lj, 1.0, 0.0).astype(BF16)

    def head_sum(x, split):
        parts = split(x)
        cols = []
        for c0 in range(0, gw, sw):
            acc = None
            for p in parts:
                d = _dot(p[:, c0:c0 + sw], head_ones)
                acc = d if acc is None else acc + d
            cols.append(acc)
        return cols[0] if len(cols) == 1 else jnp.concatenate(cols, axis=1)

    one_pass = lambda x: (x.astype(BF16),)

    t_i = lax.broadcasted_iota(jnp.int32, (L, L), 0)
    t_j = lax.broadcasted_iota(jnp.int32, (L, L), 1)
    tri = jnp.where(t_i >= t_j, 1.0, 0.0).astype(BF16)

    kkr = k * kk_ref[...]
    k = k * (1.0 + (a - 1.0) * ka_ref[...])
    sums = head_sum(jnp.concatenate([kkr * kkr, r * k * rk_ref[...]], axis=0), one_pass)
    kk = kkr * lax.rsqrt(jnp.maximum(sums[:L], 1e-24))
    bonus = sums[L:] * v

    cl = _dot_01(tri, log_decay)
    cl_last = cl[L - 1:L, :]
    gam = jnp.exp(cl)
    inv_gam = jnp.exp(-cl)
    gam_end = jnp.exp(cl_last)
    to_end = gam_end * inv_gam
    kka = kk * a
    r_hat = r * gam
    a_hat = -kk * jnp.exp(cl - log_decay)
    b_hat = kka * inv_gam
    k_hat = k * inv_gam
    b_end = kka * to_end
    k_end = k * to_end

    p_i = lax.broadcasted_iota(jnp.int32, (P, P), 0)
    p_j = lax.broadcasted_iota(jnp.int32, (P, P), 1)
    same = p_i // L == p_j // L
    strict = jnp.logical_and(same, p_i > p_j)
    incl = jnp.logical_and(same, p_i >= p_j)
    eye = jnp.where(p_i == p_j, 1.0, 0.0)
    levels = []
    s = 2
    while s < L:
        levels.append(jnp.logical_and(p_i // (2 * s) == p_j // (2 * s),
                                      jnp.logical_and(p_i % (2 * s) >= s, p_j % (2 * s) < s)))
        s *= 2
    own = (lax.broadcasted_iota(jnp.int32, (P, 2 * N), 0) // L
           == lax.broadcasted_iota(jnp.int32, (P, 2 * N), 1) // N)

    def stack(x):
        return jnp.where(own, jnp.concatenate([x, x], axis=0), 0.0).astype(BF16)

    rng = range(pairs)
    sls = [slice(p * 2 * N, (p + 1) * 2 * N) for p in rng]
    st = [s_sc[p] for p in rng]
    ar_s = [jnp.concatenate([stack(a_hat[:, sl]), stack(r_hat[:, sl])], axis=0) for sl in sls]
    v_s = [stack(v[:, sl]) for sl in sls]
    m4 = [_dot_nt(ar_s[p], jnp.concatenate([stack(b_hat[:, sls[p]]), stack(k_hat[:, sls[p]])], axis=0))
          for p in rng]
    a_ab = [jnp.where(strict, m[:P, :P], 0.0) for m in m4]
    a_ak = [jnp.where(strict, m[:P, P:], 0.0).astype(BF16) for m in m4]
    a_r = [jnp.concatenate([jnp.where(incl, m[P:, :P], 0.0), jnp.where(incl, m[P:, P:], 0.0)], axis=1).astype(BF16)
           for m in m4]
    from_state = [_dot_nt(ar_s[p], st[p].astype(BF16)) for p in rng]
    rhs_u = [from_state[p][:P] + _dot(a_ak[p], v_s[p]) for p in rng]
    x = [eye + jnp.where(p_i // 2 == p_j // 2, m, 0.0) for m in a_ab]
    for blk in levels:
        xb = [xp.astype(BF16) for xp in x]
        half = [_dot(xb[p], jnp.where(blk, a_ab[p], 0.0).astype(BF16)).astype(BF16) for p in rng]
        x = [x[p] + _dot(half[p], xb[p]) for p in rng]
    su = [_dot(x[p].astype(BF16), rhs_u[p].astype(BF16)) for p in rng]
    su_v = [jnp.concatenate([su[p].astype(BF16), v_s[p]], axis=0) for p in rng]
    y_s = [from_state[p][P:] + _dot(a_r[p], su_v[p]) for p in rng]
    for p in rng:
        be = jnp.concatenate([stack(b_end[:, sls[p]]), stack(k_end[:, sls[p]])], axis=0)
        s_sc[p] = st[p] * gam_end[:, sls[p]] + _dot_tn(su_v[p], be)
    ys = [yp[:L] + yp[L:] for yp in y_s]
    y = ys[0] if pairs == 1 else jnp.concatenate(ys, axis=1)

    inv_n = 1.0 / N
    mean = head_sum(y, _split2) * inv_n
    yc = y - mean
    var = head_sum(yc * yc, _split2) * inv_n
    yn = yc * lax.rsqrt(var + RW_GN_EPS) * lg_ref[...] + lb_ref[...]
    o_ref[...] = ((yn + bonus) * g_ref[...]).astype(o_ref.dtype)


def _rwkv(r, k, v, w, a, g, k_k, k_a, r_k, lnx_g, lnx_b, batch, seq):
    t, d = r.shape
    L = RW_CHUNK
    heads = 16
    gw = heads * RW_HEAD
    nc = seq // L
    tile = pl.BlockSpec((L, gw), lambda b, h, c: (b * nc + c, h))
    row = pl.BlockSpec((1, gw), lambda b, h, c: (0, h))
    return pl.pallas_call(
        functools.partial(_rwkv_kernel, heads=heads),
        out_shape=jax.ShapeDtypeStruct((t, d), BF16),
        grid=(batch, d // gw, nc),
        in_specs=[tile] * 6 + [row] * 5,
        out_specs=tile,
        scratch_shapes=[pltpu.VMEM((heads // 2, 2 * RW_HEAD, 2 * RW_HEAD), F32)],
        compiler_params=_params("parallel", "parallel", "arbitrary"),
        name="rwkv7_chunk",
    )(r, k, v, w, a, g, k_k, k_a, r_k, lnx_g, lnx_b)


def kernel(x, c, ada_w, ada_b, mix_ln_g, mix_ln_b, ffn_ln_g, ffn_ln_b, ffn_w_in, ffn_w_out, ml_w_in, ml_b_i, ml_b_f, ml_norm_g, ml_w_out, rw_mu, rw_w_r, rw_w_k, rw_w_v, rw_w0, rw_w1, rw_w2, rw_a0, rw_a1, rw_a2, rw_g1, rw_g2, rw_k_k, rw_k_a, rw_r_k, rw_lnx_g, rw_lnx_b, rw_w_o):
    batch, seq, d = x.shape
    depth = ada_w.shape[0]
    t = batch * seq
    alpha = (2 * depth) ** 0.25
    hidden = ffn_w_out.shape[1]
    n_mixers = 2

    mod = _ada(c, ada_w, ada_b).reshape(depth, batch, 6, 1, d)
    row = lambda p: p.reshape(1, -1)
    zero_bias = lambda n: jnp.zeros((1, n), F32)

    x2 = x.reshape(t, d)
    u = None
    for layer in range(depth):
        sh_m, sc_m, gt_m, sh_f, sc_f, gt_f = [mod[layer, :, i] for i in range(6)]
        j = layer // n_mixers
        if layer % n_mixers == 0:
            if u is None:
                u = _modulate(x2, sc_m, sh_m, seq)
            qkvo = ml_w_in.shape[2] - 2 * ML_HEADS
            proj = _mm(u, ml_w_in, j, qkvo, zero_bias(qkvo), _identity, BF16, "mlstm_in")
            w_gate = jnp.pad(ml_w_in[j:j + 1, :, qkvo:], ((0, 0), (0, 0), (0, LANES - 2 * ML_HEADS)))
            b_gate = jnp.pad(jnp.concatenate([ml_b_i[j], ml_b_f[j]]), (0, LANES - 2 * ML_HEADS)).reshape(1, LANES)
            gates = _mm(u, w_gate, 0, LANES, b_gate, _ml_gate_act, F32, "mlstm_gates")
            y = _mlstm(proj, gates, row(ml_norm_g[j]), batch, seq)
            z = _mm_resid(y, ml_w_out[j].astype(BF16), x2, gt_m, alpha, seq, "mlstm_out")
            x2, u = _layer_norm(z, row(mix_ln_g[layer]), row(mix_ln_b[layer]), seq, ("mod", sc_f, sh_f))
        else:
            xr, xw, xk, xv, xa, xg = u
            nb = zero_bias
            r = _mm(xr, rw_w_r, j, d, nb(d), _identity, F32, "rwkv_r")
            k = _mm(xk, rw_w_k, j, d, nb(d), _identity, F32, "rwkv_k")
            v = _mm(xv, rw_w_v, j, d, nb(d), _identity, F32, "rwkv_v")
            n_w, n_a, n_g = rw_w1.shape[2], rw_a1.shape[2], rw_g1.shape[2]
            lw = _mm(xw, rw_w1, j, n_w, nb(n_w), jnp.tanh, BF16, "rwkv_w1")
            la = _mm(xa, rw_a1, j, n_a, nb(n_a), _identity, BF16, "rwkv_a1")
            lg = _mm(xg, rw_g1, j, n_g, nb(n_g), _sigmoid, BF16, "rwkv_g1")
            w = _mm(lw, rw_w2, j, d, row(rw_w0[j]), _identity, F32, "rwkv_w2")
            a = _mm(la, rw_a2, j, d, row(rw_a0[j]), _identity, F32, "rwkv_a2")
            g = _mm(lg, rw_g2, j, d, nb(d), _identity, F32, "rwkv_g2")
            y = _rwkv(r, k, v, w, a, g, row(rw_k_k[j]), row(rw_k_a[j]), row(rw_r_k[j]),
                      row(rw_lnx_g[j]), row(rw_lnx_b[j]), batch, seq)
            z = _mm_resid(y, rw_w_o[j].astype(BF16), x2, gt_m, alpha, seq, "rwkv_out")
            x2, u = _layer_norm(z, row(mix_ln_g[layer]), row(mix_ln_b[layer]), seq, ("mod", sc_f, sh_f))
        hid = _mm_swiglu(u, ffn_w_in, layer, hidden)
        z = _mm_resid(hid, ffn_w_out[layer].astype(BF16), x2, gt_f, alpha, seq, "ffn_out")
        g_f, b_f = row(ffn_ln_g[layer]), row(ffn_ln_b[layer])
        if layer + 1 == depth:
            x2, u = _layer_norm(z, g_f, b_f, seq), None
        else:
            nsh, nsc = mod[layer + 1, :, 0], mod[layer + 1, :, 1]
            if (layer + 1) % n_mixers == 0:
                x2, u = _layer_norm(z, g_f, b_f, seq, ("mod", nsc, nsh))
            else:
                out = _layer_norm(z, g_f, b_f, seq, ("shift", nsc, nsh, rw_mu[(layer + 1) // n_mixers]))
                x2, u = out[0], out[1:]
    return x2.reshape(batch, seq, d)
```

```python
import functools

import jax
import jax.numpy as jnp
from jax import lax
from jax.experimental import pallas as pl
from jax.experimental.pallas import tpu as pltpu

F32 = jnp.float32
BF16 = jnp.bfloat16

ML_HEADS = 8
ML_CHUNK = 128
ML_GATE_CAP = 15.0
ML_NORM_EPS = 1e-6
RW_HEAD = 64
RW_CHUNK = 64
RW_GN_EPS = 64e-5
LN_EPS = 1e-5
LANES = 128
VMEM_LIMIT_BYTES = 56 * 1024 * 1024


def _params(*semantics):
    return pltpu.CompilerParams(dimension_semantics=semantics, vmem_limit_bytes=VMEM_LIMIT_BYTES)


def _pick(n, candidates):
    for c in candidates:
        if n % c == 0:
            return c
    return n


def _dot(a, b):
    return jnp.dot(a, b, preferred_element_type=F32)


def _dot_nt(a, b):
    return lax.dot_general(a, b, (((1,), (1,)), ((), ())), preferred_element_type=F32)


def _dot_tn(a, b):
    return lax.dot_general(a, b, (((0,), (0,)), ((), ())), preferred_element_type=F32)


def _split2(x):
    hi = x.astype(BF16)
    lo = (x - hi.astype(F32)).astype(BF16)
    return hi, lo


def _split3(x):
    hi = x.astype(BF16)
    r = x - hi.astype(F32)
    mid = r.astype(BF16)
    lo = (r - mid.astype(F32)).astype(BF16)
    return hi, mid, lo


def _dot_hp(a, b, dot=_dot):
    ah, al = _split2(a)
    bh, bl = _split2(b)
    return dot(ah, bh) + (dot(ah, bl) + dot(al, bh))


def _dot_01(m, x):
    hi, mid, lo = _split3(x)
    return _dot(m, hi) + (_dot(m, mid) + _dot(m, lo))


def _sigmoid(x):
    return 1.0 / (1.0 + jnp.exp(-x))


def _softplus(x):
    return jnp.maximum(x, 0.0) + jnp.log1p(jnp.exp(-jnp.abs(x)))


def _ada_kernel(c_ref, w_ref, b_ref, o_ref):
    c = c_ref[...]
    c_act = (c * _sigmoid(c)).astype(BF16)
    o_ref[0] = _dot(c_act, w_ref[0].astype(BF16)) + b_ref[0]


def _ada(c, ada_w, ada_b):
    depth, d, n = ada_w.shape
    b = c.shape[0]
    tn = _pick(n, (512, 256, 128))
    return pl.pallas_call(
        _ada_kernel,
        out_shape=jax.ShapeDtypeStruct((depth, b, n), F32),
        grid=(depth, n // tn),
        in_specs=[pl.BlockSpec((b, d), lambda l, j: (0, 0)),
                  pl.BlockSpec((1, d, tn), lambda l, j: (l, 0, j)),
                  pl.BlockSpec((1, 1, tn), lambda l, j: (l, 0, j))],
        out_specs=pl.BlockSpec((1, b, tn), lambda l, j: (l, 0, j)),
        compiler_params=_params("parallel", "parallel"),
        name="ada_mod",
    )(c, ada_w, ada_b.reshape(depth, 1, n))


def _modulate_kernel(x_ref, sc_ref, sh_ref, o_ref):
    o_ref[...] = (x_ref[...] * (1.0 + sc_ref[0]) + sh_ref[0]).astype(o_ref.dtype)


def _modulate(x2, sc, sh, seq):
    t, d = x2.shape
    ts = _pick(seq, (512, 256, 128))
    vec = pl.BlockSpec((1, 1, d), lambda i: ((i * ts) // seq, 0, 0))
    return pl.pallas_call(
        _modulate_kernel,
        out_shape=jax.ShapeDtypeStruct((t, d), BF16),
        grid=(t // ts,),
        in_specs=[pl.BlockSpec((ts, d), lambda i: (i, 0)), vec, vec],
        out_specs=pl.BlockSpec((ts, d), lambda i: (i, 0)),
        compiler_params=_params("parallel"),
        name="modulate",
    )(x2, sc, sh)


def _mm_kernel(a_ref, w_ref, b_ref, o_ref, wb_ref, *, act):
    @pl.when(pl.program_id(1) == 0)
    def _():
        wb_ref[...] = w_ref[...].astype(BF16)

    y = _dot(a_ref[...], wb_ref[...]) + b_ref[...]
    o_ref[...] = act(y).astype(o_ref.dtype)


def _mm(a, w, layer, n, bias, act, out_dtype, name):
    t, k = a.shape
    tm = _pick(t, (1024, 512, 256, 128))
    tn = _pick(n, (512, 256, 128))
    return pl.pallas_call(
        functools.partial(_mm_kernel, act=act),
        out_shape=jax.ShapeDtypeStruct((t, n), out_dtype),
        grid=(n // tn, t // tm),
        in_specs=[pl.BlockSpec((tm, k), lambda j, i: (i, 0)),
                  pl.BlockSpec((None, k, tn), lambda j, i: (layer, 0, j)),
                  pl.BlockSpec((1, tn), lambda j, i: (0, j))],
        out_specs=pl.BlockSpec((tm, tn), lambda j, i: (i, j)),
        scratch_shapes=[pltpu.VMEM((k, tn), BF16)],
        compiler_params=_params("parallel", "arbitrary"),
        name=name,
    )(a, w, bias)


def _identity(y):
    return y


def _swiglu_kernel(a_ref, wg_ref, wu_ref, o_ref, wgb_ref, wub_ref):
    @pl.when(pl.program_id(1) == 0)
    def _():
        wgb_ref[...] = wg_ref[...].astype(BF16)
        wub_ref[...] = wu_ref[...].astype(BF16)

    a = a_ref[...]
    gate = _dot(a, wgb_ref[...])
    up = _dot(a, wub_ref[...])
    o_ref[...] = (gate * _sigmoid(gate) * up).astype(o_ref.dtype)


def _mm_swiglu(a, w_in, layer, hidden):
    t, k = a.shape
    tm = _pick(t, (1024, 512, 256, 128))
    tn = _pick(hidden, (256, 128))
    nj = hidden // tn
    return pl.pallas_call(
        _swiglu_kernel,
        out_shape=jax.ShapeDtypeStruct((t, hidden), BF16),
        grid=(nj, t // tm),
        in_specs=[pl.BlockSpec((tm, k), lambda j, i: (i, 0)),
                  pl.BlockSpec((None, k, tn), lambda j, i: (layer, 0, j)),
                  pl.BlockSpec((None, k, tn), lambda j, i: (layer, 0, j + nj))],
        out_specs=pl.BlockSpec((tm, tn), lambda j, i: (i, j)),
        scratch_shapes=[pltpu.VMEM((k, tn), BF16), pltpu.VMEM((k, tn), BF16)],
        compiler_params=_params("parallel", "arbitrary"),
        name="ffn_in_swiglu",
    )(a, w_in, w_in)


def _resid_kernel(a_ref, w_ref, x_ref, gt_ref, o_ref, *scratch, alpha, nk):
    def finish(y):
        o_ref[...] = alpha * x_ref[...] + gt_ref[0] * y

    if nk == 1:
        finish(_dot(a_ref[...], w_ref[...]))
        return
    acc_ref, = scratch
    kk = pl.program_id(2)

    @pl.when(kk == 0)
    def _():
        acc_ref[...] = _dot(a_ref[...], w_ref[...])

    @pl.when(jnp.logical_and(kk > 0, kk < nk - 1))
    def _():
        acc_ref[...] += _dot(a_ref[...], w_ref[...])

    @pl.when(kk == nk - 1)
    def _():
        finish(acc_ref[...] + _dot(a_ref[...], w_ref[...]))


def _mm_resid(a, w, layer, x2, gt, alpha, seq, name):
    t, k = a.shape
    n = w.shape[2]
    tm = _pick(min(t, seq), (1024, 512, 256, 128))
    tk = k if k <= 4096 else _pick(k, (k // 2,))
    nk = k // tk
    tn = _pick(n, (1024, 512, 256, 128)) if nk == 1 else _pick(n, (512, 256, 128))
    scratch = [] if nk == 1 else [pltpu.VMEM((tm, tn), F32)]
    return pl.pallas_call(
        functools.partial(_resid_kernel, alpha=alpha, nk=nk),
        out_shape=jax.ShapeDtypeStruct((t, n), F32),
        grid=(t // tm, n // tn, nk),
        in_specs=[pl.BlockSpec((tm, tk), lambda i, j, kk: (i, kk)),
                  pl.BlockSpec((None, tk, tn), lambda i, j, kk: (layer, kk, j)),
                  pl.BlockSpec((tm, tn), lambda i, j, kk: (i, j)),
                  pl.BlockSpec((1, 1, tn), lambda i, j, kk: ((i * tm) // seq, 0, j))],
        out_specs=pl.BlockSpec((tm, tn), lambda i, j, kk: (i, j)),
        scratch_shapes=scratch,
        compiler_params=_params("parallel", "parallel", "arbitrary"),
        name=name,
    )(a, w, x2, gt)


def _ln_rows(z, g, b):
    mu = jnp.mean(z, axis=-1, keepdims=True)
    zc = z - mu
    var = jnp.mean(zc * zc, axis=-1, keepdims=True)
    return zc * lax.rsqrt(var + LN_EPS) * g + b


def _ln_kernel(z_ref, g_ref, b_ref, x_ref):
    x_ref[...] = _ln_rows(z_ref[...], g_ref[...], b_ref[...])


def _ln_mod_kernel(z_ref, g_ref, b_ref, sc_ref, sh_ref, x_ref, u_ref):
    x = _ln_rows(z_ref[...], g_ref[...], b_ref[...])
    x_ref[...] = x
    u_ref[...] = (x * (1.0 + sc_ref[0]) + sh_ref[0]).astype(u_ref.dtype)


def _ln_shift_kernel(z_ref, zp_ref, g_ref, b_ref, sc_ref, sh_ref, mu_ref, x_ref, *mix_refs, ts, seq):
    g, b = g_ref[...], b_ref[...]
    scale, shift = 1.0 + sc_ref[0], sh_ref[0]
    x = _ln_rows(z_ref[...], g, b)
    x_ref[...] = x
    u = x * scale + shift
    prev = _ln_rows(zp_ref[...], g, b)[7:8, :] * scale + shift
    first = (pl.program_id(0) * ts) % seq == 0
    prev = jnp.where(first, 0.0, prev)
    row = lax.broadcasted_iota(jnp.int32, u.shape, 0)
    u_prev = jnp.where(row == 0, prev, pltpu.roll(u, shift=1, axis=0))
    xx = u_prev - u
    for j, ref in enumerate(mix_refs):
        ref[...] = (u + xx * mu_ref[j:j + 1, :]).astype(ref.dtype)


def _layer_norm(z, g, b, seq, nxt=None):
    t, d = z.shape
    row = pl.BlockSpec((1, d), lambda i: (0, 0))
    if nxt is None:
        ts = _pick(seq, (256, 128))
        tile = pl.BlockSpec((ts, d), lambda i: (i, 0))
        return pl.pallas_call(
            _ln_kernel, out_shape=jax.ShapeDtypeStruct((t, d), F32), grid=(t // ts,),
            in_specs=[tile, row, row], out_specs=tile,
            compiler_params=_params("parallel"), name="layer_norm",
        )(z, g, b)
    if nxt[0] == "mod":
        ts = _pick(seq, (256, 128))
        tile = pl.BlockSpec((ts, d), lambda i: (i, 0))
        vec = pl.BlockSpec((1, 1, d), lambda i: ((i * ts) // seq, 0, 0))
        return pl.pallas_call(
            _ln_mod_kernel,
            out_shape=(jax.ShapeDtypeStruct((t, d), F32), jax.ShapeDtypeStruct((t, d), BF16)),
            grid=(t // ts,),
            in_specs=[tile, row, row, vec, vec], out_specs=(tile, tile),
            compiler_params=_params("parallel"), name="layer_norm_mod",
        )(z, g, b, nxt[1], nxt[2])
    _, sc, sh, mu = nxt
    nmix = mu.shape[0]
    ts = _pick(seq, (128,))
    sub = 8
    tile = pl.BlockSpec((ts, d), lambda i: (i, 0))
    prev = pl.BlockSpec((sub, d), lambda i: (jnp.maximum(i * (ts // sub) - 1, 0), 0))
    vec = pl.BlockSpec((1, 1, d), lambda i: ((i * ts) // seq, 0, 0))
    return pl.pallas_call(
        functools.partial(_ln_shift_kernel, ts=ts, seq=seq),
        out_shape=(jax.ShapeDtypeStruct((t, d), F32),) + (jax.ShapeDtypeStruct((t, d), BF16),) * nmix,
        grid=(t // ts,),
        in_specs=[tile, prev, row, row, vec, vec, pl.BlockSpec((nmix, d), lambda i: (0, 0))],
        out_specs=(tile,) * (1 + nmix),
        compiler_params=_params("parallel"), name="layer_norm_shift",
    )(z, z, g, b, sc, sh, mu)


def _ml_gate_act(y):
    lane = lax.broadcasted_iota(jnp.int32, y.shape, 1)
    i_pre = ML_GATE_CAP * jnp.tanh(y / ML_GATE_CAP)
    log_f = -_softplus(-y)
    return jnp.where(lane < ML_HEADS, i_pre, log_f)


def _mlstm_kernel(q_ref, k_ref, v_ref, o_ref, icol_ref, fcol_ref, irow_ref, frow_ref, ng_ref, out_ref,
                  c_sc, n_sc, m_sc, *, heads):
    L = ML_CHUNK

    @pl.when(pl.program_id(2) == 0)
    def _():
        c_sc[...] = jnp.zeros_like(c_sc)
        n_sc[...] = jnp.zeros_like(n_sc)
        m_sc[...] = jnp.zeros_like(m_sc)

    dqk = q_ref.shape[1] // heads
    dv = v_ref.shape[1] // heads
    k_scale = dqk ** -0.5
    r_idx = lax.broadcasted_iota(jnp.int32, (L, L), 0)
    c_idx = lax.broadcasted_iota(jnp.int32, (L, L), 1)
    causal = r_idx >= c_idx
    anti = r_idx <= c_idx

    hs = range(heads)
    q = [q_ref[:, h * dqk:(h + 1) * dqk] for h in hs]
    k = [k_ref[:, h * dqk:(h + 1) * dqk] for h in hs]
    v = [v_ref[:, h * dv:(h + 1) * dv] for h in hs]
    i_col, f_col = [icol_ref[h] for h in hs], [fcol_ref[h] for h in hs]
    i_row, f_row = [irow_ref[h] for h in hs], [frow_ref[h] for h in hs]
    c_st, n_st, m_st = [c_sc[h] for h in hs], [n_sc[h] for h in hs], [m_sc[h] for h in hs]

    qk = [_dot_nt(q[h], k[h]) for h in hs]
    q_c = [_dot(q[h], c_st[h].astype(BF16)) for h in hs]
    g_col = [jnp.sum(jnp.where(causal, f_row[h], 0.0), axis=1, keepdims=True) for h in hs]
    g_row = [jnp.sum(jnp.where(anti, f_col[h], 0.0), axis=0, keepdims=True) for h in hs]
    g_last = [jnp.sum(f_row[h], axis=1, keepdims=True) for h in hs]
    log_d = [jnp.where(causal, g_col[h] - g_row[h] + i_row[h], -jnp.inf) for h in hs]
    log_inter = [g_col[h] + m_st[h] for h in hs]
    m_row = [jnp.maximum(jnp.max(log_d[h], axis=1, keepdims=True), log_inter[h]) for h in hs]
    scores = [qk[h] * k_scale * jnp.exp(log_d[h] - m_row[h]) for h in hs]
    inter = [jnp.exp(log_inter[h] - m_row[h]) for h in hs]
    num = [_dot(scores[h].astype(BF16), v[h]) + inter[h] * q_c[h] for h in hs]
    q_n = [jnp.sum(q[h].astype(F32) * n_st[h], axis=1, keepdims=True) for h in hs]
    den = [jnp.sum(scores[h], axis=1, keepdims=True) + inter[h] * q_n[h] for h in hs]
    for h in hs:
        hid = num[h] / jnp.maximum(jnp.abs(den[h]), jnp.exp(-m_row[h]))
        hid = hid * lax.rsqrt(jnp.mean(hid * hid, axis=1, keepdims=True) + ML_NORM_EPS)
        sl = slice(h * dv, (h + 1) * dv)
        out_ref[:, sl] = (hid * ng_ref[:, sl] * _sigmoid(o_ref[:, sl].astype(F32))).astype(out_ref.dtype)

    log_w = [g_last[h] - g_col[h] + i_col[h] for h in hs]
    m_new = [jnp.maximum(g_last[h] + m_st[h], jnp.max(log_w[h], axis=0, keepdims=True)) for h in hs]
    wk = [k[h].astype(F32) * (k_scale * jnp.exp(log_w[h] - m_new[h])) for h in hs]
    kv = [_dot_tn(wk[h].astype(BF16), v[h]) for h in hs]
    for h in hs:
        decay = jnp.exp(g_last[h] + m_st[h] - m_new[h])
        c_sc[h] = decay * c_st[h] + kv[h]
        n_sc[h] = decay * n_st[h] + jnp.sum(wk[h], axis=0, keepdims=True)
        m_sc[h] = m_new[h]


def _mlstm(proj, gates, norm_g, batch, seq):
    t = proj.shape[0]
    H, L = ML_HEADS, ML_CHUNK
    G = 4
    dv = norm_g.shape[1] // H
    dqk = dv // 2
    nc = seq // L
    ng = H // G
    g = gates[:, :2 * H].reshape(batch, seq, 2 * H).transpose(0, 2, 1)
    i_g, f_g = g[:, :H], g[:, H:]
    col = lambda a: a.reshape(batch, H, seq, 1)
    rowv = lambda a: a.reshape(batch, H, nc, 1, L)
    col_spec = pl.BlockSpec((None, G, L, 1), lambda b, h, c: (b, h, c, 0))
    row_spec = pl.BlockSpec((None, G, None, 1, L), lambda b, h, c: (b, h, c, 0, 0))
    tok = lambda b, c: b * nc + c
    return pl.pallas_call(
        functools.partial(_mlstm_kernel, heads=G),
        out_shape=jax.ShapeDtypeStruct((t, H * dv), BF16),
        grid=(batch, ng, nc),
        in_specs=[pl.BlockSpec((L, G * dqk), lambda b, h, c: (tok(b, c), h)),
                  pl.BlockSpec((L, G * dqk), lambda b, h, c: (tok(b, c), ng + h)),
                  pl.BlockSpec((L, G * dv), lambda b, h, c: (tok(b, c), ng + h)),
                  pl.BlockSpec((L, G * dv), lambda b, h, c: (tok(b, c), 2 * ng + h)),
                  col_spec, col_spec, row_spec, row_spec,
                  pl.BlockSpec((1, G * dv), lambda b, h, c: (0, h))],
        out_specs=pl.BlockSpec((L, G * dv), lambda b, h, c: (tok(b, c), h)),
        scratch_shapes=[pltpu.VMEM((G, dqk, dv), F32), pltpu.VMEM((G, 1, dqk), F32), pltpu.VMEM((G, 1, 1), F32)],
        compiler_params=_params("parallel", "parallel", "arbitrary"),
        name="mlstm_chunk",
    )(proj, proj, proj, proj, col(i_g), col(f_g), rowv(i_g), rowv(f_g), norm_g)


def _rwkv_kernel(r_ref, k_ref, v_ref, w_ref, a_ref, g_ref, kk_ref, ka_ref, rk_ref, lg_ref, lb_ref, o_ref, s_sc,
                 *, heads):
    L, N = RW_CHUNK, RW_HEAD
    gw = heads * N

    @pl.when(pl.program_id(2) == 0)
    def _():
        s_sc[...] = jnp.zeros_like(s_sc)

    P = 2 * L
    pairs = heads // 2

    r, k, v = r_ref[...], k_ref[...], v_ref[...]
    log_decay = -jnp.exp(-_softplus(-w_ref[...]) - 0.5)
    a = _sigmoid(a_ref[...])

    sw = min(gw, 256)
    li = lax.broadcasted_iota(jnp.int32, (sw, sw), 0) // N
    lj = lax.broadcasted_iota(jnp.int32, (sw, sw), 1) // N
    head_ones = jnp.where(li == lj, 1.0, 0.0).astype(BF16)

    def head_sum(x, split):
        parts = split(x)
        cols = []
        for c0 in range(0, gw, sw):
            acc = None
            for p in parts:
                d = _dot(p[:, c0:c0 + sw], head_ones)
                acc = d if acc is None else acc + d
            cols.append(acc)
        return cols[0] if len(cols) == 1 else jnp.concatenate(cols, axis=1)

    one_pass = lambda x: (x.astype(BF16),)

    t_i = lax.broadcasted_iota(jnp.int32, (L, L), 0)
    t_j = lax.broadcasted_iota(jnp.int32, (L, L), 1)
    tri = jnp.where(t_i >= t_j, 1.0, 0.0).astype(BF16)

    kkr = k * kk_ref[...]
    k = k * (1.0 + (a - 1.0) * ka_ref[...])
    sums = head_sum(jnp.concatenate([kkr * kkr, r * k * rk_ref[...]], axis=0), one_pass)
    kk = kkr * lax.rsqrt(jnp.maximum(sums[:L], 1e-24))
    bonus = sums[L:] * v

    cl = _dot_01(tri, log_decay)
    cl_last = cl[L - 1:L, :]
    gam = jnp.exp(cl)
    inv_gam = jnp.exp(-cl)
    gam_end = jnp.exp(cl_last)
    to_end = gam_end * inv_gam
    kka = kk * a
    r_hat = r * gam
    a_hat = -kk * jnp.exp(cl - log_decay)
    b_hat = kka * inv_gam
    k_hat = k * inv_gam
    b_end = kka * to_end
    k_end = k * to_end

    p_i = lax.broadcasted_iota(jnp.int32, (P, P), 0)
    p_j = lax.broadcasted_iota(jnp.int32, (P, P), 1)
    same = p_i // L == p_j // L
    strict = jnp.logical_and(same, p_i > p_j)
    incl = jnp.logical_and(same, p_i >= p_j)
    eye = jnp.where(p_i == p_j, 1.0, 0.0)
    levels = []
    s = 2
    while s < L:
        levels.append(jnp.logical_and(p_i // (2 * s) == p_j // (2 * s),
                                      jnp.logical_and(p_i % (2 * s) >= s, p_j % (2 * s) < s)))
        s *= 2
    own = (lax.broadcasted_iota(jnp.int32, (P, 2 * N), 0) // L
           == lax.broadcasted_iota(jnp.int32, (P, 2 * N), 1) // N)

    def stack(x):
        return jnp.where(own, jnp.concatenate([x, x], axis=0), 0.0).astype(BF16)

    rng = range(pairs)
    sls = [slice(p * 2 * N, (p + 1) * 2 * N) for p in rng]
    st = [s_sc[p] for p in rng]
    ar_s = [jnp.concatenate([stack(a_hat[:, sl]), stack(r_hat[:, sl])], axis=0) for sl in sls]
    v_s = [stack(v[:, sl]) for sl in sls]
    m4 = [_dot_nt(ar_s[p], jnp.concatenate([stack(b_hat[:, sls[p]]), stack(k_hat[:, sls[p]])], axis=0))
          for p in rng]
    a_ab = [jnp.where(strict, m[:P, :P], 0.0) for m in m4]
    a_ak = [jnp.where(strict, m[:P, P:], 0.0).astype(BF16) for m in m4]
    a_r = [jnp.concatenate([jnp.where(incl, m[P:, :P], 0.0), jnp.where(incl, m[P:, P:], 0.0)], axis=1).astype(BF16)
           for m in m4]
    from_state = [_dot_nt(ar_s[p], st[p].astype(BF16)) for p in rng]
    rhs_u = [from_state[p][:P] + _dot(a_ak[p], v_s[p]) for p in rng]
    x = [eye + jnp.where(p_i // 2 == p_j // 2, m, 0.0) for m in a_ab]
    for blk in levels:
        xb = [xp.astype(BF16) for xp in x]
        half = [_dot(xb[p], jnp.where(blk, a_ab[p], 0.0).astype(BF16)).astype(BF16) for p in rng]
        x = [x[p] + _dot(half[p], xb[p]) for p in rng]
    su = [_dot(x[p].astype(BF16), rhs_u[p].astype(BF16)) for p in rng]
    su_v = [jnp.concatenate([su[p].astype(BF16), v_s[p]], axis=0) for p in rng]
    y_s = [from_state[p][P:] + _dot(a_r[p], su_v[p]) for p in rng]
    for p in rng:
        be = jnp.concatenate([stack(b_end[:, sls[p]]), stack(k_end[:, sls[p]])], axis=0)
        s_sc[p] = st[p] * gam_end[:, sls[p]] + _dot_tn(su_v[p], be)
    ys = [yp[:L] + yp[L:] for yp in y_s]
    y = ys[0] if pairs == 1 else jnp.concatenate(ys, axis=1)

    inv_n = 1.0 / N
    mean = head_sum(y, one_pass) * inv_n
    yc = y - mean
    var = head_sum(yc * yc, one_pass) * inv_n
    yn = yc * lax.rsqrt(var + RW_GN_EPS) * lg_ref[...] + lb_ref[...]
    o_ref[...] = ((yn + bonus) * g_ref[...]).astype(o_ref.dtype)


def _rwkv(r, k, v, w, a, g, k_k, k_a, r_k, lnx_g, lnx_b, batch, seq):
    t, d = r.shape
    L = RW_CHUNK
    heads = min(32, d // RW_HEAD)
    gw = heads * RW_HEAD
    nc = seq // L
    tile = pl.BlockSpec((L, gw), lambda b, h, c: (b * nc + c, h))
    row = pl.BlockSpec((1, gw), lambda b, h, c: (0, h))
    return pl.pallas_call(
        functools.partial(_rwkv_kernel, heads=heads),
        out_shape=jax.ShapeDtypeStruct((t, d), BF16),
        grid=(batch, d // gw, nc),
        in_specs=[tile] * 6 + [row] * 5,
        out_specs=tile,
        scratch_shapes=[pltpu.VMEM((heads // 2, 2 * RW_HEAD, 2 * RW_HEAD), F32)],
        compiler_params=_params("parallel", "parallel", "arbitrary"),
        name="rwkv7_chunk",
    )(r, k, v, w, a, g, k_k, k_a, r_k, lnx_g, lnx_b)


def kernel(x, c, ada_w, ada_b, mix_ln_g, mix_ln_b, ffn_ln_g, ffn_ln_b, ffn_w_in, ffn_w_out, ml_w_in, ml_b_i, ml_b_f, ml_norm_g, ml_w_out, rw_mu, rw_w_r, rw_w_k, rw_w_v, rw_w0, rw_w1, rw_w2, rw_a0, rw_a1, rw_a2, rw_g1, rw_g2, rw_k_k, rw_k_a, rw_r_k, rw_lnx_g, rw_lnx_b, rw_w_o):
    batch, seq, d = x.shape
    depth = ada_w.shape[0]
    t = batch * seq
    alpha = (2 * depth) ** 0.25
    hidden = ffn_w_out.shape[1]
    n_mixers = 2

    mod = _ada(c, ada_w, ada_b).reshape(depth, batch, 6, 1, d)
    row = lambda p: p.reshape(1, -1)
    zero_bias = lambda n: jnp.zeros((1, n), F32)
    ffn_w_out_bf16 = ffn_w_out.astype(BF16)

    x2 = x.reshape(t, d)
    u = None
    for layer in range(depth):
        sh_m, sc_m, gt_m, sh_f, sc_f, gt_f = [mod[layer, :, i] for i in range(6)]
        j = layer // n_mixers
        if layer % n_mixers == 0:
            if u is None:
                u = _modulate(x2, sc_m, sh_m, seq)
            qkvo = ml_w_in.shape[2] - 2 * ML_HEADS
            proj = _mm(u, ml_w_in, j, qkvo, zero_bias(qkvo), _identity, BF16, "mlstm_in")
            w_gate = jnp.pad(ml_w_in[j:j + 1, :, qkvo:], ((0, 0), (0, 0), (0, LANES - 2 * ML_HEADS)))
            b_gate = jnp.pad(jnp.concatenate([ml_b_i[j], ml_b_f[j]]), (0, LANES - 2 * ML_HEADS)).reshape(1, LANES)
            gates = _mm(u, w_gate, 0, LANES, b_gate, _ml_gate_act, F32, "mlstm_gates")
            y = _mlstm(proj, gates, row(ml_norm_g[j]), batch, seq)
            z = _mm_resid(y, ml_w_out.astype(BF16), j, x2, gt_m, alpha, seq, "mlstm_out")
            x2, u = _layer_norm(z, row(mix_ln_g[layer]), row(mix_ln_b[layer]), seq, ("mod", sc_f, sh_f))
        else:
            xr, xw, xk, xv, xa, xg = u
            nb = zero_bias
            r = _mm(xr, rw_w_r, j, d, nb(d), _identity, F32, "rwkv_r")
            k = _mm(xk, rw_w_k, j, d, nb(d), _identity, F32, "rwkv_k")
            v = _mm(xv, rw_w_v, j, d, nb(d), _identity, F32, "rwkv_v")
            n_w, n_a, n_g = rw_w1.shape[2], rw_a1.shape[2], rw_g1.shape[2]
            lw = _mm(xw, rw_w1, j, n_w, nb(n_w), jnp.tanh, BF16, "rwkv_w1")
            la = _mm(xa, rw_a1, j, n_a, nb(n_a), _identity, BF16, "rwkv_a1")
            lg = _mm(xg, rw_g1, j, n_g, nb(n_g), _sigmoid, BF16, "rwkv_g1")
            w = _mm(lw, rw_w2, j, d, row(rw_w0[j]), _identity, F32, "rwkv_w2")
            a = _mm(la, rw_a2, j, d, row(rw_a0[j]), _identity, F32, "rwkv_a2")
            g = _mm(lg, rw_g2, j, d, nb(d), _identity, F32, "rwkv_g2")
            y = _rwkv(r, k, v, w, a, g, row(rw_k_k[j]), row(rw_k_a[j]), row(rw_r_k[j]),
                      row(rw_lnx_g[j]), row(rw_lnx_b[j]), batch, seq)
            z = _mm_resid(y, rw_w_o.astype(BF16), j, x2, gt_m, alpha, seq, "rwkv_out")
            x2, u = _layer_norm(z, row(mix_ln_g[layer]), row(mix_ln_b[layer]), seq, ("mod", sc_f, sh_f))
        hid = _mm_swiglu(u, ffn_w_in, layer, hidden)
        z = _mm_resid(hid, ffn_w_out_bf16, layer, x2, gt_f, alpha, seq, "ffn_out")
        g_f, b_f = row(ffn_ln_g[layer]), row(ffn_ln_b[layer])
        if layer + 1 == depth:
            x2, u = _layer_norm(z, g_f, b_f, seq), None
        else:
            nsh, nsc = mod[layer + 1, :, 0], mod[layer + 1, :, 1]
            if (layer + 1) % n_mixers == 0:
                x2, u = _layer_norm(z, g_f, b_f, seq, ("mod", nsc, nsh))
            else:
                out = _layer_norm(z, g_f, b_f, seq, ("shift", nsc, nsh, rw_mu[(layer + 1) // n_mixers]))
                x2, u = out[0], out[1:]
    return x2.reshape(batch, seq, d)
```

```python
import functools

import jax
import jax.numpy as jnp
from jax import lax
from jax.experimental import pallas as pl
from jax.experimental.pallas import tpu as pltpu

F32 = jnp.float32
BF16 = jnp.bfloat16

ML_HEADS = 8
ML_CHUNK = 128
ML_GATE_CAP = 15.0
ML_NORM_EPS = 1e-6
RW_HEAD = 64
RW_CHUNK = 64
RW_GN_EPS = 64e-5
LN_EPS = 1e-5
LANES = 128
VMEM_LIMIT_BYTES = 56 * 1024 * 1024


def _params(*semantics):
    return pltpu.CompilerParams(dimension_semantics=semantics, vmem_limit_bytes=VMEM_LIMIT_BYTES)


def _pick(n, candidates):
    for c in candidates:
        if n % c == 0:
            return c
    return n


def _dot(a, b):
    return jnp.dot(a, b, preferred_element_type=F32)


def _dot_nt(a, b):
    return lax.dot_general(a, b, (((1,), (1,)), ((), ())), preferred_element_type=F32)


def _dot_tn(a, b):
    return lax.dot_general(a, b, (((0,), (0,)), ((), ())), preferred_element_type=F32)


def _split2(x):
    hi = x.astype(BF16)
    lo = (x - hi.astype(F32)).astype(BF16)
    return hi, lo


def _split3(x):
    hi = x.astype(BF16)
    r = x - hi.astype(F32)
    mid = r.astype(BF16)
    lo = (r - mid.astype(F32)).astype(BF16)
    return hi, mid, lo


def _dot_hp(a, b, dot=_dot):
    ah, al = _split2(a)
    bh, bl = _split2(b)
    return dot(ah, bh) + (dot(ah, bl) + dot(al, bh))


def _dot_01(m, x):
    hi, mid, lo = _split3(x)
    return _dot(m, hi) + (_dot(m, mid) + _dot(m, lo))


def _sigmoid(x):
    return 1.0 / (1.0 + jnp.exp(-x))


def _softplus(x):
    return jnp.maximum(x, 0.0) + jnp.log1p(jnp.exp(-jnp.abs(x)))


def _ada_kernel(c_ref, w_ref, b_ref, o_ref):
    c = c_ref[...]
    c_act = (c * _sigmoid(c)).astype(BF16)
    o_ref[0] = _dot(c_act, w_ref[0].astype(BF16)) + b_ref[0]


def _ada(c, ada_w, ada_b):
    depth, d, n = ada_w.shape
    b = c.shape[0]
    tn = _pick(n, (512, 256, 128))
    return pl.pallas_call(
        _ada_kernel,
        out_shape=jax.ShapeDtypeStruct((depth, b, n), F32),
        grid=(depth, n // tn),
        in_specs=[pl.BlockSpec((b, d), lambda l, j: (0, 0)),
                  pl.BlockSpec((1, d, tn), lambda l, j: (l, 0, j)),
                  pl.BlockSpec((1, 1, tn), lambda l, j: (l, 0, j))],
        out_specs=pl.BlockSpec((1, b, tn), lambda l, j: (l, 0, j)),
        compiler_params=_params("parallel", "parallel"),
        name="ada_mod",
    )(c, ada_w, ada_b.reshape(depth, 1, n))


def _modulate_kernel(x_ref, sc_ref, sh_ref, o_ref):
    o_ref[...] = (x_ref[...] * (1.0 + sc_ref[0]) + sh_ref[0]).astype(o_ref.dtype)


def _modulate(x2, sc, sh, seq):
    t, d = x2.shape
    ts = _pick(seq, (512, 256, 128))
    vec = pl.BlockSpec((1, 1, d), lambda i: ((i * ts) // seq, 0, 0))
    return pl.pallas_call(
        _modulate_kernel,
        out_shape=jax.ShapeDtypeStruct((t, d), BF16),
        grid=(t // ts,),
        in_specs=[pl.BlockSpec((ts, d), lambda i: (i, 0)), vec, vec],
        out_specs=pl.BlockSpec((ts, d), lambda i: (i, 0)),
        compiler_params=_params("parallel"),
        name="modulate",
    )(x2, sc, sh)


def _mm_kernel(a_ref, w_ref, b_ref, o_ref, *, act, w_is_nk):
    dot = _dot_nt if w_is_nk else _dot
    y = dot(a_ref[...], w_ref[...].astype(BF16)) + b_ref[...]
    o_ref[...] = act(y).astype(o_ref.dtype)


def _mm(a, w, layer, n, bias, act, out_dtype, name, w_is_nk=False):
    t, k = a.shape
    tm = _pick(t, (1024, 512, 256, 128))
    tn = _pick(n, (512, 256, 128))
    if w_is_nk:
        w_spec = pl.BlockSpec((None, tn, k), lambda j, i: (layer, j, 0))
    else:
        w_spec = pl.BlockSpec((None, k, tn), lambda j, i: (layer, 0, j))
    return pl.pallas_call(
        functools.partial(_mm_kernel, act=act, w_is_nk=w_is_nk),
        out_shape=jax.ShapeDtypeStruct((t, n), out_dtype),
        grid=(n // tn, t // tm),
        in_specs=[pl.BlockSpec((tm, k), lambda j, i: (i, 0)),
                  w_spec,
                  pl.BlockSpec((1, tn), lambda j, i: (0, j))],
        out_specs=pl.BlockSpec((tm, tn), lambda j, i: (i, j)),
        compiler_params=_params("parallel", "parallel"),
        name=name,
    )(a, w, bias)


def _identity(y):
    return y


def _swiglu_kernel(a_ref, wg_ref, wu_ref, o_ref):
    a = a_ref[...]
    gate = _dot(a, wg_ref[...].astype(BF16))
    up = _dot(a, wu_ref[...].astype(BF16))
    o_ref[...] = (gate * _sigmoid(gate) * up).astype(o_ref.dtype)


def _mm_swiglu(a, w_in, layer, hidden):
    t, k = a.shape
    tm = _pick(t, (1024, 512, 256, 128))
    tn = _pick(hidden, (256, 128))
    nj = hidden // tn
    return pl.pallas_call(
        _swiglu_kernel,
        out_shape=jax.ShapeDtypeStruct((t, hidden), BF16),
        grid=(nj, t // tm),
        in_specs=[pl.BlockSpec((tm, k), lambda j, i: (i, 0)),
                  pl.BlockSpec((None, k, tn), lambda j, i: (layer, 0, j)),
                  pl.BlockSpec((None, k, tn), lambda j, i: (layer, 0, j + nj))],
        out_specs=pl.BlockSpec((tm, tn), lambda j, i: (i, j)),
        compiler_params=_params("parallel", "parallel"),
        name="ffn_in_swiglu",
    )(a, w_in, w_in)


def _resid_kernel(a_ref, w_ref, x_ref, gt_ref, o_ref, *, alpha):
    o_ref[...] = alpha * x_ref[...] + gt_ref[0] * _dot(a_ref[...], w_ref[...])


def _mm_resid(a, w, layer, x2, gt, alpha, seq, name):
    t, k = a.shape
    n = w.shape[2]
    if k <= 4096:
        tm, tn = _pick(min(t, seq), (1024, 512, 256, 128)), _pick(n, (1024, 512, 256, 128))
    else:
        tm, tn = _pick(min(t, seq), (512, 256, 128)), _pick(n, (512, 256, 128))
    return pl.pallas_call(
        functools.partial(_resid_kernel, alpha=alpha),
        out_shape=jax.ShapeDtypeStruct((t, n), F32),
        grid=(t // tm, n // tn),
        in_specs=[pl.BlockSpec((tm, k), lambda i, j: (i, 0)),
                  pl.BlockSpec((None, k, tn), lambda i, j: (layer, 0, j)),
                  pl.BlockSpec((tm, tn), lambda i, j: (i, j)),
                  pl.BlockSpec((1, 1, tn), lambda i, j: ((i * tm) // seq, 0, j))],
        out_specs=pl.BlockSpec((tm, tn), lambda i, j: (i, j)),
        compiler_params=_params("parallel", "parallel"),
        name=name,
    )(a, w, x2, gt)


def _ln_rows(z, g, b):
    mu = jnp.mean(z, axis=-1, keepdims=True)
    zc = z - mu
    var = jnp.mean(zc * zc, axis=-1, keepdims=True)
    return zc * lax.rsqrt(var + LN_EPS) * g + b


def _ln_kernel(z_ref, g_ref, b_ref, x_ref):
    x_ref[...] = _ln_rows(z_ref[...], g_ref[...], b_ref[...])


def _ln_mod_kernel(z_ref, g_ref, b_ref, sc_ref, sh_ref, x_ref, u_ref):
    x = _ln_rows(z_ref[...], g_ref[...], b_ref[...])
    x_ref[...] = x
    u_ref[...] = (x * (1.0 + sc_ref[0]) + sh_ref[0]).astype(u_ref.dtype)


def _ln_shift_kernel(z_ref, zp_ref, g_ref, b_ref, sc_ref, sh_ref, mu_ref, x_ref, *mix_refs, ts, seq):
    g, b = g_ref[...], b_ref[...]
    scale, shift = 1.0 + sc_ref[0], sh_ref[0]
    x = _ln_rows(z_ref[...], g, b)
    x_ref[...] = x
    u = x * scale + shift
    prev = _ln_rows(zp_ref[...], g, b)[7:8, :] * scale + shift
    first = (pl.program_id(0) * ts) % seq == 0
    prev = jnp.where(first, 0.0, prev)
    row = lax.broadcasted_iota(jnp.int32, u.shape, 0)
    u_prev = jnp.where(row == 0, prev, pltpu.roll(u, shift=1, axis=0))
    xx = u_prev - u
    for j, ref in enumerate(mix_refs):
        ref[...] = (u + xx * mu_ref[j:j + 1, :]).astype(ref.dtype)


def _layer_norm(z, g, b, seq, nxt=None):
    t, d = z.shape
    row = pl.BlockSpec((1, d), lambda i: (0, 0))
    if nxt is None:
        ts = _pick(seq, (256, 128))
        tile = pl.BlockSpec((ts, d), lambda i: (i, 0))
        return pl.pallas_call(
            _ln_kernel, out_shape=jax.ShapeDtypeStruct((t, d), F32), grid=(t // ts,),
            in_specs=[tile, row, row], out_specs=tile,
            compiler_params=_params("parallel"), name="layer_norm",
        )(z, g, b)
    if nxt[0] == "mod":
        ts = _pick(seq, (256, 128))
        tile = pl.BlockSpec((ts, d), lambda i: (i, 0))
        vec = pl.BlockSpec((1, 1, d), lambda i: ((i * ts) // seq, 0, 0))
        return pl.pallas_call(
            _ln_mod_kernel,
            out_shape=(jax.ShapeDtypeStruct((t, d), F32), jax.ShapeDtypeStruct((t, d), BF16)),
            grid=(t // ts,),
            in_specs=[tile, row, row, vec, vec], out_specs=(tile, tile),
            compiler_params=_params("parallel"), name="layer_norm_mod",
        )(z, g, b, nxt[1], nxt[2])
    _, sc, sh, mu = nxt
    nmix = mu.shape[0]
    ts = _pick(seq, (128,))
    sub = 8
    tile = pl.BlockSpec((ts, d), lambda i: (i, 0))
    prev = pl.BlockSpec((sub, d), lambda i: (jnp.maximum(i * (ts // sub) - 1, 0), 0))
    vec = pl.BlockSpec((1, 1, d), lambda i: ((i * ts) // seq, 0, 0))
    return pl.pallas_call(
        functools.partial(_ln_shift_kernel, ts=ts, seq=seq),
        out_shape=(jax.ShapeDtypeStruct((t, d), F32),) + (jax.ShapeDtypeStruct((t, d), BF16),) * nmix,
        grid=(t // ts,),
        in_specs=[tile, prev, row, row, vec, vec, pl.BlockSpec((nmix, d), lambda i: (0, 0))],
        out_specs=(tile,) * (1 + nmix),
        compiler_params=_params("parallel"), name="layer_norm_shift",
    )(z, z, g, b, sc, sh, mu)


def _ml_gate_act(y):
    lane = lax.broadcasted_iota(jnp.int32, y.shape, 1)
    i_pre = ML_GATE_CAP * jnp.tanh(y / ML_GATE_CAP)
    log_f = -_softplus(-y)
    return jnp.where(lane < ML_HEADS, i_pre, log_f)


def _mlstm_kernel(q_ref, k_ref, v_ref, o_ref, icol_ref, fcol_ref, irow_ref, frow_ref, ng_ref, out_ref,
                  c_sc, n_sc, m_sc, *, heads):
    L = ML_CHUNK

    @pl.when(pl.program_id(2) == 0)
    def _():
        c_sc[...] = jnp.zeros_like(c_sc)
        n_sc[...] = jnp.zeros_like(n_sc)
        m_sc[...] = jnp.zeros_like(m_sc)

    dqk = q_ref.shape[1] // heads
    dv = v_ref.shape[1] // heads
    k_scale = dqk ** -0.5
    r_idx = lax.broadcasted_iota(jnp.int32, (L, L), 0)
    c_idx = lax.broadcasted_iota(jnp.int32, (L, L), 1)
    causal = r_idx >= c_idx
    anti = r_idx <= c_idx

    hs = range(heads)
    q = [q_ref[:, h * dqk:(h + 1) * dqk] for h in hs]
    k = [k_ref[:, h * dqk:(h + 1) * dqk] for h in hs]
    v = [v_ref[:, h * dv:(h + 1) * dv] for h in hs]
    i_col, f_col = [icol_ref[h] for h in hs], [fcol_ref[h] for h in hs]
    i_row, f_row = [irow_ref[h] for h in hs], [frow_ref[h] for h in hs]
    c_st, n_st, m_st = [c_sc[h] for h in hs], [n_sc[h] for h in hs], [m_sc[h] for h in hs]

    qk = [_dot_nt(q[h], k[h]) for h in hs]
    q_c = [_dot(q[h], c_st[h].astype(BF16)) for h in hs]
    g_col = [jnp.sum(jnp.where(causal, f_row[h], 0.0), axis=1, keepdims=True) for h in hs]
    g_row = [jnp.sum(jnp.where(anti, f_col[h], 0.0), axis=0, keepdims=True) for h in hs]
    g_last = [jnp.sum(f_row[h], axis=1, keepdims=True) for h in hs]
    log_d = [jnp.where(causal, g_col[h] - g_row[h] + i_row[h], -jnp.inf) for h in hs]
    log_inter = [g_col[h] + m_st[h] for h in hs]
    m_row = [jnp.maximum(jnp.max(log_d[h], axis=1, keepdims=True), log_inter[h]) for h in hs]
    scores = [qk[h] * k_scale * jnp.exp(log_d[h] - m_row[h]) for h in hs]
    inter = [jnp.exp(log_inter[h] - m_row[h]) for h in hs]
    num = [_dot(scores[h].astype(BF16), v[h]) + inter[h] * q_c[h] for h in hs]
    q_n = [jnp.sum(q[h].astype(F32) * n_st[h], axis=1, keepdims=True) for h in hs]
    den = [jnp.sum(scores[h], axis=1, keepdims=True) + inter[h] * q_n[h] for h in hs]
    for h in hs:
        hid = num[h] / jnp.maximum(jnp.abs(den[h]), jnp.exp(-m_row[h]))
        hid = hid * lax.rsqrt(jnp.mean(hid * hid, axis=1, keepdims=True) + ML_NORM_EPS)
        sl = slice(h * dv, (h + 1) * dv)
        out_ref[:, sl] = (hid * ng_ref[:, sl] * _sigmoid(o_ref[:, sl].astype(F32))).astype(out_ref.dtype)

    log_w = [g_last[h] - g_col[h] + i_col[h] for h in hs]
    m_new = [jnp.maximum(g_last[h] + m_st[h], jnp.max(log_w[h], axis=0, keepdims=True)) for h in hs]
    wk = [k[h].astype(F32) * (k_scale * jnp.exp(log_w[h] - m_new[h])) for h in hs]
    kv = [_dot_tn(wk[h].astype(BF16), v[h]) for h in hs]
    for h in hs:
        decay = jnp.exp(g_last[h] + m_st[h] - m_new[h])
        c_sc[h] = decay * c_st[h] + kv[h]
        n_sc[h] = decay * n_st[h] + jnp.sum(wk[h], axis=0, keepdims=True)
        m_sc[h] = m_new[h]


def _mlstm(proj, gates, norm_g, batch, seq):
    t = proj.shape[0]
    H, L = ML_HEADS, ML_CHUNK
    G = 4
    dv = norm_g.shape[1] // H
    dqk = dv // 2
    nc = seq // L
    ng = H // G
    g = gates[:, :2 * H].reshape(batch, seq, 2 * H).transpose(0, 2, 1)
    i_g, f_g = g[:, :H], g[:, H:]
    col = lambda a: a.reshape(batch, H, seq, 1)
    rowv = lambda a: a.reshape(batch, H, nc, 1, L)
    col_spec = pl.BlockSpec((None, G, L, 1), lambda b, h, c: (b, h, c, 0))
    row_spec = pl.BlockSpec((None, G, None, 1, L), lambda b, h, c: (b, h, c, 0, 0))
    tok = lambda b, c: b * nc + c
    return pl.pallas_call(
        functools.partial(_mlstm_kernel, heads=G),
        out_shape=jax.ShapeDtypeStruct((t, H * dv), BF16),
        grid=(batch, ng, nc),
        in_specs=[pl.BlockSpec((L, G * dqk), lambda b, h, c: (tok(b, c), h)),
                  pl.BlockSpec((L, G * dqk), lambda b, h, c: (tok(b, c), ng + h)),
                  pl.BlockSpec((L, G * dv), lambda b, h, c: (tok(b, c), ng + h)),
                  pl.BlockSpec((L, G * dv), lambda b, h, c: (tok(b, c), 2 * ng + h)),
                  col_spec, col_spec, row_spec, row_spec,
                  pl.BlockSpec((1, G * dv), lambda b, h, c: (0, h))],
        out_specs=pl.BlockSpec((L, G * dv), lambda b, h, c: (tok(b, c), h)),
        scratch_shapes=[pltpu.VMEM((G, dqk, dv), F32), pltpu.VMEM((G, 1, dqk), F32), pltpu.VMEM((G, 1, 1), F32)],
        compiler_params=_params("parallel", "parallel", "arbitrary"),
        name="mlstm_chunk",
    )(proj, proj, proj, proj, col(i_g), col(f_g), rowv(i_g), rowv(f_g), norm_g)


def _rwkv_kernel(r_ref, k_ref, v_ref, lw_ref, la_ref, lgate_ref, w2_ref, a2_ref, g2_ref, w0_ref, a0_ref,
                 kk_ref, ka_ref, rk_ref, lg_ref, lb_ref, o_ref, s_sc, *, heads):
    L, N = RW_CHUNK, RW_HEAD
    gw = heads * N

    @pl.when(pl.program_id(2) == 0)
    def _():
        s_sc[...] = jnp.zeros_like(s_sc)

    P = 2 * L
    pairs = heads // 2

    r, k, v = r_ref[...], k_ref[...], v_ref[...]
    w_pre = w0_ref[...] + _dot(lw_ref[...], w2_ref[...])
    a_pre = a0_ref[...] + _dot(la_ref[...], a2_ref[...])
    gate = _dot(lgate_ref[...], g2_ref[...])
    log_decay = -jnp.exp(-_softplus(-w_pre) - 0.5)
    a = _sigmoid(a_pre)

    sw = min(gw, 256)
    li = lax.broadcasted_iota(jnp.int32, (sw, sw), 0) // N
    lj = lax.broadcasted_iota(jnp.int32, (sw, sw), 1) // N
    head_ones = jnp.where(li == lj, 1.0, 0.0).astype(BF16)

    def head_sum(x, split):
        parts = split(x)
        cols = []
        for c0 in range(0, gw, sw):
            acc = None
            for p in parts:
                d = _dot(p[:, c0:c0 + sw], head_ones)
                acc = d if acc is None else acc + d
            cols.append(acc)
        return cols[0] if len(cols) == 1 else jnp.concatenate(cols, axis=1)

    one_pass = lambda x: (x.astype(BF16),)

    t_i = lax.broadcasted_iota(jnp.int32, (L, L), 0)
    t_j = lax.broadcasted_iota(jnp.int32, (L, L), 1)
    tri = jnp.where(t_i >= t_j, 1.0, 0.0).astype(BF16)

    kkr = k * kk_ref[...]
    k = k * (1.0 + (a - 1.0) * ka_ref[...])
    sums = head_sum(jnp.concatenate([kkr * kkr, r * k * rk_ref[...]], axis=0), one_pass)
    kk = kkr * lax.rsqrt(jnp.maximum(sums[:L], 1e-24))
    bonus = sums[L:] * v

    cl = _dot_01(tri, log_decay)
    cl_last = cl[L - 1:L, :]
    gam = jnp.exp(cl)
    inv_gam = jnp.exp(-cl)
    gam_end = jnp.exp(cl_last)
    to_end = gam_end * inv_gam
    kka = kk * a
    r_hat = r * gam
    a_hat = -kk * jnp.exp(cl - log_decay)
    b_hat = kka * inv_gam
    k_hat = k * inv_gam
    b_end = kka * to_end
    k_end = k * to_end

    p_i = lax.broadcasted_iota(jnp.int32, (P, P), 0)
    p_j = lax.broadcasted_iota(jnp.int32, (P, P), 1)
    same = p_i // L == p_j // L
    strict = jnp.logical_and(same, p_i > p_j)
    incl = jnp.logical_and(same, p_i >= p_j)
    eye = jnp.where(p_i == p_j, 1.0, 0.0)
    levels = []
    s = 2
    while s < L:
        levels.append(jnp.logical_and(p_i // (2 * s) == p_j // (2 * s),
                                      jnp.logical_and(p_i % (2 * s) >= s, p_j % (2 * s) < s)))
        s *= 2
    own = (lax.broadcasted_iota(jnp.int32, (P, 2 * N), 0) // L
           == lax.broadcasted_iota(jnp.int32, (P, 2 * N), 1) // N)

    def stack(x):
        return jnp.where(own, jnp.concatenate([x, x], axis=0), 0.0).astype(BF16)

    rng = range(pairs)
    sls = [slice(p * 2 * N, (p + 1) * 2 * N) for p in rng]
    st = [s_sc[p] for p in rng]
    ar_s = [jnp.concatenate([stack(a_hat[:, sl]), stack(r_hat[:, sl])], axis=0) for sl in sls]
    v_s = [stack(v[:, sl]) for sl in sls]
    m4 = [_dot_nt(ar_s[p], jnp.concatenate([stack(b_hat[:, sls[p]]), stack(k_hat[:, sls[p]])], axis=0))
          for p in rng]
    a_ab = [jnp.where(strict, m[:P, :P], 0.0) for m in m4]
    a_ak = [jnp.where(strict, m[:P, P:], 0.0).astype(BF16) for m in m4]
    a_r = [jnp.concatenate([jnp.where(incl, m[P:, :P], 0.0), jnp.where(incl, m[P:, P:], 0.0)], axis=1).astype(BF16)
           for m in m4]
    from_state = [_dot_nt(ar_s[p], st[p].astype(BF16)) for p in rng]
    rhs_u = [from_state[p][:P] + _dot(a_ak[p], v_s[p]) for p in rng]
    x = [eye + jnp.where(p_i // 2 == p_j // 2, m, 0.0) for m in a_ab]
    for blk in levels:
        xb = [xp.astype(BF16) for xp in x]
        half = [_dot(xb[p], jnp.where(blk, a_ab[p], 0.0).astype(BF16)).astype(BF16) for p in rng]
        x = [x[p] + _dot(half[p], xb[p]) for p in rng]
    su = [_dot(x[p].astype(BF16), rhs_u[p].astype(BF16)) for p in rng]
    su_v = [jnp.concatenate([su[p].astype(BF16), v_s[p]], axis=0) for p in rng]
    y_s = [from_state[p][P:] + _dot(a_r[p], su_v[p]) for p in rng]
    for p in rng:
        be = jnp.concatenate([stack(b_end[:, sls[p]]), stack(k_end[:, sls[p]])], axis=0)
        s_sc[p] = st[p] * gam_end[:, sls[p]] + _dot_tn(su_v[p], be)
    ys = [yp[:L] + yp[L:] for yp in y_s]
    y = ys[0] if pairs == 1 else jnp.concatenate(ys, axis=1)

    inv_n = 1.0 / N
    mean = head_sum(y, one_pass) * inv_n
    yc = y - mean
    var = head_sum(yc * yc, one_pass) * inv_n
    yn = yc * lax.rsqrt(var + RW_GN_EPS) * lg_ref[...] + lb_ref[...]
    o_ref[...] = ((yn + bonus) * gate).astype(o_ref.dtype)


def _rwkv(r, k, v, lw, la, lgate, w2, a2, g2, layer, w0, a0, k_k, k_a, r_k, lnx_g, lnx_b, batch, seq):
    t, d = r.shape
    L = RW_CHUNK
    heads = min(32, d // RW_HEAD)
    gw = heads * RW_HEAD
    nc = seq // L
    tile = pl.BlockSpec((L, gw), lambda b, h, c: (b * nc + c, h))
    row = pl.BlockSpec((1, gw), lambda b, h, c: (0, h))
    lora = lambda f: pl.BlockSpec((L, f.shape[1]), lambda b, h, c: (b * nc + c, 0))
    lora_w = lambda w: pl.BlockSpec((None, w.shape[1], gw), lambda b, h, c: (layer, 0, h))
    return pl.pallas_call(
        functools.partial(_rwkv_kernel, heads=heads),
        out_shape=jax.ShapeDtypeStruct((t, d), BF16),
        grid=(batch, d // gw, nc),
        in_specs=[tile] * 3 + [lora(lw), lora(la), lora(lgate), lora_w(w2), lora_w(a2), lora_w(g2)] + [row] * 7,
        out_specs=tile,
        scratch_shapes=[pltpu.VMEM((heads // 2, 2 * RW_HEAD, 2 * RW_HEAD), F32)],
        compiler_params=_params("parallel", "parallel", "arbitrary"),
        name="rwkv7_chunk",
    )(r, k, v, lw, la, lgate, w2, a2, g2, w0, a0, k_k, k_a, r_k, lnx_g, lnx_b)


def kernel(x, c, ada_w, ada_b, mix_ln_g, mix_ln_b, ffn_ln_g, ffn_ln_b, ffn_w_in, ffn_w_out, ml_w_in, ml_b_i, ml_b_f, ml_norm_g, ml_w_out, rw_mu, rw_w_r, rw_w_k, rw_w_v, rw_w0, rw_w1, rw_w2, rw_a0, rw_a1, rw_a2, rw_g1, rw_g2, rw_k_k, rw_k_a, rw_r_k, rw_lnx_g, rw_lnx_b, rw_w_o):
    batch, seq, d = x.shape
    depth = ada_w.shape[0]
    t = batch * seq
    alpha = (2 * depth) ** 0.25
    hidden = ffn_w_out.shape[1]
    n_mixers = 2

    mod = _ada(c, ada_w, ada_b).reshape(depth, batch, 6, 1, d)
    row = lambda p: p.reshape(1, -1)
    zero_bias = lambda n: jnp.zeros((1, n), F32)
    ffn_w_out_bf16 = ffn_w_out.astype(BF16)

    x2 = x.reshape(t, d)
    u = None
    for layer in range(depth):
        sh_m, sc_m, gt_m, sh_f, sc_f, gt_f = [mod[layer, :, i] for i in range(6)]
        j = layer // n_mixers
        if layer % n_mixers == 0:
            if u is None:
                u = _modulate(x2, sc_m, sh_m, seq)
            qkvo = ml_w_in.shape[2] - 2 * ML_HEADS
            proj = _mm(u, jnp.swapaxes(ml_w_in, 1, 2), j, qkvo, zero_bias(qkvo), _identity, BF16, "mlstm_in",
                       w_is_nk=True)
            w_gate = jnp.pad(ml_w_in[j:j + 1, :, qkvo:], ((0, 0), (0, 0), (0, LANES - 2 * ML_HEADS)))
            b_gate = jnp.pad(jnp.concatenate([ml_b_i[j], ml_b_f[j]]), (0, LANES - 2 * ML_HEADS)).reshape(1, LANES)
            gates = _mm(u, w_gate, 0, LANES, b_gate, _ml_gate_act, F32, "mlstm_gates")
            y = _mlstm(proj, gates, row(ml_norm_g[j]), batch, seq)
            z = _mm_resid(y, ml_w_out.astype(BF16), j, x2, gt_m, alpha, seq, "mlstm_out")
            x2, u = _layer_norm(z, row(mix_ln_g[layer]), row(mix_ln_b[layer]), seq, ("mod", sc_f, sh_f))
        else:
            xr, xw, xk, xv, xa, xg = u
            nb = zero_bias
            r = _mm(xr, rw_w_r, j, d, nb(d), _identity, F32, "rwkv_r")
            k = _mm(xk, rw_w_k, j, d, nb(d), _identity, F32, "rwkv_k")
            v = _mm(xv, rw_w_v, j, d, nb(d), _identity, F32, "rwkv_v")
            n_w, n_a, n_g = rw_w1.shape[2], rw_a1.shape[2], rw_g1.shape[2]
            lw = _mm(xw, rw_w1, j, n_w, nb(n_w), jnp.tanh, BF16, "rwkv_w1")
            la = _mm(xa, rw_a1, j, n_a, nb(n_a), _identity, BF16, "rwkv_a1")
            lg = _mm(xg, rw_g1, j, n_g, nb(n_g), _sigmoid, BF16, "rwkv_g1")
            y = _rwkv(r, k, v, lw, la, lg, rw_w2.astype(BF16), rw_a2.astype(BF16), rw_g2.astype(BF16), j,
                      row(rw_w0[j]), row(rw_a0[j]), row(rw_k_k[j]), row(rw_k_a[j]), row(rw_r_k[j]),
                      row(rw_lnx_g[j]), row(rw_lnx_b[j]), batch, seq)
            z = _mm_resid(y, rw_w_o.astype(BF16), j, x2, gt_m, alpha, seq, "rwkv_out")
            x2, u = _layer_norm(z, row(mix_ln_g[layer]), row(mix_ln_b[layer]), seq, ("mod", sc_f, sh_f))
        hid = _mm_swiglu(u, ffn_w_in, layer, hidden)
        z = _mm_resid(hid, ffn_w_out_bf16, layer, x2, gt_f, alpha, seq, "ffn_out")
        g_f, b_f = row(ffn_ln_g[layer]), row(ffn_ln_b[layer])
        if layer + 1 == depth:
            x2, u = _layer_norm(z, g_f, b_f, seq), None
        else:
            nsh, nsc = mod[layer + 1, :, 0], mod[layer + 1, :, 1]
            if (layer + 1) % n_mixers == 0:
                x2, u = _layer_norm(z, g_f, b_f, seq, ("mod", nsc, nsh))
            else:
                out = _layer_norm(z, g_f, b_f, seq, ("shift", nsc, nsh, rw_mu[(layer + 1) // n_mixers]))
                x2, u = out[0], out[1:]
    return x2.reshape(batch, seq, d)
```

```python
import functools

import jax
import jax.numpy as jnp
from jax import lax
from jax.experimental import pallas as pl
from jax.experimental.pallas import tpu as pltpu

F32 = jnp.float32
BF16 = jnp.bfloat16

ML_HEADS = 8
ML_CHUNK = 128
ML_GATE_CAP = 15.0
ML_NORM_EPS = 1e-6
RW_HEAD = 64
RW_CHUNK = 64
RW_GN_EPS = 64e-5
LN_EPS = 1e-5
LANES = 128
VMEM_LIMIT_BYTES = 56 * 1024 * 1024
TILE_VMEM_BUDGET_BYTES = 52 * 1024 * 1024


def _params(*semantics):
    return pltpu.CompilerParams(dimension_semantics=semantics, vmem_limit_bytes=VMEM_LIMIT_BYTES)


def _pick(n, candidates):
    for c in candidates:
        if n % c == 0:
            return c
    return n


def _dot(a, b):
    return jnp.dot(a, b, preferred_element_type=F32)


def _dot_nt(a, b):
    return lax.dot_general(a, b, (((1,), (1,)), ((), ())), preferred_element_type=F32)


def _dot_tn(a, b):
    return lax.dot_general(a, b, (((0,), (0,)), ((), ())), preferred_element_type=F32)


def _split3(x):
    hi = x.astype(BF16)
    r = x - hi.astype(F32)
    mid = r.astype(BF16)
    lo = (r - mid.astype(F32)).astype(BF16)
    return hi, mid, lo


def _dot_01(m, x):
    hi, mid, lo = _split3(x)
    return _dot(m, hi) + (_dot(m, mid) + _dot(m, lo))


def _sigmoid(x):
    return 1.0 / (1.0 + jnp.exp(-x))


def _softplus(x):
    return jnp.maximum(x, 0.0) + jnp.log1p(jnp.exp(-jnp.abs(x)))


def _ada_kernel(c_ref, w_ref, b_ref, o_ref):
    c = c_ref[...]
    c_act = (c * _sigmoid(c)).astype(BF16)
    o_ref[0] = _dot(c_act, w_ref[0].astype(BF16)) + b_ref[0]


def _ada(c, ada_w, ada_b):
    depth, d, n = ada_w.shape
    b = c.shape[0]
    tn = _pick(n, (512, 256, 128))
    return pl.pallas_call(
        _ada_kernel,
        out_shape=jax.ShapeDtypeStruct((depth, b, n), F32),
        grid=(depth, n // tn),
        in_specs=[pl.BlockSpec((b, d), lambda l, j: (0, 0)),
                  pl.BlockSpec((1, d, tn), lambda l, j: (l, 0, j)),
                  pl.BlockSpec((1, 1, tn), lambda l, j: (l, 0, j))],
        out_specs=pl.BlockSpec((1, b, tn), lambda l, j: (l, 0, j)),
        compiler_params=_params("parallel", "parallel"),
        name="ada_mod",
    )(c, ada_w, ada_b.reshape(depth, 1, n))


def _modulate_kernel(x_ref, sc_ref, sh_ref, o_ref):
    o_ref[...] = (x_ref[...] * (1.0 + sc_ref[0]) + sh_ref[0]).astype(o_ref.dtype)


def _modulate(x2, sc, sh, seq):
    t, d = x2.shape
    ts = _pick(seq, (512, 256, 128))
    vec = pl.BlockSpec((1, 1, d), lambda i: ((i * ts) // seq, 0, 0))
    return pl.pallas_call(
        _modulate_kernel,
        out_shape=jax.ShapeDtypeStruct((t, d), BF16),
        grid=(t // ts,),
        in_specs=[pl.BlockSpec((ts, d), lambda i: (i, 0)), vec, vec],
        out_specs=pl.BlockSpec((ts, d), lambda i: (i, 0)),
        compiler_params=_params("parallel"),
        name="modulate",
    )(x2, sc, sh)


def _mm_kernel(a_ref, w_ref, b_ref, o_ref, *, act, w_is_nk):
    dot = _dot_nt if w_is_nk else _dot
    y = dot(a_ref[...], w_ref[...].astype(BF16)) + b_ref[...]
    o_ref[...] = act(y).astype(o_ref.dtype)


def _mm(a, w, layer, n, bias, act, out_dtype, name, w_is_nk=False):
    t, k = a.shape
    tm = _pick(t, (2048, 1024, 512, 256, 128))
    out_bytes = jnp.dtype(out_dtype).itemsize
    fits = lambda c: 2 * (tm * k * 2 + k * c * 4 + tm * c * out_bytes) <= TILE_VMEM_BUDGET_BYTES
    tn = _pick(n, [c for c in (512, 256, 128) if fits(c)])
    if w_is_nk:
        w_spec = pl.BlockSpec((None, tn, k), lambda i, j: (layer, j, 0))
    else:
        w_spec = pl.BlockSpec((None, k, tn), lambda i, j: (layer, 0, j))
    return pl.pallas_call(
        functools.partial(_mm_kernel, act=act, w_is_nk=w_is_nk),
        out_shape=jax.ShapeDtypeStruct((t, n), out_dtype),
        grid=(t // tm, n // tn),
        in_specs=[pl.BlockSpec((tm, k), lambda i, j: (i, 0)),
                  w_spec,
                  pl.BlockSpec((1, tn), lambda i, j: (0, j))],
        out_specs=pl.BlockSpec((tm, tn), lambda i, j: (i, j)),
        compiler_params=_params("parallel", "parallel"),
        name=name,
    )(a, w, bias)


def _identity(y):
    return y


def _swiglu_kernel(a_ref, wg_ref, wu_ref, o_ref):
    a = a_ref[...]
    gate = _dot(a, wg_ref[...].astype(BF16))
    up = _dot(a, wu_ref[...].astype(BF16))
    o_ref[...] = (gate * _sigmoid(gate) * up).astype(o_ref.dtype)


def _mm_swiglu(a, w_in, layer, hidden):
    t, k = a.shape
    tm = _pick(t, (2048, 1024, 512, 256, 128))
    tn = _pick(hidden, (256, 128))
    nj = hidden // tn
    return pl.pallas_call(
        _swiglu_kernel,
        out_shape=jax.ShapeDtypeStruct((t, hidden), BF16),
        grid=(t // tm, nj),
        in_specs=[pl.BlockSpec((tm, k), lambda i, j: (i, 0)),
                  pl.BlockSpec((None, k, tn), lambda i, j: (layer, 0, j)),
                  pl.BlockSpec((None, k, tn), lambda i, j: (layer, 0, j + nj))],
        out_specs=pl.BlockSpec((tm, tn), lambda i, j: (i, j)),
        compiler_params=_params("parallel", "parallel"),
        name="ffn_in_swiglu",
    )(a, w_in, w_in)


def _resid_kernel(a_ref, w_ref, x_ref, gt_ref, o_ref, *, alpha):
    o_ref[...] = alpha * x_ref[...] + gt_ref[0] * _dot(a_ref[...], w_ref[...])


def _mm_resid(a, w, layer, x2, gt, alpha, seq, name):
    t, k = a.shape
    n = w.shape[2]
    if k <= 4096:
        tm, tn = _pick(min(t, seq), (1024, 512, 256, 128)), _pick(n, (1024, 512, 256, 128))
    else:
        tm, tn = _pick(min(t, seq), (512, 256, 128)), _pick(n, (512, 256, 128))
    return pl.pallas_call(
        functools.partial(_resid_kernel, alpha=alpha),
        out_shape=jax.ShapeDtypeStruct((t, n), F32),
        grid=(t // tm, n // tn),
        in_specs=[pl.BlockSpec((tm, k), lambda i, j: (i, 0)),
                  pl.BlockSpec((None, k, tn), lambda i, j: (layer, 0, j)),
                  pl.BlockSpec((tm, tn), lambda i, j: (i, j)),
                  pl.BlockSpec((1, 1, tn), lambda i, j: ((i * tm) // seq, 0, j))],
        out_specs=pl.BlockSpec((tm, tn), lambda i, j: (i, j)),
        compiler_params=_params("parallel", "parallel"),
        name=name,
    )(a, w, x2, gt)


def _ln_rows(z, g, b):
    mu = jnp.mean(z, axis=-1, keepdims=True)
    zc = z - mu
    var = jnp.mean(zc * zc, axis=-1, keepdims=True)
    return zc * lax.rsqrt(var + LN_EPS) * g + b


def _ln_kernel(z_ref, g_ref, b_ref, x_ref):
    x_ref[...] = _ln_rows(z_ref[...], g_ref[...], b_ref[...])


def _ln_mod_kernel(z_ref, g_ref, b_ref, sc_ref, sh_ref, x_ref, u_ref):
    x = _ln_rows(z_ref[...], g_ref[...], b_ref[...])
    x_ref[...] = x
    u_ref[...] = (x * (1.0 + sc_ref[0]) + sh_ref[0]).astype(u_ref.dtype)


def _ln_shift_kernel(z_ref, zp_ref, g_ref, b_ref, sc_ref, sh_ref, mu_ref, x_ref, *mix_refs, ts, seq):
    g, b = g_ref[...], b_ref[...]
    scale, shift = 1.0 + sc_ref[0], sh_ref[0]
    x = _ln_rows(z_ref[...], g, b)
    x_ref[...] = x
    u = x * scale + shift
    prev = _ln_rows(zp_ref[...], g, b)[7:8, :] * scale + shift
    first = (pl.program_id(0) * ts) % seq == 0
    prev = jnp.where(first, 0.0, prev)
    row = lax.broadcasted_iota(jnp.int32, u.shape, 0)
    u_prev = jnp.where(row == 0, prev, pltpu.roll(u, shift=1, axis=0))
    xx = u_prev - u
    for j, ref in enumerate(mix_refs):
        ref[...] = (u + xx * mu_ref[j:j + 1, :]).astype(ref.dtype)


def _layer_norm(z, g, b, seq, nxt=None):
    t, d = z.shape
    row = pl.BlockSpec((1, d), lambda i: (0, 0))
    if nxt is None:
        ts = _pick(seq, (256, 128))
        tile = pl.BlockSpec((ts, d), lambda i: (i, 0))
        return pl.pallas_call(
            _ln_kernel, out_shape=jax.ShapeDtypeStruct((t, d), F32), grid=(t // ts,),
            in_specs=[tile, row, row], out_specs=tile,
            compiler_params=_params("parallel"), name="layer_norm",
        )(z, g, b)
    if nxt[0] == "mod":
        ts = _pick(seq, (256, 128))
        tile = pl.BlockSpec((ts, d), lambda i: (i, 0))
        vec = pl.BlockSpec((1, 1, d), lambda i: ((i * ts) // seq, 0, 0))
        return pl.pallas_call(
            _ln_mod_kernel,
            out_shape=(jax.ShapeDtypeStruct((t, d), F32), jax.ShapeDtypeStruct((t, d), BF16)),
            grid=(t // ts,),
            in_specs=[tile, row, row, vec, vec], out_specs=(tile, tile),
            compiler_params=_params("parallel"), name="layer_norm_mod",
        )(z, g, b, nxt[1], nxt[2])
    _, sc, sh, mu = nxt
    nmix = mu.shape[0]
    ts = _pick(seq, (128,))
    sub = 8
    tile = pl.BlockSpec((ts, d), lambda i: (i, 0))
    prev = pl.BlockSpec((sub, d), lambda i: (jnp.maximum(i * (ts // sub) - 1, 0), 0))
    vec = pl.BlockSpec((1, 1, d), lambda i: ((i * ts) // seq, 0, 0))
    return pl.pallas_call(
        functools.partial(_ln_shift_kernel, ts=ts, seq=seq),
        out_shape=(jax.ShapeDtypeStruct((t, d), F32),) + (jax.ShapeDtypeStruct((t, d), BF16),) * nmix,
        grid=(t // ts,),
        in_specs=[tile, prev, row, row, vec, vec, pl.BlockSpec((nmix, d), lambda i: (0, 0))],
        out_specs=(tile,) * (1 + nmix),
        compiler_params=_params("parallel"), name="layer_norm_shift",
    )(z, z, g, b, sc, sh, mu)


def _ml_gate_act(y):
    lane = lax.broadcasted_iota(jnp.int32, y.shape, 1)
    i_pre = ML_GATE_CAP * jnp.tanh(y / ML_GATE_CAP)
    log_f = -_softplus(-y)
    return jnp.where(lane < ML_HEADS, i_pre, log_f)


def _mlstm_kernel(q_ref, k_ref, v_ref, o_ref, icol_ref, fcol_ref, irow_ref, frow_ref, ng_ref, out_ref,
                  c_sc, n_sc, m_sc, *, heads):
    L = ML_CHUNK

    @pl.when(pl.program_id(2) == 0)
    def _():
        c_sc[...] = jnp.zeros_like(c_sc)
        n_sc[...] = jnp.zeros_like(n_sc)
        m_sc[...] = jnp.zeros_like(m_sc)

    dqk = q_ref.shape[1] // heads
    dv = v_ref.shape[1] // heads
    k_scale = dqk ** -0.5
    r_idx = lax.broadcasted_iota(jnp.int32, (L, L), 0)
    c_idx = lax.broadcasted_iota(jnp.int32, (L, L), 1)
    causal = r_idx >= c_idx
    anti = r_idx <= c_idx

    hs = range(heads)
    q = [q_ref[:, h * dqk:(h + 1) * dqk] for h in hs]
    k = [k_ref[:, h * dqk:(h + 1) * dqk] for h in hs]
    v = [v_ref[:, h * dv:(h + 1) * dv] for h in hs]
    i_col, f_col = [icol_ref[h] for h in hs], [fcol_ref[h] for h in hs]
    i_row, f_row = [irow_ref[h] for h in hs], [frow_ref[h] for h in hs]
    c_st, n_st, m_st = [c_sc[h] for h in hs], [n_sc[h] for h in hs], [m_sc[h] for h in hs]

    qk = [_dot_nt(q[h], k[h]) for h in hs]
    q_c = [_dot(q[h], c_st[h].astype(BF16)) for h in hs]
    g_col = [jnp.sum(jnp.where(causal, f_row[h], 0.0), axis=1, keepdims=True) for h in hs]
    g_row = [jnp.sum(jnp.where(anti, f_col[h], 0.0), axis=0, keepdims=True) for h in hs]
    g_last = [jnp.sum(f_row[h], axis=1, keepdims=True) for h in hs]
    log_d = [jnp.where(causal, g_col[h] - g_row[h] + i_row[h], -jnp.inf) for h in hs]
    log_inter = [g_col[h] + m_st[h] for h in hs]
    m_row = [jnp.maximum(jnp.max(log_d[h], axis=1, keepdims=True), log_inter[h]) for h in hs]
    scores = [qk[h] * k_scale * jnp.exp(log_d[h] - m_row[h]) for h in hs]
    inter = [jnp.exp(log_inter[h] - m_row[h]) for h in hs]
    num = [_dot(scores[h].astype(BF16), v[h]) + inter[h] * q_c[h] for h in hs]
    q_n = [jnp.sum(q[h].astype(F32) * n_st[h], axis=1, keepdims=True) for h in hs]
    den = [jnp.sum(scores[h], axis=1, keepdims=True) + inter[h] * q_n[h] for h in hs]
    for h in hs:
        hid = num[h] / jnp.maximum(jnp.abs(den[h]), jnp.exp(-m_row[h]))
        hid = hid * lax.rsqrt(jnp.mean(hid * hid, axis=1, keepdims=True) + ML_NORM_EPS)
        sl = slice(h * dv, (h + 1) * dv)
        out_ref[:, sl] = (hid * ng_ref[:, sl] * _sigmoid(o_ref[:, sl].astype(F32))).astype(out_ref.dtype)

    log_w = [g_last[h] - g_col[h] + i_col[h] for h in hs]
    m_new = [jnp.maximum(g_last[h] + m_st[h], jnp.max(log_w[h], axis=0, keepdims=True)) for h in hs]
    wk = [k[h].astype(F32) * (k_scale * jnp.exp(log_w[h] - m_new[h])) for h in hs]
    kv = [_dot_tn(wk[h].astype(BF16), v[h]) for h in hs]
    for h in hs:
        decay = jnp.exp(g_last[h] + m_st[h] - m_new[h])
        c_sc[h] = decay * c_st[h] + kv[h]
        n_sc[h] = decay * n_st[h] + jnp.sum(wk[h], axis=0, keepdims=True)
        m_sc[h] = m_new[h]


def _mlstm(proj, gates, norm_g, batch, seq):
    t = proj.shape[0]
    H, L = ML_HEADS, ML_CHUNK
    G = 4
    dv = norm_g.shape[1] // H
    dqk = dv // 2
    nc = seq // L
    ng = H // G
    g = gates[:, :2 * H].reshape(batch, seq, 2 * H).transpose(0, 2, 1)
    i_g, f_g = g[:, :H], g[:, H:]
    col = lambda a: a.reshape(batch, H, seq, 1)
    rowv = lambda a: a.reshape(batch, H, nc, 1, L)
    col_spec = pl.BlockSpec((None, G, L, 1), lambda b, h, c: (b, h, c, 0))
    row_spec = pl.BlockSpec((None, G, None, 1, L), lambda b, h, c: (b, h, c, 0, 0))
    tok = lambda b, c: b * nc + c
    return pl.pallas_call(
        functools.partial(_mlstm_kernel, heads=G),
        out_shape=jax.ShapeDtypeStruct((t, H * dv), BF16),
        grid=(batch, ng, nc),
        in_specs=[pl.BlockSpec((L, G * dqk), lambda b, h, c: (tok(b, c), h)),
                  pl.BlockSpec((L, G * dqk), lambda b, h, c: (tok(b, c), ng + h)),
                  pl.BlockSpec((L, G * dv), lambda b, h, c: (tok(b, c), ng + h)),
                  pl.BlockSpec((L, G * dv), lambda b, h, c: (tok(b, c), 2 * ng + h)),
                  col_spec, col_spec, row_spec, row_spec,
                  pl.BlockSpec((1, G * dv), lambda b, h, c: (0, h))],
        out_specs=pl.BlockSpec((L, G * dv), lambda b, h, c: (tok(b, c), h)),
        scratch_shapes=[pltpu.VMEM((G, dqk, dv), F32), pltpu.VMEM((G, 1, dqk), F32), pltpu.VMEM((G, 1, 1), F32)],
        compiler_params=_params("parallel", "parallel", "arbitrary"),
        name="mlstm_chunk",
    )(proj, proj, proj, proj, col(i_g), col(f_g), rowv(i_g), rowv(f_g), norm_g)


def _rwkv_kernel(r_ref, k_ref, v_ref, lw_ref, la_ref, lgate_ref, w2_ref, a2_ref, g2_ref, w0_ref, a0_ref,
                 kk_ref, ka_ref, rk_ref, lg_ref, lb_ref, o_ref, s_sc, *, heads):
    L, N = RW_CHUNK, RW_HEAD
    gw = heads * N

    @pl.when(pl.program_id(2) == 0)
    def _():
        s_sc[...] = jnp.zeros_like(s_sc)

    P = 2 * L
    pairs = heads // 2

    r, k, v = r_ref[...], k_ref[...], v_ref[...]
    w_pre = w0_ref[...] + _dot(lw_ref[...], w2_ref[...])
    a_pre = a0_ref[...] + _dot(la_ref[...], a2_ref[...])
    gate = _dot(lgate_ref[...], g2_ref[...])
    log_decay = -jnp.exp(-_softplus(-w_pre) - 0.5)
    a = _sigmoid(a_pre)

    sw = min(gw, 256)
    li = lax.broadcasted_iota(jnp.int32, (sw, sw), 0) // N
    lj = lax.broadcasted_iota(jnp.int32, (sw, sw), 1) // N
    head_ones = jnp.where(li == lj, 1.0, 0.0).astype(BF16)

    def head_sum(x, split):
        parts = split(x)
        cols = []
        for c0 in range(0, gw, sw):
            acc = None
            for p in parts:
                d = _dot(p[:, c0:c0 + sw], head_ones)
                acc = d if acc is None else acc + d
            cols.append(acc)
        return cols[0] if len(cols) == 1 else jnp.concatenate(cols, axis=1)

    one_pass = lambda x: (x.astype(BF16),)

    t_i = lax.broadcasted_iota(jnp.int32, (L, L), 0)
    t_j = lax.broadcasted_iota(jnp.int32, (L, L), 1)
    tri = jnp.where(t_i >= t_j, 1.0, 0.0).astype(BF16)

    kkr = k * kk_ref[...]
    k = k * (1.0 + (a - 1.0) * ka_ref[...])
    sums = head_sum(jnp.concatenate([kkr * kkr, r * k * rk_ref[...]], axis=0), one_pass)
    kk = kkr * lax.rsqrt(jnp.maximum(sums[:L], 1e-24))
    bonus = sums[L:] * v

    cl = _dot_01(tri, log_decay)
    cl_last = cl[L - 1:L, :]
    gam = jnp.exp(cl)
    inv_gam = jnp.exp(-cl)
    gam_end = jnp.exp(cl_last)
    to_end = gam_end * inv_gam
    kka = kk * a
    r_hat = r * gam
    a_hat = -kk * jnp.exp(cl - log_decay)
    b_hat = kka * inv_gam
    k_hat = k * inv_gam
    b_end = kka * to_end
    k_end = k * to_end

    p_i = lax.broadcasted_iota(jnp.int32, (P, P), 0)
    p_j = lax.broadcasted_iota(jnp.int32, (P, P), 1)
    same = p_i // L == p_j // L
    strict = jnp.logical_and(same, p_i > p_j)
    incl = jnp.logical_and(same, p_i >= p_j)
    eye = jnp.where(p_i == p_j, 1.0, 0.0)
    levels = []
    s = 2
    while s < L:
        levels.append(jnp.logical_and(p_i // (2 * s) == p_j // (2 * s),
                                      jnp.logical_and(p_i % (2 * s) >= s, p_j % (2 * s) < s)))
        s *= 2
    own = (lax.broadcasted_iota(jnp.int32, (P, 2 * N), 0) // L
           == lax.broadcasted_iota(jnp.int32, (P, 2 * N), 1) // N)

    def stack(x):
        return jnp.where(own, jnp.concatenate([x, x], axis=0), 0.0).astype(BF16)

    rng = range(pairs)
    sls = [slice(p * 2 * N, (p + 1) * 2 * N) for p in rng]
    st = [s_sc[p] for p in rng]
    ar_s = [jnp.concatenate([stack(a_hat[:, sl]), stack(r_hat[:, sl])], axis=0) for sl in sls]
    v_s = [stack(v[:, sl]) for sl in sls]
    m4 = [_dot_nt(ar_s[p], jnp.concatenate([stack(b_hat[:, sls[p]]), stack(k_hat[:, sls[p]])], axis=0))
          for p in rng]
    a_ab = [jnp.where(strict, m[:P, :P], 0.0) for m in m4]
    a_ak = [jnp.where(strict, m[:P, P:], 0.0).astype(BF16) for m in m4]
    a_r = [jnp.concatenate([jnp.where(incl, m[P:, :P], 0.0), jnp.where(incl, m[P:, P:], 0.0)], axis=1).astype(BF16)
           for m in m4]
    from_state = [_dot_nt(ar_s[p], st[p].astype(BF16)) for p in rng]
    rhs_u = [from_state[p][:P] + _dot(a_ak[p], v_s[p]) for p in rng]
    x = [eye + jnp.where(p_i // 2 == p_j // 2, m, 0.0) for m in a_ab]
    for blk in levels:
        xb = [xp.astype(BF16) for xp in x]
        half = [_dot(xb[p], jnp.where(blk, a_ab[p], 0.0).astype(BF16)).astype(BF16) for p in rng]
        x = [x[p] + _dot(half[p], xb[p]) for p in rng]
    su = [_dot(x[p].astype(BF16), rhs_u[p].astype(BF16)) for p in rng]
    su_v = [jnp.concatenate([su[p].astype(BF16), v_s[p]], axis=0) for p in rng]
    y_s = [from_state[p][P:] + _dot(a_r[p], su_v[p]) for p in rng]
    for p in rng:
        be = jnp.concatenate([stack(b_end[:, sls[p]]), stack(k_end[:, sls[p]])], axis=0)
        s_sc[p] = st[p] * gam_end[:, sls[p]] + _dot_tn(su_v[p], be)
    ys = [yp[:L] + yp[L:] for yp in y_s]
    y = ys[0] if pairs == 1 else jnp.concatenate(ys, axis=1)

    inv_n = 1.0 / N
    mean = head_sum(y, one_pass) * inv_n
    yc = y - mean
    var = head_sum(yc * yc, one_pass) * inv_n
    yn = yc * lax.rsqrt(var + RW_GN_EPS) * lg_ref[...] + lb_ref[...]
    o_ref[...] = ((yn + bonus) * gate).astype(o_ref.dtype)


def _rwkv(r, k, v, lw, la, lgate, w2, a2, g2, layer, w0, a0, k_k, k_a, r_k, lnx_g, lnx_b, batch, seq):
    t, d = r.shape
    L = RW_CHUNK
    heads = min(32, d // RW_HEAD)
    gw = heads * RW_HEAD
    nc = seq // L
    tile = pl.BlockSpec((L, gw), lambda b, h, c: (b * nc + c, h))
    row = pl.BlockSpec((1, gw), lambda b, h, c: (0, h))
    lora = lambda f: pl.BlockSpec((L, f.shape[1]), lambda b, h, c: (b * nc + c, 0))
    lora_w = lambda w: pl.BlockSpec((None, w.shape[1], gw), lambda b, h, c: (layer, 0, h))
    return pl.pallas_call(
        functools.partial(_rwkv_kernel, heads=heads),
        out_shape=jax.ShapeDtypeStruct((t, d), BF16),
        grid=(batch, d // gw, nc),
        in_specs=[tile] * 3 + [lora(lw), lora(la), lora(lgate), lora_w(w2), lora_w(a2), lora_w(g2)] + [row] * 7,
        out_specs=tile,
        scratch_shapes=[pltpu.VMEM((heads // 2, 2 * RW_HEAD, 2 * RW_HEAD), F32)],
        compiler_params=_params("parallel", "parallel", "arbitrary"),
        name="rwkv7_chunk",
    )(r, k, v, lw, la, lgate, w2, a2, g2, w0, a0, k_k, k_a, r_k, lnx_g, lnx_b)


def kernel(x, c, ada_w, ada_b, mix_ln_g, mix_ln_b, ffn_ln_g, ffn_ln_b, ffn_w_in, ffn_w_out, ml_w_in, ml_b_i, ml_b_f, ml_norm_g, ml_w_out, rw_mu, rw_w_r, rw_w_k, rw_w_v, rw_w0, rw_w1, rw_w2, rw_a0, rw_a1, rw_a2, rw_g1, rw_g2, rw_k_k, rw_k_a, rw_r_k, rw_lnx_g, rw_lnx_b, rw_w_o):
    batch, seq, d = x.shape
    depth = ada_w.shape[0]
    t = batch * seq
    alpha = (2 * depth) ** 0.25
    hidden = ffn_w_out.shape[1]
    n_mixers = 2

    mod = _ada(c, ada_w, ada_b).reshape(depth, batch, 6, 1, d)
    row = lambda p: p.reshape(1, -1)
    zero_bias = lambda n: jnp.zeros((1, n), F32)
    ffn_w_out_bf16 = ffn_w_out.astype(BF16)

    x2 = x.reshape(t, d)
    u = None
    for layer in range(depth):
        sh_m, sc_m, gt_m, sh_f, sc_f, gt_f = [mod[layer, :, i] for i in range(6)]
        j = layer // n_mixers
        if layer % n_mixers == 0:
            if u is None:
                u = _modulate(x2, sc_m, sh_m, seq)
            qkvo = ml_w_in.shape[2] - 2 * ML_HEADS
            proj = _mm(u, jnp.swapaxes(ml_w_in, 1, 2), j, qkvo, zero_bias(qkvo), _identity, BF16, "mlstm_in",
                       w_is_nk=True)
            w_gate = jnp.pad(ml_w_in[j:j + 1, :, qkvo:], ((0, 0), (0, 0), (0, LANES - 2 * ML_HEADS)))
            b_gate = jnp.pad(jnp.concatenate([ml_b_i[j], ml_b_f[j]]), (0, LANES - 2 * ML_HEADS)).reshape(1, LANES)
            gates = _mm(u, w_gate, 0, LANES, b_gate, _ml_gate_act, F32, "mlstm_gates")
            y = _mlstm(proj, gates, row(ml_norm_g[j]), batch, seq)
            z = _mm_resid(y, ml_w_out.astype(BF16), j, x2, gt_m, alpha, seq, "mlstm_out")
            x2, u = _layer_norm(z, row(mix_ln_g[layer]), row(mix_ln_b[layer]), seq, ("mod", sc_f, sh_f))
        else:
            xr, xw, xk, xv, xa, xg = u
            nb = zero_bias
            r = _mm(xr, rw_w_r, j, d, nb(d), _identity, F32, "rwkv_r")
            k = _mm(xk, rw_w_k, j, d, nb(d), _identity, F32, "rwkv_k")
            v = _mm(xv, rw_w_v, j, d, nb(d), _identity, F32, "rwkv_v")
            n_w, n_a, n_g = rw_w1.shape[2], rw_a1.shape[2], rw_g1.shape[2]
            lw = _mm(xw, rw_w1, j, n_w, nb(n_w), jnp.tanh, BF16, "rwkv_w1")
            la = _mm(xa, rw_a1, j, n_a, nb(n_a), _identity, BF16, "rwkv_a1")
            lg = _mm(xg, rw_g1, j, n_g, nb(n_g), _sigmoid, BF16, "rwkv_g1")
            y = _rwkv(r, k, v, lw, la, lg, rw_w2.astype(BF16), rw_a2.astype(BF16), rw_g2.astype(BF16), j,
                      row(rw_w0[j]), row(rw_a0[j]), row(rw_k_k[j]), row(rw_k_a[j]), row(rw_r_k[j]),
                      row(rw_lnx_g[j]), row(rw_lnx_b[j]), batch, seq)
            z = _mm_resid(y, rw_w_o.astype(BF16), j, x2, gt_m, alpha, seq, "rwkv_out")
            x2, u = _layer_norm(z, row(mix_ln_g[layer]), row(mix_ln_b[layer]), seq, ("mod", sc_f, sh_f))
        hid = _mm_swiglu(u, ffn_w_in, layer, hidden)
        z = _mm_resid(hid, ffn_w_out_bf16, layer, x2, gt_f, alpha, seq, "ffn_out")
        g_f, b_f = row(ffn_ln_g[layer]), row(ffn_ln_b[layer])
        if layer + 1 == depth:
            x2, u = _layer_norm(z, g_f, b_f, seq), None
        else:
            nsh, nsc = mod[layer + 1, :, 0], mod[layer + 1, :, 1]
            if (layer + 1) % n_mixers == 0:
                x2, u = _layer_norm(z, g_f, b_f, seq, ("mod", nsc, nsh))
            else:
                out = _layer_norm(z, g_f, b_f, seq, ("shift", nsc, nsh, rw_mu[(layer + 1) // n_mixers]))
                x2, u = out[0], out[1:]
    return x2.reshape(batch, seq, d)
```

```python
import functools

import jax
import jax.numpy as jnp
from jax import lax
from jax.experimental import pallas as pl
from jax.experimental.pallas import tpu as pltpu

F32 = jnp.float32
BF16 = jnp.bfloat16

ML_HEADS = 8
ML_CHUNK = 128
ML_GATE_CAP = 15.0
ML_NORM_EPS = 1e-6
RW_HEAD = 64
RW_CHUNK = 64
RW_GN_EPS = 64e-5
LN_EPS = 1e-5
LANES = 128
SUBLANES = 8
VMEM_LIMIT_BYTES = 56 * 1024 * 1024
TILE_VMEM_BUDGET_BYTES = 52 * 1024 * 1024


def _params(*semantics):
    return pltpu.CompilerParams(dimension_semantics=semantics, vmem_limit_bytes=VMEM_LIMIT_BYTES)


def _pick(n, candidates):
    for c in candidates:
        if n % c == 0:
            return c
    return n


def _dot(a, b):
    return jnp.dot(a, b, preferred_element_type=F32)


def _dot_nt(a, b):
    return lax.dot_general(a, b, (((1,), (1,)), ((), ())), preferred_element_type=F32)


def _dot_tn(a, b):
    return lax.dot_general(a, b, (((0,), (0,)), ((), ())), preferred_element_type=F32)


def _split3(x):
    hi = x.astype(BF16)
    r = x - hi.astype(F32)
    mid = r.astype(BF16)
    lo = (r - mid.astype(F32)).astype(BF16)
    return hi, mid, lo


def _dot_01(m, x):
    hi, mid, lo = _split3(x)
    return _dot(m, hi) + (_dot(m, mid) + _dot(m, lo))


def _sigmoid(x):
    return 1.0 / (1.0 + jnp.exp(-x))


def _softplus(x):
    return jnp.maximum(x, 0.0) + jnp.log1p(jnp.exp(-jnp.abs(x)))


def _ada_kernel(c_ref, w_ref, b_ref, o_ref):
    c = c_ref[...]
    c_act = (c * _sigmoid(c)).astype(BF16)
    o_ref[0] = _dot(c_act, w_ref[0].astype(BF16)) + b_ref[0]


def _ada(c, ada_w, ada_b):
    depth, d, n = ada_w.shape
    b = c.shape[0]
    tn = _pick(n, (512, 256, 128))
    return pl.pallas_call(
        _ada_kernel,
        out_shape=jax.ShapeDtypeStruct((depth, b, n), F32),
        grid=(depth, n // tn),
        in_specs=[pl.BlockSpec((b, d), lambda l, j: (0, 0)),
                  pl.BlockSpec((1, d, tn), lambda l, j: (l, 0, j)),
                  pl.BlockSpec((1, 1, tn), lambda l, j: (l, 0, j))],
        out_specs=pl.BlockSpec((1, b, tn), lambda l, j: (l, 0, j)),
        compiler_params=_params("parallel", "parallel"),
        name="ada_mod",
    )(c, ada_w, ada_b.reshape(depth, 1, n))


def _modulate_kernel(x_ref, sc_ref, sh_ref, o_ref):
    o_ref[...] = (x_ref[...] * (1.0 + sc_ref[0]) + sh_ref[0]).astype(o_ref.dtype)


def _modulate(x2, sc, sh, seq):
    t, d = x2.shape
    ts = _pick(seq, (512, 256, 128))
    vec = pl.BlockSpec((1, 1, d), lambda i: ((i * ts) // seq, 0, 0))
    return pl.pallas_call(
        _modulate_kernel,
        out_shape=jax.ShapeDtypeStruct((t, d), BF16),
        grid=(t // ts,),
        in_specs=[pl.BlockSpec((ts, d), lambda i: (i, 0)), vec, vec],
        out_specs=pl.BlockSpec((ts, d), lambda i: (i, 0)),
        compiler_params=_params("parallel"),
        name="modulate",
    )(x2, sc, sh)


def _mm_kernel(a_ref, w_ref, b_ref, o_ref, *, act, w_is_nk):
    dot = _dot_nt if w_is_nk else _dot
    y = dot(a_ref[...], w_ref[...].astype(BF16)) + b_ref[...]
    o_ref[...] = act(y).astype(o_ref.dtype)


def _mm(a, w, layer, n, bias, act, out_dtype, name, w_is_nk=False):
    t, k = a.shape
    tm = _pick(t, (2048, 1024, 512, 256, 128))
    out_bytes = jnp.dtype(out_dtype).itemsize
    fits = lambda c: 2 * (tm * k * 2 + k * c * 4 + tm * c * out_bytes) <= TILE_VMEM_BUDGET_BYTES
    tn = _pick(n, [c for c in (512, 256, 128) if fits(c)])
    if w_is_nk:
        w_spec = pl.BlockSpec((None, tn, k), lambda i, j: (layer, j, 0))
    else:
        w_spec = pl.BlockSpec((None, k, tn), lambda i, j: (layer, 0, j))
    return pl.pallas_call(
        functools.partial(_mm_kernel, act=act, w_is_nk=w_is_nk),
        out_shape=jax.ShapeDtypeStruct((t, n), out_dtype),
        grid=(t // tm, n // tn),
        in_specs=[pl.BlockSpec((tm, k), lambda i, j: (i, 0)),
                  w_spec,
                  pl.BlockSpec((1, tn), lambda i, j: (0, j))],
        out_specs=pl.BlockSpec((tm, tn), lambda i, j: (i, j)),
        compiler_params=_params("parallel", "parallel"),
        name=name,
    )(a, w, bias)


def _identity(y):
    return y


def _swiglu_kernel(a_ref, wg_ref, wu_ref, o_ref):
    a = a_ref[...]
    gate = _dot(a, wg_ref[...].astype(BF16))
    up = _dot(a, wu_ref[...].astype(BF16))
    o_ref[...] = (gate * _sigmoid(gate) * up).astype(o_ref.dtype)


def _mm_swiglu(a, w_in, layer, hidden):
    t, k = a.shape
    tm = _pick(t, (2048, 1024, 512, 256, 128))
    tn = _pick(hidden, (256, 128))
    nj = hidden // tn
    return pl.pallas_call(
        _swiglu_kernel,
        out_shape=jax.ShapeDtypeStruct((t, hidden), BF16),
        grid=(t // tm, nj),
        in_specs=[pl.BlockSpec((tm, k), lambda i, j: (i, 0)),
                  pl.BlockSpec((None, k, tn), lambda i, j: (layer, 0, j)),
                  pl.BlockSpec((None, k, tn), lambda i, j: (layer, 0, j + nj))],
        out_specs=pl.BlockSpec((tm, tn), lambda i, j: (i, j)),
        compiler_params=_params("parallel", "parallel"),
        name="ffn_in_swiglu",
    )(a, w_in, w_in)


def _ln_stats(z):
    mu = jnp.mean(z, axis=-1, keepdims=True)
    zc = z - mu
    var = jnp.mean(zc * zc, axis=-1, keepdims=True)
    return mu, lax.rsqrt(var + LN_EPS)


def _ln_apply(z, mu, rstd, g, b):
    return (z - mu) * rstd * g + b


def _resid_kernel(a_ref, w_ref, x_ref, gt_ref, *rest, alpha, normed):
    if normed:
        st_ref, g_ref, b_ref, o_ref = rest
        st = st_ref[...]
        x = _ln_apply(x_ref[...], st[:, 0:1], st[:, 1:2], g_ref[...], b_ref[...])
    else:
        o_ref, = rest
        x = x_ref[...]
    o_ref[...] = alpha * x + gt_ref[0] * _dot(a_ref[...], w_ref[...])


def _mm_resid(a, w, layer, res, gt, alpha, seq, name):
    t, k = a.shape
    n = w.shape[2]
    if k <= 4096:
        tm, tn = _pick(min(t, seq), (1024, 512, 256, 128)), _pick(n, (512, 256, 128))
    else:
        tm, tn = _pick(min(t, seq), (512, 256, 128)), _pick(n, (512, 256, 128))
    normed = len(res) > 1
    in_specs = [pl.BlockSpec((tm, k), lambda i, j: (i, 0)),
                pl.BlockSpec((None, k, tn), lambda i, j: (layer, 0, j)),
                pl.BlockSpec((tm, tn), lambda i, j: (i, j)),
                pl.BlockSpec((1, 1, tn), lambda i, j: ((i * tm) // seq, 0, j))]
    operands = [a, w, res[0], gt]
    if normed:
        in_specs += [pl.BlockSpec((tm, 2), lambda i, j: (i, 0)),
                     pl.BlockSpec((1, tn), lambda i, j: (0, j)),
                     pl.BlockSpec((1, tn), lambda i, j: (0, j))]
        operands += list(res[1:])
    return pl.pallas_call(
        functools.partial(_resid_kernel, alpha=alpha, normed=normed),
        out_shape=jax.ShapeDtypeStruct((t, n), F32),
        grid=(t // tm, n // tn),
        in_specs=in_specs,
        out_specs=pl.BlockSpec((tm, tn), lambda i, j: (i, j)),
        compiler_params=_params("parallel", "parallel"),
        name=name,
    )(*operands)


def _ln_kernel(z_ref, g_ref, b_ref, x_ref):
    z = z_ref[...]
    mu, rstd = _ln_stats(z)
    x_ref[...] = _ln_apply(z, mu, rstd, g_ref[...], b_ref[...])


def _ln_mod_kernel(z_ref, g_ref, b_ref, sc_ref, sh_ref, st_ref, u_ref):
    z = z_ref[...]
    mu, rstd = _ln_stats(z)
    st_ref[:, 0:1] = mu
    st_ref[:, 1:2] = rstd
    x = _ln_apply(z, mu, rstd, g_ref[...], b_ref[...])
    u_ref[...] = (x * (1.0 + sc_ref[0]) + sh_ref[0]).astype(u_ref.dtype)


def _ln_shift_kernel(z_ref, zp_ref, g_ref, b_ref, sc_ref, sh_ref, mu_ref, st_ref, *mix_refs, ts, seq):
    g, b = g_ref[...], b_ref[...]
    scale, shift = 1.0 + sc_ref[0], sh_ref[0]
    z = z_ref[...]
    mu, rstd = _ln_stats(z)
    st_ref[:, 0:1] = mu
    st_ref[:, 1:2] = rstd
    u = _ln_apply(z, mu, rstd, g, b) * scale + shift
    zp = zp_ref[...]
    prev = _ln_apply(zp, *_ln_stats(zp), g, b)[SUBLANES - 1:SUBLANES, :] * scale + shift
    first = (pl.program_id(0) * ts) % seq == 0
    prev = jnp.where(first, 0.0, prev)
    row = lax.broadcasted_iota(jnp.int32, u.shape, 0)
    u_prev = jnp.where(row == 0, prev, pltpu.roll(u, shift=1, axis=0))
    xx = u_prev - u
    for j, ref in enumerate(mix_refs):
        ref[...] = (u + xx * mu_ref[j:j + 1, :]).astype(ref.dtype)


def _layer_norm(z, g, b, seq, nxt=None):
    t, d = z.shape
    row = pl.BlockSpec((1, d), lambda i: (0, 0))
    stats = jax.ShapeDtypeStruct((t, 2), F32)
    if nxt is None:
        ts = _pick(seq, (256, 128))
        tile = pl.BlockSpec((ts, d), lambda i: (i, 0))
        return pl.pallas_call(
            _ln_kernel, out_shape=jax.ShapeDtypeStruct((t, d), F32), grid=(t // ts,),
            in_specs=[tile, row, row], out_specs=tile,
            compiler_params=_params("parallel"), name="layer_norm",
        )(z, g, b)
    if nxt[0] == "mod":
        ts = _pick(seq, (256, 128))
        tile = pl.BlockSpec((ts, d), lambda i: (i, 0))
        vec = pl.BlockSpec((1, 1, d), lambda i: ((i * ts) // seq, 0, 0))
        return pl.pallas_call(
            _ln_mod_kernel,
            out_shape=(stats, jax.ShapeDtypeStruct((t, d), BF16)),
            grid=(t // ts,),
            in_specs=[tile, row, row, vec, vec],
            out_specs=(pl.BlockSpec((ts, 2), lambda i: (i, 0)), tile),
            compiler_params=_params("parallel"), name="layer_norm_mod",
        )(z, g, b, nxt[1], nxt[2])
    _, sc, sh, mu = nxt
    nmix = mu.shape[0]
    ts = _pick(seq, (128,))
    tile = pl.BlockSpec((ts, d), lambda i: (i, 0))
    prev = pl.BlockSpec((SUBLANES, d), lambda i: (jnp.maximum(i * (ts // SUBLANES) - 1, 0), 0))
    vec = pl.BlockSpec((1, 1, d), lambda i: ((i * ts) // seq, 0, 0))
    return pl.pallas_call(
        functools.partial(_ln_shift_kernel, ts=ts, seq=seq),
        out_shape=(stats,) + (jax.ShapeDtypeStruct((t, d), BF16),) * nmix,
        grid=(t // ts,),
        in_specs=[tile, prev, row, row, vec, vec, pl.BlockSpec((nmix, d), lambda i: (0, 0))],
        out_specs=(pl.BlockSpec((ts, 2), lambda i: (i, 0)),) + (tile,) * nmix,
        compiler_params=_params("parallel"), name="layer_norm_shift",
    )(z, z, g, b, sc, sh, mu)


def _ml_gate_act(y):
    lane = lax.broadcasted_iota(jnp.int32, y.shape, 1)
    i_pre = ML_GATE_CAP * jnp.tanh(y / ML_GATE_CAP)
    log_f = -_softplus(-y)
    return jnp.where(lane < ML_HEADS, i_pre, log_f)


def _mlstm_kernel(q_ref, k_ref, v_ref, o_ref, icol_ref, fcol_ref, irow_ref, frow_ref, ng_ref, out_ref,
                  c_sc, n_sc, m_sc, *, heads):
    L = ML_CHUNK

    @pl.when(pl.program_id(2) == 0)
    def _():
        c_sc[...] = jnp.zeros_like(c_sc)
        n_sc[...] = jnp.zeros_like(n_sc)
        m_sc[...] = jnp.zeros_like(m_sc)

    dqk = q_ref.shape[1] // heads
    dv = v_ref.shape[1] // heads
    k_scale = dqk ** -0.5
    r_idx = lax.broadcasted_iota(jnp.int32, (L, L), 0)
    c_idx = lax.broadcasted_iota(jnp.int32, (L, L), 1)
    causal = r_idx >= c_idx
    anti = r_idx <= c_idx

    hs = range(heads)
    q = [q_ref[:, h * dqk:(h + 1) * dqk] for h in hs]
    k = [k_ref[:, h * dqk:(h + 1) * dqk] for h in hs]
    v = [v_ref[:, h * dv:(h + 1) * dv] for h in hs]
    i_col, f_col = [icol_ref[h] for h in hs], [fcol_ref[h] for h in hs]
    i_row, f_row = [irow_ref[h] for h in hs], [frow_ref[h] for h in hs]
    c_st, n_st, m_st = [c_sc[h] for h in hs], [n_sc[h] for h in hs], [m_sc[h] for h in hs]

    qk = [_dot_nt(q[h], k[h]) for h in hs]
    q_c = [_dot(q[h], c_st[h].astype(BF16)) for h in hs]
    g_col = [jnp.sum(jnp.where(causal, f_row[h], 0.0), axis=1, keepdims=True) for h in hs]
    g_row = [jnp.sum(jnp.where(anti, f_col[h], 0.0), axis=0, keepdims=True) for h in hs]
    g_last = [jnp.sum(f_row[h], axis=1, keepdims=True) for h in hs]
    log_d = [jnp.where(causal, g_col[h] - g_row[h] + i_row[h], -jnp.inf) for h in hs]
    log_inter = [g_col[h] + m_st[h] for h in hs]
    m_row = [jnp.maximum(jnp.max(log_d[h], axis=1, keepdims=True), log_inter[h]) for h in hs]
    scores = [qk[h] * k_scale * jnp.exp(log_d[h] - m_row[h]) for h in hs]
    inter = [jnp.exp(log_inter[h] - m_row[h]) for h in hs]
    num = [_dot(scores[h].astype(BF16), v[h]) + inter[h] * q_c[h] for h in hs]
    q_n = [jnp.sum(q[h].astype(F32) * n_st[h], axis=1, keepdims=True) for h in hs]
    den = [jnp.sum(scores[h], axis=1, keepdims=True) + inter[h] * q_n[h] for h in hs]
    for h in hs:
        hid = num[h] / jnp.maximum(jnp.abs(den[h]), jnp.exp(-m_row[h]))
        hid = hid * lax.rsqrt(jnp.mean(hid * hid, axis=1, keepdims=True) + ML_NORM_EPS)
        sl = slice(h * dv, (h + 1) * dv)
        out_ref[:, sl] = (hid * ng_ref[:, sl] * _sigmoid(o_ref[:, sl].astype(F32))).astype(out_ref.dtype)

    log_w = [g_last[h] - g_col[h] + i_col[h] for h in hs]
    m_new = [jnp.maximum(g_last[h] + m_st[h], jnp.max(log_w[h], axis=0, keepdims=True)) for h in hs]
    wk = [k[h].astype(F32) * (k_scale * jnp.exp(log_w[h] - m_new[h])) for h in hs]
    kv = [_dot_tn(wk[h].astype(BF16), v[h]) for h in hs]
    for h in hs:
        decay = jnp.exp(g_last[h] + m_st[h] - m_new[h])
        c_sc[h] = decay * c_st[h] + kv[h]
        n_sc[h] = decay * n_st[h] + jnp.sum(wk[h], axis=0, keepdims=True)
        m_sc[h] = m_new[h]


def _mlstm(proj, gates, norm_g, batch, seq):
    t = proj.shape[0]
    H, L = ML_HEADS, ML_CHUNK
    G = 4
    dv = norm_g.shape[1] // H
    dqk = dv // 2
    nc = seq // L
    ng = H // G
    g = gates[:, :2 * H].reshape(batch, seq, 2 * H).transpose(0, 2, 1)
    i_g, f_g = g[:, :H], g[:, H:]
    col = lambda a: a.reshape(batch, H, seq, 1)
    rowv = lambda a: a.reshape(batch, H, nc, 1, L)
    col_spec = pl.BlockSpec((None, G, L, 1), lambda b, h, c: (b, h, c, 0))
    row_spec = pl.BlockSpec((None, G, None, 1, L), lambda b, h, c: (b, h, c, 0, 0))
    tok = lambda b, c: b * nc + c
    return pl.pallas_call(
        functools.partial(_mlstm_kernel, heads=G),
        out_shape=jax.ShapeDtypeStruct((t, H * dv), BF16),
        grid=(batch, ng, nc),
        in_specs=[pl.BlockSpec((L, G * dqk), lambda b, h, c: (tok(b, c), h)),
                  pl.BlockSpec((L, G * dqk), lambda b, h, c: (tok(b, c), ng + h)),
                  pl.BlockSpec((L, G * dv), lambda b, h, c: (tok(b, c), ng + h)),
                  pl.BlockSpec((L, G * dv), lambda b, h, c: (tok(b, c), 2 * ng + h)),
                  col_spec, col_spec, row_spec, row_spec,
                  pl.BlockSpec((1, G * dv), lambda b, h, c: (0, h))],
        out_specs=pl.BlockSpec((L, G * dv), lambda b, h, c: (tok(b, c), h)),
        scratch_shapes=[pltpu.VMEM((G, dqk, dv), F32), pltpu.VMEM((G, 1, dqk), F32), pltpu.VMEM((G, 1, 1), F32)],
        compiler_params=_params("parallel", "parallel", "arbitrary"),
        name="mlstm_chunk",
    )(proj, proj, proj, proj, col(i_g), col(f_g), rowv(i_g), rowv(f_g), norm_g)


def _rwkv_kernel(r_ref, k_ref, v_ref, lw_ref, la_ref, lgate_ref, w2_ref, a2_ref, g2_ref, w0_ref, a0_ref,
                 kk_ref, ka_ref, rk_ref, lg_ref, lb_ref, o_ref, s_sc, *, heads):
    L, N = RW_CHUNK, RW_HEAD
    gw = heads * N

    @pl.when(pl.program_id(2) == 0)
    def _():
        s_sc[...] = jnp.zeros_like(s_sc)

    P = 2 * L
    pairs = heads // 2

    r, k, v = r_ref[...].astype(F32), k_ref[...].astype(F32), v_ref[...].astype(F32)
    w_pre = w0_ref[...] + _dot(lw_ref[...], w2_ref[...])
    a_pre = a0_ref[...] + _dot(la_ref[...], a2_ref[...])
    gate = _dot(lgate_ref[...], g2_ref[...])
    log_decay = -jnp.exp(-_softplus(-w_pre) - 0.5)
    a = _sigmoid(a_pre)

    sw = min(gw, 256)
    li = lax.broadcasted_iota(jnp.int32, (sw, sw), 0) // N
    lj = lax.broadcasted_iota(jnp.int32, (sw, sw), 1) // N
    head_ones = jnp.where(li == lj, 1.0, 0.0).astype(BF16)

    def head_sum(x, split):
        parts = split(x)
        cols = []
        for c0 in range(0, gw, sw):
            acc = None
            for p in parts:
                d = _dot(p[:, c0:c0 + sw], head_ones)
                acc = d if acc is None else acc + d
            cols.append(acc)
        return cols[0] if len(cols) == 1 else jnp.concatenate(cols, axis=1)

    one_pass = lambda x: (x.astype(BF16),)

    t_i = lax.broadcasted_iota(jnp.int32, (L, L), 0)
    t_j = lax.broadcasted_iota(jnp.int32, (L, L), 1)
    tri = jnp.where(t_i >= t_j, 1.0, 0.0).astype(BF16)

    kkr = k * kk_ref[...]
    k = k * (1.0 + (a - 1.0) * ka_ref[...])
    sums = head_sum(jnp.concatenate([kkr * kkr, r * k * rk_ref[...]], axis=0), one_pass)
    kk = kkr * lax.rsqrt(jnp.maximum(sums[:L], 1e-24))
    bonus = sums[L:] * v

    cl = _dot_01(tri, log_decay)
    cl_last = cl[L - 1:L, :]
    gam = jnp.exp(cl)
    inv_gam = jnp.exp(-cl)
    gam_end = jnp.exp(cl_last)
    to_end = gam_end * inv_gam
    kka = kk * a
    r_hat = r * gam
    a_hat = -kk * jnp.exp(cl - log_decay)
    b_hat = kka * inv_gam
    k_hat = k * inv_gam
    b_end = kka * to_end
    k_end = k * to_end

    p_i = lax.broadcasted_iota(jnp.int32, (P, P), 0)
    p_j = lax.broadcasted_iota(jnp.int32, (P, P), 1)
    same = p_i // L == p_j // L
    strict = jnp.logical_and(same, p_i > p_j)
    incl = jnp.logical_and(same, p_i >= p_j)
    eye = jnp.where(p_i == p_j, 1.0, 0.0)
    levels = []
    s = 2
    while s < L:
        levels.append((s, jnp.logical_and(p_i // (2 * s) == p_j // (2 * s),
                                          jnp.logical_and(p_i % (2 * s) >= s, p_j % (2 * s) < s))))
        s *= 2
    own = (lax.broadcasted_iota(jnp.int32, (P, 2 * N), 0) // L
           == lax.broadcasted_iota(jnp.int32, (P, 2 * N), 1) // N)

    def stack(x):
        return jnp.where(own, jnp.concatenate([x, x], axis=0), 0.0).astype(BF16)

    rng = range(pairs)
    sls = [slice(p * 2 * N, (p + 1) * 2 * N) for p in rng]
    st = [s_sc[p] for p in rng]
    ar_s = [jnp.concatenate([stack(a_hat[:, sl]), stack(r_hat[:, sl])], axis=0) for sl in sls]
    v_s = [stack(v[:, sl]) for sl in sls]
    m4 = [_dot_nt(ar_s[p], jnp.concatenate([stack(b_hat[:, sls[p]]), stack(k_hat[:, sls[p]])], axis=0))
          for p in rng]
    a_ab = [jnp.where(strict, m[:P, :P], 0.0) for m in m4]
    a_ak = [jnp.where(strict, m[:P, P:], 0.0).astype(BF16) for m in m4]
    a_r = [jnp.concatenate([jnp.where(incl, m[P:, :P], 0.0), jnp.where(incl, m[P:, P:], 0.0)], axis=1).astype(BF16)
           for m in m4]
    from_state = [_dot_nt(ar_s[p], st[p].astype(BF16)) for p in rng]
    rhs_u = [from_state[p][:P] + _dot(a_ak[p], v_s[p]) for p in rng]
    x = [eye + jnp.where(p_i // 2 == p_j // 2, m, 0.0) for m in a_ab]
    for s, blk in levels:
        xb = [xp.astype(BF16) for xp in x]
        m_s = [jnp.where(blk, a_ab[p], 0.0).astype(BF16) for p in rng]
        if s % SUBLANES:
            half = [_dot(xb[p], m_s[p]).astype(BF16) for p in rng]
            x = [x[p] + _dot(half[p], xb[p]) for p in rng]
        else:
            lower = [jnp.concatenate([xp[b + s:b + 2 * s] for b in range(0, P, 2 * s)], axis=0) for xp in x]
            half = [_dot(lower[p].astype(BF16), m_s[p]).astype(BF16) for p in rng]
            lower = [lower[p] + _dot(half[p], xb[p]) for p in rng]
            x = [jnp.concatenate(
                [blk_rows for i, b in enumerate(range(0, P, 2 * s))
                 for blk_rows in (x[p][b:b + s], lower[p][i * s:(i + 1) * s])], axis=0) for p in rng]
    su = [_dot(x[p].astype(BF16), rhs_u[p].astype(BF16)) for p in rng]
    su_v = [jnp.concatenate([su[p].astype(BF16), v_s[p]], axis=0) for p in rng]
    y_s = [from_state[p][P:] + _dot(a_r[p], su_v[p]) for p in rng]
    for p in rng:
        be = jnp.concatenate([stack(b_end[:, sls[p]]), stack(k_end[:, sls[p]])], axis=0)
        s_sc[p] = st[p] * gam_end[:, sls[p]] + _dot_tn(su_v[p], be)
    ys = [yp[:L] + yp[L:] for yp in y_s]
    y = ys[0] if pairs == 1 else jnp.concatenate(ys, axis=1)

    inv_n = 1.0 / N
    mean = head_sum(y, one_pass) * inv_n
    yc = y - mean
    var = head_sum(yc * yc, one_pass) * inv_n
    yn = yc * lax.rsqrt(var + RW_GN_EPS) * lg_ref[...] + lb_ref[...]
    o_ref[...] = ((yn + bonus) * gate).astype(o_ref.dtype)


def _rwkv(r, k, v, lw, la, lgate, w2, a2, g2, layer, w0, a0, k_k, k_a, r_k, lnx_g, lnx_b, batch, seq):
    t, d = r.shape
    L = RW_CHUNK
    heads = min(64, d // RW_HEAD)
    gw = heads * RW_HEAD
    nc = seq // L
    tile = pl.BlockSpec((L, gw), lambda b, h, c: (b * nc + c, h))
    row = pl.BlockSpec((1, gw), lambda b, h, c: (0, h))
    lora = lambda f: pl.BlockSpec((L, f.shape[1]), lambda b, h, c: (b * nc + c, 0))
    lora_w = lambda w: pl.BlockSpec((None, w.shape[1], gw), lambda b, h, c: (layer, 0, h))
    return pl.pallas_call(
        functools.partial(_rwkv_kernel, heads=heads),
        out_shape=jax.ShapeDtypeStruct((t, d), BF16),
        grid=(batch, d // gw, nc),
        in_specs=[tile] * 3 + [lora(lw), lora(la), lora(lgate), lora_w(w2), lora_w(a2), lora_w(g2)] + [row] * 7,
        out_specs=tile,
        scratch_shapes=[pltpu.VMEM((heads // 2, 2 * RW_HEAD, 2 * RW_HEAD), F32)],
        compiler_params=_params("parallel", "parallel", "arbitrary"),
        name="rwkv7_chunk",
    )(r, k, v, lw, la, lgate, w2, a2, g2, w0, a0, k_k, k_a, r_k, lnx_g, lnx_b)


def kernel(x, c, ada_w, ada_b, mix_ln_g, mix_ln_b, ffn_ln_g, ffn_ln_b, ffn_w_in, ffn_w_out, ml_w_in, ml_b_i, ml_b_f, ml_norm_g, ml_w_out, rw_mu, rw_w_r, rw_w_k, rw_w_v, rw_w0, rw_w1, rw_w2, rw_a0, rw_a1, rw_a2, rw_g1, rw_g2, rw_k_k, rw_k_a, rw_r_k, rw_lnx_g, rw_lnx_b, rw_w_o):
    batch, seq, d = x.shape
    depth = ada_w.shape[0]
    t = batch * seq
    alpha = (2 * depth) ** 0.25
    hidden = ffn_w_out.shape[1]
    n_mixers = 2

    mod = _ada(c, ada_w, ada_b).reshape(depth, batch, 6, 1, d)
    row = lambda p: p.reshape(1, -1)
    zero_bias = lambda n: jnp.zeros((1, n), F32)
    ffn_w_out_bf16 = ffn_w_out.astype(BF16)

    res = (x.reshape(t, d),)
    u = None
    for layer in range(depth):
        sh_m, sc_m, gt_m, sh_f, sc_f, gt_f = [mod[layer, :, i] for i in range(6)]
        j = layer // n_mixers
        g_m, b_m = row(mix_ln_g[layer]), row(mix_ln_b[layer])
        if layer % n_mixers == 0:
            if u is None:
                u = _modulate(res[0], sc_m, sh_m, seq)
            qkvo = ml_w_in.shape[2] - 2 * ML_HEADS
            proj = _mm(u, jnp.swapaxes(ml_w_in, 1, 2), j, qkvo, zero_bias(qkvo), _identity, BF16, "mlstm_in",
                       w_is_nk=True)
            w_gate = jnp.pad(ml_w_in[j:j + 1, :, qkvo:], ((0, 0), (0, 0), (0, LANES - 2 * ML_HEADS)))
            b_gate = jnp.pad(jnp.concatenate([ml_b_i[j], ml_b_f[j]]), (0, LANES - 2 * ML_HEADS)).reshape(1, LANES)
            gates = _mm(u, w_gate, 0, LANES, b_gate, _ml_gate_act, F32, "mlstm_gates")
            y = _mlstm(proj, gates, row(ml_norm_g[j]), batch, seq)
            z = _mm_resid(y, ml_w_out.astype(BF16), j, res, gt_m, alpha, seq, "mlstm_out")
        else:
            xr, xw, xk, xv, xa, xg = u
            nb = zero_bias
            r = _mm(xr, rw_w_r, j, d, nb(d), _identity, BF16, "rwkv_r")
            k = _mm(xk, rw_w_k, j, d, nb(d), _identity, BF16, "rwkv_k")
            v = _mm(xv, rw_w_v, j, d, nb(d), _identity, BF16, "rwkv_v")
            n_w, n_a, n_g = rw_w1.shape[2], rw_a1.shape[2], rw_g1.shape[2]
            lw = _mm(xw, rw_w1, j, n_w, nb(n_w), jnp.tanh, BF16, "rwkv_w1")
            la = _mm(xa, rw_a1, j, n_a, nb(n_a), _identity, BF16, "rwkv_a1")
            lg = _mm(xg, rw_g1, j, n_g, nb(n_g), _sigmoid, BF16, "rwkv_g1")
            y = _rwkv(r, k, v, lw, la, lg, rw_w2.astype(BF16), rw_a2.astype(BF16), rw_g2.astype(BF16), j,
                      row(rw_w0[j]), row(rw_a0[j]), row(rw_k_k[j]), row(rw_k_a[j]), row(rw_r_k[j]),
                      row(rw_lnx_g[j]), row(rw_lnx_b[j]), batch, seq)
            z = _mm_resid(y, rw_w_o.astype(BF16), j, res, gt_m, alpha, seq, "rwkv_out")
        stats, u = _layer_norm(z, g_m, b_m, seq, ("mod", sc_f, sh_f))
        res = (z, stats, g_m, b_m)
        hid = _mm_swiglu(u, ffn_w_in, layer, hidden)
        z = _mm_resid(hid, ffn_w_out_bf16, layer, res, gt_f, alpha, seq, "ffn_out")
        g_f, b_f = row(ffn_ln_g[layer]), row(ffn_ln_b[layer])
        if layer + 1 == depth:
            return _layer_norm(z, g_f, b_f, seq).reshape(batch, seq, d)
        nsh, nsc = mod[layer + 1, :, 0], mod[layer + 1, :, 1]
        if (layer + 1) % n_mixers == 0:
            stats, u = _layer_norm(z, g_f, b_f, seq, ("mod", nsc, nsh))
        else:
            out = _layer_norm(z, g_f, b_f, seq, ("shift", nsc, nsh, rw_mu[(layer + 1) // n_mixers]))
            stats, u = out[0], out[1:]
        res = (z, stats, g_f, b_f)
```

```python
import functools

import jax
import jax.numpy as jnp
from jax import lax
from jax.experimental import pallas as pl
from jax.experimental.pallas import tpu as pltpu

F32 = jnp.float32
BF16 = jnp.bfloat16

ML_HEADS = 8
ML_CHUNK = 128
ML_GATE_CAP = 15.0
ML_NORM_EPS = 1e-6
RW_HEAD = 64
RW_CHUNK = 64
RW_GN_EPS = 64e-5
LN_EPS = 1e-5
LANES = 128
SUBLANES = 8
VMEM_LIMIT_BYTES = 56 * 1024 * 1024
TILE_VMEM_BUDGET_BYTES = 52 * 1024 * 1024


def _params(*semantics):
    return pltpu.CompilerParams(dimension_semantics=semantics, vmem_limit_bytes=VMEM_LIMIT_BYTES)


def _pick(n, candidates):
    for c in candidates:
        if n % c == 0:
            return c
    return n


def _dot(a, b):
    return jnp.dot(a, b, preferred_element_type=F32)


def _dot_nt(a, b):
    return lax.dot_general(a, b, (((1,), (1,)), ((), ())), preferred_element_type=F32)


def _dot_tn(a, b):
    return lax.dot_general(a, b, (((0,), (0,)), ((), ())), preferred_element_type=F32)


def _split3(x):
    hi = x.astype(BF16)
    r = x - hi.astype(F32)
    mid = r.astype(BF16)
    lo = (r - mid.astype(F32)).astype(BF16)
    return hi, mid, lo


def _dot_01(m, x):
    hi, mid, lo = _split3(x)
    return _dot(m, hi) + (_dot(m, mid) + _dot(m, lo))


def _sigmoid(x):
    return 1.0 / (1.0 + jnp.exp(-x))


def _softplus(x):
    return jnp.maximum(x, 0.0) + jnp.log1p(jnp.exp(-jnp.abs(x)))


def _ada_kernel(c_ref, w_ref, b_ref, o_ref):
    c = c_ref[...]
    c_act = (c * _sigmoid(c)).astype(BF16)
    o_ref[0] = _dot(c_act, w_ref[0].astype(BF16)) + b_ref[0]


def _ada(c, ada_w, ada_b):
    depth, d, n = ada_w.shape
    b = c.shape[0]
    tn = _pick(n, (512, 256, 128))
    return pl.pallas_call(
        _ada_kernel,
        out_shape=jax.ShapeDtypeStruct((depth, b, n), F32),
        grid=(depth, n // tn),
        in_specs=[pl.BlockSpec((b, d), lambda l, j: (0, 0)),
                  pl.BlockSpec((1, d, tn), lambda l, j: (l, 0, j)),
                  pl.BlockSpec((1, 1, tn), lambda l, j: (l, 0, j))],
        out_specs=pl.BlockSpec((1, b, tn), lambda l, j: (l, 0, j)),
        compiler_params=_params("parallel", "parallel"),
        name="ada_mod",
    )(c, ada_w, ada_b.reshape(depth, 1, n))


def _modulate_kernel(x_ref, sc_ref, sh_ref, o_ref):
    o_ref[...] = (x_ref[...] * (1.0 + sc_ref[0]) + sh_ref[0]).astype(o_ref.dtype)


def _modulate(x2, sc, sh, seq):
    t, d = x2.shape
    ts = _pick(seq, (512, 256, 128))
    vec = pl.BlockSpec((1, 1, d), lambda i: ((i * ts) // seq, 0, 0))
    return pl.pallas_call(
        _modulate_kernel,
        out_shape=jax.ShapeDtypeStruct((t, d), BF16),
        grid=(t // ts,),
        in_specs=[pl.BlockSpec((ts, d), lambda i: (i, 0)), vec, vec],
        out_specs=pl.BlockSpec((ts, d), lambda i: (i, 0)),
        compiler_params=_params("parallel"),
        name="modulate",
    )(x2, sc, sh)


def _mm_kernel(a_ref, w_ref, b_ref, o_ref, *, act, w_is_nk):
    dot = _dot_nt if w_is_nk else _dot
    y = dot(a_ref[...], w_ref[...].astype(BF16)) + b_ref[...]
    o_ref[...] = act(y).astype(o_ref.dtype)


def _mm(a, w, layer, n, bias, act, out_dtype, name, w_is_nk=False):
    t, k = a.shape
    tm = _pick(t, (2048, 1024, 512, 256, 128))
    out_bytes = jnp.dtype(out_dtype).itemsize
    fits = lambda c: 2 * (tm * k * 2 + k * c * 4 + tm * c * out_bytes) <= TILE_VMEM_BUDGET_BYTES
    tn = _pick(n, [c for c in (512, 256, 128) if fits(c)])
    if w_is_nk:
        w_spec = pl.BlockSpec((None, tn, k), lambda i, j: (layer, j, 0))
    else:
        w_spec = pl.BlockSpec((None, k, tn), lambda i, j: (layer, 0, j))
    return pl.pallas_call(
        functools.partial(_mm_kernel, act=act, w_is_nk=w_is_nk),
        out_shape=jax.ShapeDtypeStruct((t, n), out_dtype),
        grid=(t // tm, n // tn),
        in_specs=[pl.BlockSpec((tm, k), lambda i, j: (i, 0)),
                  w_spec,
                  pl.BlockSpec((1, tn), lambda i, j: (0, j))],
        out_specs=pl.BlockSpec((tm, tn), lambda i, j: (i, j)),
        compiler_params=_params("parallel", "parallel"),
        name=name,
    )(a, w, bias)


def _identity(y):
    return y


def _swiglu_kernel(a_ref, wg_ref, wu_ref, o_ref):
    a = a_ref[...]
    gate = _dot(a, wg_ref[...].astype(BF16))
    up = _dot(a, wu_ref[...].astype(BF16))
    o_ref[...] = (gate * _sigmoid(gate) * up).astype(o_ref.dtype)


def _mm_swiglu(a, w_in, layer, hidden):
    t, k = a.shape
    tm = _pick(t, (2048, 1024, 512, 256, 128))
    tn = _pick(hidden, (256, 128))
    nj = hidden // tn
    return pl.pallas_call(
        _swiglu_kernel,
        out_shape=jax.ShapeDtypeStruct((t, hidden), BF16),
        grid=(t // tm, nj),
        in_specs=[pl.BlockSpec((tm, k), lambda i, j: (i, 0)),
                  pl.BlockSpec((None, k, tn), lambda i, j: (layer, 0, j)),
                  pl.BlockSpec((None, k, tn), lambda i, j: (layer, 0, j + nj))],
        out_specs=pl.BlockSpec((tm, tn), lambda i, j: (i, j)),
        compiler_params=_params("parallel", "parallel"),
        name="ffn_in_swiglu",
    )(a, w_in, w_in)


def _ln_apply(z, mu, rstd, g, b):
    return (z - mu) * rstd * g + b


def _resid_kernel(a_ref, w_ref, x_ref, gt_ref, *rest, alpha, normed):
    if normed:
        st_ref, g_ref, b_ref, o_ref = rest
        st = st_ref[...]
        x = _ln_apply(x_ref[...], st[:, 0:1], st[:, 1:2], g_ref[...], b_ref[...])
    else:
        o_ref, = rest
        x = x_ref[...]
    o_ref[...] = alpha * x + gt_ref[0] * _dot(a_ref[...], w_ref[...])


def _mm_resid(a, w, layer, res, gt, alpha, seq, name):
    t, k = a.shape
    n = w.shape[2]
    normed = len(res) > 1
    if k <= 4096:
        tm = _pick(min(t, seq), (1024, 512, 256, 128))
        tn = _pick(n, (512, 256, 128) if normed else (1024, 512, 256, 128))
    else:
        tm, tn = _pick(min(t, seq), (512, 256, 128)), _pick(n, (512, 256, 128))
    in_specs = [pl.BlockSpec((tm, k), lambda i, j: (i, 0)),
                pl.BlockSpec((None, k, tn), lambda i, j: (layer, 0, j)),
                pl.BlockSpec((tm, tn), lambda i, j: (i, j)),
                pl.BlockSpec((1, 1, tn), lambda i, j: ((i * tm) // seq, 0, j))]
    operands = [a, w, res[0], gt]
    if normed:
        in_specs += [pl.BlockSpec((tm, 2), lambda i, j: (i, 0)),
                     pl.BlockSpec((1, tn), lambda i, j: (0, j)),
                     pl.BlockSpec((1, tn), lambda i, j: (0, j))]
        operands += list(res[1:])
    return pl.pallas_call(
        functools.partial(_resid_kernel, alpha=alpha, normed=normed),
        out_shape=jax.ShapeDtypeStruct((t, n), F32),
        grid=(t // tm, n // tn),
        in_specs=in_specs,
        out_specs=pl.BlockSpec((tm, tn), lambda i, j: (i, j)),
        compiler_params=_params("parallel", "parallel"),
        name=name,
    )(*operands)


def _col_chunks(d):
    cw = _pick(d, (512, 256, 128))
    return [slice(c, c + cw) for c in range(0, d, cw)]


def _ln_stats_ref(z_ref):
    d = z_ref.shape[1]
    chunks = _col_chunks(d)
    mu = jnp.sum(sum(z_ref[:, c] for c in chunks), axis=-1, keepdims=True) * (1.0 / d)
    sq = jnp.sum(sum(jnp.square(z_ref[:, c] - mu) for c in chunks), axis=-1, keepdims=True)
    return mu, lax.rsqrt(sq * (1.0 / d) + LN_EPS)


def _ln_stats(z):
    mu = jnp.mean(z, axis=-1, keepdims=True)
    zc = z - mu
    var = jnp.mean(zc * zc, axis=-1, keepdims=True)
    return mu, lax.rsqrt(var + LN_EPS)


def _ln_kernel(z_ref, g_ref, b_ref, x_ref):
    z = z_ref[...]
    mu, rstd = _ln_stats(z)
    x_ref[...] = _ln_apply(z, mu, rstd, g_ref[...], b_ref[...])


def _ln_mod_kernel(z_ref, g_ref, b_ref, sc_ref, sh_ref, st_ref, u_ref):
    z = z_ref[...]
    mu, rstd = _ln_stats(z)
    st_ref[:, 0:1] = mu
    st_ref[:, 1:2] = rstd
    x = _ln_apply(z, mu, rstd, g_ref[...], b_ref[...])
    u_ref[...] = (x * (1.0 + sc_ref[0]) + sh_ref[0]).astype(u_ref.dtype)


def _ln_shift_kernel(z_ref, zp_ref, g_ref, b_ref, sc_ref, sh_ref, mu_ref, st_ref, *mix_refs, ts, seq):
    mu, rstd = _ln_stats_ref(z_ref)
    st_ref[:, 0:1] = mu
    st_ref[:, 1:2] = rstd
    mu_p, rstd_p = _ln_stats_ref(zp_ref)
    first = (pl.program_id(0) * ts) % seq == 0
    row = lax.broadcasted_iota(jnp.int32, (ts, 1), 0)
    for c in _col_chunks(z_ref.shape[1]):
        g, b = g_ref[:, c], b_ref[:, c]
        scale, shift = 1.0 + sc_ref[0, :, c], sh_ref[0, :, c]
        u = _ln_apply(z_ref[:, c], mu, rstd, g, b) * scale + shift
        prev = _ln_apply(zp_ref[:, c], mu_p, rstd_p, g, b)[SUBLANES - 1:SUBLANES, :] * scale + shift
        prev = jnp.where(first, 0.0, prev)
        u_prev = jnp.where(row == 0, prev, pltpu.roll(u, shift=1, axis=0))
        xx = (u_prev - u).astype(BF16)
        u = u.astype(BF16)
        for j, ref in enumerate(mix_refs):
            ref[:, c] = u + xx * mu_ref[j:j + 1, c].astype(BF16)


def _layer_norm(z, g, b, seq, nxt=None):
    t, d = z.shape
    row = pl.BlockSpec((1, d), lambda i: (0, 0))
    stats = jax.ShapeDtypeStruct((t, 2), F32)
    if nxt is None:
        ts = _pick(seq, (256, 128))
        tile = pl.BlockSpec((ts, d), lambda i: (i, 0))
        return pl.pallas_call(
            _ln_kernel, out_shape=jax.ShapeDtypeStruct((t, d), F32), grid=(t // ts,),
            in_specs=[tile, row, row], out_specs=tile,
            compiler_params=_params("parallel"), name="layer_norm",
        )(z, g, b)
    if nxt[0] == "mod":
        ts = _pick(seq, (256, 128))
        tile = pl.BlockSpec((ts, d), lambda i: (i, 0))
        vec = pl.BlockSpec((1, 1, d), lambda i: ((i * ts) // seq, 0, 0))
        return pl.pallas_call(
            _ln_mod_kernel,
            out_shape=(stats, jax.ShapeDtypeStruct((t, d), BF16)),
            grid=(t // ts,),
            in_specs=[tile, row, row, vec, vec],
            out_specs=(pl.BlockSpec((ts, 2), lambda i: (i, 0)), tile),
            compiler_params=_params("parallel"), name="layer_norm_mod",
        )(z, g, b, nxt[1], nxt[2])
    _, sc, sh, mu = nxt
    nmix = mu.shape[0]
    ts = _pick(seq, (128,))
    tile = pl.BlockSpec((ts, d), lambda i: (i, 0))
    prev = pl.BlockSpec((SUBLANES, d), lambda i: (jnp.maximum(i * (ts // SUBLANES) - 1, 0), 0))
    vec = pl.BlockSpec((1, 1, d), lambda i: ((i * ts) // seq, 0, 0))
    return pl.pallas_call(
        functools.partial(_ln_shift_kernel, ts=ts, seq=seq),
        out_shape=(stats,) + (jax.ShapeDtypeStruct((t, d), BF16),) * nmix,
        grid=(t // ts,),
        in_specs=[tile, prev, row, row, vec, vec, pl.BlockSpec((nmix, d), lambda i: (0, 0))],
        out_specs=(pl.BlockSpec((ts, 2), lambda i: (i, 0)),) + (tile,) * nmix,
        compiler_params=_params("parallel"), name="layer_norm_shift",
    )(z, z, g, b, sc, sh, mu)


def _ml_gate_act(y):
    lane = lax.broadcasted_iota(jnp.int32, y.shape, 1)
    i_pre = ML_GATE_CAP * jnp.tanh(y / ML_GATE_CAP)
    log_f = -_softplus(-y)
    return jnp.where(lane < ML_HEADS, i_pre, log_f)


def _mlstm_kernel(q_ref, k_ref, v_ref, o_ref, icol_ref, fcol_ref, irow_ref, frow_ref, ng_ref, out_ref,
                  c_sc, n_sc, m_sc, *, heads):
    L = ML_CHUNK

    @pl.when(pl.program_id(2) == 0)
    def _():
        c_sc[...] = jnp.zeros_like(c_sc)
        n_sc[...] = jnp.zeros_like(n_sc)
        m_sc[...] = jnp.zeros_like(m_sc)

    dqk = q_ref.shape[1] // heads
    dv = v_ref.shape[1] // heads
    k_scale = dqk ** -0.5
    r_idx = lax.broadcasted_iota(jnp.int32, (L, L), 0)
    c_idx = lax.broadcasted_iota(jnp.int32, (L, L), 1)
    causal = r_idx >= c_idx
    anti = r_idx <= c_idx

    hs = range(heads)
    q = [q_ref[:, h * dqk:(h + 1) * dqk] for h in hs]
    k = [k_ref[:, h * dqk:(h + 1) * dqk] for h in hs]
    v = [v_ref[:, h * dv:(h + 1) * dv] for h in hs]
    i_col, f_col = [icol_ref[h] for h in hs], [fcol_ref[h] for h in hs]
    i_row, f_row = [irow_ref[h] for h in hs], [frow_ref[h] for h in hs]
    c_st, n_st, m_st = [c_sc[h] for h in hs], [n_sc[h] for h in hs], [m_sc[h] for h in hs]

    qk = [_dot_nt(q[h], k[h]) for h in hs]
    q_c = [_dot(q[h], c_st[h].astype(BF16)) for h in hs]
    g_col = [jnp.sum(jnp.where(causal, f_row[h], 0.0), axis=1, keepdims=True) for h in hs]
    g_row = [jnp.sum(jnp.where(anti, f_col[h], 0.0), axis=0, keepdims=True) for h in hs]
    g_last = [jnp.sum(f_row[h], axis=1, keepdims=True) for h in hs]
    log_d = [jnp.where(causal, g_col[h] - g_row[h] + i_row[h], -jnp.inf) for h in hs]
    log_inter = [g_col[h] + m_st[h] for h in hs]
    m_row = [jnp.maximum(jnp.max(log_d[h], axis=1, keepdims=True), log_inter[h]) for h in hs]
    scores = [qk[h] * k_scale * jnp.exp(log_d[h] - m_row[h]) for h in hs]
    inter = [jnp.exp(log_inter[h] - m_row[h]) for h in hs]
    num = [_dot(scores[h].astype(BF16), v[h]) + inter[h] * q_c[h] for h in hs]
    q_n = [jnp.sum(q[h].astype(F32) * n_st[h], axis=1, keepdims=True) for h in hs]
    den = [jnp.sum(scores[h], axis=1, keepdims=True) + inter[h] * q_n[h] for h in hs]
    for h in hs:
        hid = num[h] / jnp.maximum(jnp.abs(den[h]), jnp.exp(-m_row[h]))
        hid = hid * lax.rsqrt(jnp.mean(hid * hid, axis=1, keepdims=True) + ML_NORM_EPS)
        sl = slice(h * dv, (h + 1) * dv)
        out_ref[:, sl] = (hid * ng_ref[:, sl] * _sigmoid(o_ref[:, sl].astype(F32))).astype(out_ref.dtype)

    log_w = [g_last[h] - g_col[h] + i_col[h] for h in hs]
    m_new = [jnp.maximum(g_last[h] + m_st[h], jnp.max(log_w[h], axis=0, keepdims=True)) for h in hs]
    wk = [k[h].astype(F32) * (k_scale * jnp.exp(log_w[h] - m_new[h])) for h in hs]
    kv = [_dot_tn(wk[h].astype(BF16), v[h]) for h in hs]
    for h in hs:
        decay = jnp.exp(g_last[h] + m_st[h] - m_new[h])
        c_sc[h] = decay * c_st[h] + kv[h]
        n_sc[h] = decay * n_st[h] + jnp.sum(wk[h], axis=0, keepdims=True)
        m_sc[h] = m_new[h]


def _mlstm(proj, gates, norm_g, batch, seq):
    t = proj.shape[0]
    H, L = ML_HEADS, ML_CHUNK
    G = 4
    dv = norm_g.shape[1] // H
    dqk = dv // 2
    nc = seq // L
    ng = H // G
    g = gates[:, :2 * H].reshape(batch, seq, 2 * H).transpose(0, 2, 1)
    i_g, f_g = g[:, :H], g[:, H:]
    col = lambda a: a.reshape(batch, H, seq, 1)
    rowv = lambda a: a.reshape(batch, H, nc, 1, L)
    col_spec = pl.BlockSpec((None, G, L, 1), lambda b, h, c: (b, h, c, 0))
    row_spec = pl.BlockSpec((None, G, None, 1, L), lambda b, h, c: (b, h, c, 0, 0))
    tok = lambda b, c: b * nc + c
    return pl.pallas_call(
        functools.partial(_mlstm_kernel, heads=G),
        out_shape=jax.ShapeDtypeStruct((t, H * dv), BF16),
        grid=(batch, ng, nc),
        in_specs=[pl.BlockSpec((L, G * dqk), lambda b, h, c: (tok(b, c), h)),
                  pl.BlockSpec((L, G * dqk), lambda b, h, c: (tok(b, c), ng + h)),
                  pl.BlockSpec((L, G * dv), lambda b, h, c: (tok(b, c), ng + h)),
                  pl.BlockSpec((L, G * dv), lambda b, h, c: (tok(b, c), 2 * ng + h)),
                  col_spec, col_spec, row_spec, row_spec,
                  pl.BlockSpec((1, G * dv), lambda b, h, c: (0, h))],
        out_specs=pl.BlockSpec((L, G * dv), lambda b, h, c: (tok(b, c), h)),
        scratch_shapes=[pltpu.VMEM((G, dqk, dv), F32), pltpu.VMEM((G, 1, dqk), F32), pltpu.VMEM((G, 1, 1), F32)],
        compiler_params=_params("parallel", "parallel", "arbitrary"),
        name="mlstm_chunk",
    )(proj, proj, proj, proj, col(i_g), col(f_g), rowv(i_g), rowv(f_g), norm_g)


def _rwkv_kernel(r_ref, k_ref, v_ref, lw_ref, la_ref, lgate_ref, w2_ref, a2_ref, g2_ref, w0_ref, a0_ref,
                 kk_ref, ka_ref, rk_ref, lg_ref, lb_ref, o_ref, s_sc, *, heads):
    L, N = RW_CHUNK, RW_HEAD
    gw = heads * N

    @pl.when(pl.program_id(2) == 0)
    def _():
        s_sc[...] = jnp.zeros_like(s_sc)

    P = 2 * L
    pairs = heads // 2

    r, k, v = r_ref[...].astype(F32), k_ref[...].astype(F32), v_ref[...].astype(F32)
    w_pre = w0_ref[...] + _dot(lw_ref[...], w2_ref[...])
    a_pre = a0_ref[...] + _dot(la_ref[...], a2_ref[...])
    gate = _dot(lgate_ref[...], g2_ref[...])
    log_decay = -jnp.exp(-_softplus(-w_pre) - 0.5)
    a = _sigmoid(a_pre)

    sw = min(gw, 256)
    li = lax.broadcasted_iota(jnp.int32, (sw, sw), 0) // N
    lj = lax.broadcasted_iota(jnp.int32, (sw, sw), 1) // N
    head_ones = jnp.where(li == lj, 1.0, 0.0).astype(BF16)

    def head_sum(x, split):
        parts = split(x)
        cols = []
        for c0 in range(0, gw, sw):
            acc = None
            for p in parts:
                d = _dot(p[:, c0:c0 + sw], head_ones)
                acc = d if acc is None else acc + d
            cols.append(acc)
        return cols[0] if len(cols) == 1 else jnp.concatenate(cols, axis=1)

    one_pass = lambda x: (x.astype(BF16),)

    t_i = lax.broadcasted_iota(jnp.int32, (L, L), 0)
    t_j = lax.broadcasted_iota(jnp.int32, (L, L), 1)
    tri = jnp.where(t_i >= t_j, 1.0, 0.0).astype(BF16)

    kkr = k * kk_ref[...]
    k = k * (1.0 + (a - 1.0) * ka_ref[...])
    sums = head_sum(jnp.concatenate([kkr * kkr, r * k * rk_ref[...]], axis=0), one_pass)
    kk = kkr * lax.rsqrt(jnp.maximum(sums[:L], 1e-24))
    bonus = sums[L:] * v

    cl = _dot_01(tri, log_decay)
    cl_last = cl[L - 1:L, :]
    gam = jnp.exp(cl)
    inv_gam = jnp.exp(-cl)
    gam_end = jnp.exp(cl_last)
    to_end = gam_end * inv_gam
    kka = kk * a
    r_hat = r * gam
    a_hat = -kk * jnp.exp(cl - log_decay)
    b_hat = kka * inv_gam
    k_hat = k * inv_gam
    b_end = kka * to_end
    k_end = k * to_end

    p_i = lax.broadcasted_iota(jnp.int32, (P, P), 0)
    p_j = lax.broadcasted_iota(jnp.int32, (P, P), 1)
    same = p_i // L == p_j // L
    strict = jnp.logical_and(same, p_i > p_j)
    incl = jnp.logical_and(same, p_i >= p_j)
    eye = jnp.where(p_i == p_j, 1.0, 0.0)
    levels = []
    s = 2
    while s < L:
        levels.append((s, jnp.logical_and(p_i // (2 * s) == p_j // (2 * s),
                                          jnp.logical_and(p_i % (2 * s) >= s, p_j % (2 * s) < s))))
        s *= 2
    own = (lax.broadcasted_iota(jnp.int32, (P, 2 * N), 0) // L
           == lax.broadcasted_iota(jnp.int32, (P, 2 * N), 1) // N)

    def stack(x):
        return jnp.where(own, jnp.concatenate([x, x], axis=0), 0.0).astype(BF16)

    rng = range(pairs)
    sls = [slice(p * 2 * N, (p + 1) * 2 * N) for p in rng]
    st = [s_sc[p] for p in rng]
    ar_s = [jnp.concatenate([stack(a_hat[:, sl]), stack(r_hat[:, sl])], axis=0) for sl in sls]
    v_s = [stack(v[:, sl]) for sl in sls]
    m4 = [_dot_nt(ar_s[p], jnp.concatenate([stack(b_hat[:, sls[p]]), stack(k_hat[:, sls[p]])], axis=0))
          for p in rng]
    a_ab = [jnp.where(strict, m[:P, :P], 0.0) for m in m4]
    a_ak = [jnp.where(strict, m[:P, P:], 0.0).astype(BF16) for m in m4]
    a_r = [jnp.concatenate([jnp.where(incl, m[P:, :P], 0.0), jnp.where(incl, m[P:, P:], 0.0)], axis=1).astype(BF16)
           for m in m4]
    from_state = [_dot_nt(ar_s[p], st[p].astype(BF16)) for p in rng]
    rhs_u = [from_state[p][:P] + _dot(a_ak[p], v_s[p]) for p in rng]
    x = [eye + jnp.where(p_i // 2 == p_j // 2, m, 0.0) for m in a_ab]
    for s, blk in levels:
        xb = [xp.astype(BF16) for xp in x]
        m_s = [jnp.where(blk, a_ab[p], 0.0).astype(BF16) for p in rng]
        if s % SUBLANES:
            half = [_dot(xb[p], m_s[p]).astype(BF16) for p in rng]
            x = [x[p] + _dot(half[p], xb[p]) for p in rng]
        else:
            lower = [jnp.concatenate([xp[b + s:b + 2 * s] for b in range(0, P, 2 * s)], axis=0) for xp in x]
            half = [_dot(lower[p].astype(BF16), m_s[p]).astype(BF16) for p in rng]
            lower = [lower[p] + _dot(half[p], xb[p]) for p in rng]
            x = [jnp.concatenate(
                [blk_rows for i, b in enumerate(range(0, P, 2 * s))
                 for blk_rows in (x[p][b:b + s], lower[p][i * s:(i + 1) * s])], axis=0) for p in rng]
    su = [_dot(x[p].astype(BF16), rhs_u[p].astype(BF16)) for p in rng]
    su_v = [jnp.concatenate([su[p].astype(BF16), v_s[p]], axis=0) for p in rng]
    y_s = [from_state[p][P:] + _dot(a_r[p], su_v[p]) for p in rng]
    for p in rng:
        be = jnp.concatenate([stack(b_end[:, sls[p]]), stack(k_end[:, sls[p]])], axis=0)
        s_sc[p] = st[p] * gam_end[:, sls[p]] + _dot_tn(su_v[p], be)
    ys = [yp[:L] + yp[L:] for yp in y_s]
    y = ys[0] if pairs == 1 else jnp.concatenate(ys, axis=1)

    inv_n = 1.0 / N
    mean = head_sum(y, one_pass) * inv_n
    yc = y - mean
    var = head_sum(yc * yc, one_pass) * inv_n
    yn = yc * lax.rsqrt(var + RW_GN_EPS) * lg_ref[...] + lb_ref[...]
    o_ref[...] = ((yn + bonus) * gate).astype(o_ref.dtype)


def _rwkv(r, k, v, lw, la, lgate, w2, a2, g2, layer, w0, a0, k_k, k_a, r_k, lnx_g, lnx_b, batch, seq):
    t, d = r.shape
    L = RW_CHUNK
    heads = min(64, d // RW_HEAD)
    gw = heads * RW_HEAD
    nc = seq // L
    tile = pl.BlockSpec((L, gw), lambda b, h, c: (b * nc + c, h))
    row = pl.BlockSpec((1, gw), lambda b, h, c: (0, h))
    lora = lambda f: pl.BlockSpec((L, f.shape[1]), lambda b, h, c: (b * nc + c, 0))
    lora_w = lambda w: pl.BlockSpec((None, w.shape[1], gw), lambda b, h, c: (layer, 0, h))
    return pl.pallas_call(
        functools.partial(_rwkv_kernel, heads=heads),
        out_shape=jax.ShapeDtypeStruct((t, d), BF16),
        grid=(batch, d // gw, nc),
        in_specs=[tile] * 3 + [lora(lw), lora(la), lora(lgate), lora_w(w2), lora_w(a2), lora_w(g2)] + [row] * 7,
        out_specs=tile,
        scratch_shapes=[pltpu.VMEM((heads // 2, 2 * RW_HEAD, 2 * RW_HEAD), F32)],
        compiler_params=_params("parallel", "parallel", "arbitrary"),
        name="rwkv7_chunk",
    )(r, k, v, lw, la, lgate, w2, a2, g2, w0, a0, k_k, k_a, r_k, lnx_g, lnx_b)


def kernel(x, c, ada_w, ada_b, mix_ln_g, mix_ln_b, ffn_ln_g, ffn_ln_b, ffn_w_in, ffn_w_out, ml_w_in, ml_b_i, ml_b_f, ml_norm_g, ml_w_out, rw_mu, rw_w_r, rw_w_k, rw_w_v, rw_w0, rw_w1, rw_w2, rw_a0, rw_a1, rw_a2, rw_g1, rw_g2, rw_k_k, rw_k_a, rw_r_k, rw_lnx_g, rw_lnx_b, rw_w_o):
    batch, seq, d = x.shape
    depth = ada_w.shape[0]
    t = batch * seq
    alpha = (2 * depth) ** 0.25
    hidden = ffn_w_out.shape[1]
    n_mixers = 2

    mod = _ada(c, ada_w, ada_b).reshape(depth, batch, 6, 1, d)
    row = lambda p: p.reshape(1, -1)
    zero_bias = lambda n: jnp.zeros((1, n), F32)
    ffn_w_out_bf16 = ffn_w_out.astype(BF16)

    res = (x.reshape(t, d),)
    u = None
    for layer in range(depth):
        sh_m, sc_m, gt_m, sh_f, sc_f, gt_f = [mod[layer, :, i] for i in range(6)]
        j = layer // n_mixers
        g_m, b_m = row(mix_ln_g[layer]), row(mix_ln_b[layer])
        if layer % n_mixers == 0:
            if u is None:
                u = _modulate(res[0], sc_m, sh_m, seq)
            qkvo = ml_w_in.shape[2] - 2 * ML_HEADS
            proj = _mm(u, jnp.swapaxes(ml_w_in, 1, 2), j, qkvo, zero_bias(qkvo), _identity, BF16, "mlstm_in",
                       w_is_nk=True)
            w_gate = jnp.pad(ml_w_in[j:j + 1, :, qkvo:], ((0, 0), (0, 0), (0, LANES - 2 * ML_HEADS)))
            b_gate = jnp.pad(jnp.concatenate([ml_b_i[j], ml_b_f[j]]), (0, LANES - 2 * ML_HEADS)).reshape(1, LANES)
            gates = _mm(u, w_gate, 0, LANES, b_gate, _ml_gate_act, F32, "mlstm_gates")
            y = _mlstm(proj, gates, row(ml_norm_g[j]), batch, seq)
            z = _mm_resid(y, ml_w_out.astype(BF16), j, res, gt_m, alpha, seq, "mlstm_out")
        else:
            xr, xw, xk, xv, xa, xg = u
            nb = zero_bias
            r = _mm(xr, rw_w_r, j, d, nb(d), _identity, BF16, "rwkv_r")
            k = _mm(xk, rw_w_k, j, d, nb(d), _identity, BF16, "rwkv_k")
            v = _mm(xv, rw_w_v, j, d, nb(d), _identity, BF16, "rwkv_v")
            n_w, n_a, n_g = rw_w1.shape[2], rw_a1.shape[2], rw_g1.shape[2]
            lw = _mm(xw, rw_w1, j, n_w, nb(n_w), jnp.tanh, BF16, "rwkv_w1")
            la = _mm(xa, rw_a1, j, n_a, nb(n_a), _identity, BF16, "rwkv_a1")
            lg = _mm(xg, rw_g1, j, n_g, nb(n_g), _sigmoid, BF16, "rwkv_g1")
            y = _rwkv(r, k, v, lw, la, lg, rw_w2.astype(BF16), rw_a2.astype(BF16), rw_g2.astype(BF16), j,
                      row(rw_w0[j]), row(rw_a0[j]), row(rw_k_k[j]), row(rw_k_a[j]), row(rw_r_k[j]),
                      row(rw_lnx_g[j]), row(rw_lnx_b[j]), batch, seq)
            z = _mm_resid(y, rw_w_o.astype(BF16), j, res, gt_m, alpha, seq, "rwkv_out")
        stats, u = _layer_norm(z, g_m, b_m, seq, ("mod", sc_f, sh_f))
        res = (z, stats, g_m, b_m)
        hid = _mm_swiglu(u, ffn_w_in, layer, hidden)
        z = _mm_resid(hid, ffn_w_out_bf16, layer, res, gt_f, alpha, seq, "ffn_out")
        g_f, b_f = row(ffn_ln_g[layer]), row(ffn_ln_b[layer])
        if layer + 1 == depth:
            return _layer_norm(z, g_f, b_f, seq).reshape(batch, seq, d)
        nsh, nsc = mod[layer + 1, :, 0], mod[layer + 1, :, 1]
        if (layer + 1) % n_mixers == 0:
            stats, u = _layer_norm(z, g_f, b_f, seq, ("mod", nsc, nsh))
        else:
            out = _layer_norm(z, g_f, b_f, seq, ("shift", nsc, nsh, rw_mu[(layer + 1) // n_mixers]))
            stats, u = out[0], out[1:]
        res = (z, stats, g_f, b_f)
```

```python
import functools

import jax
import jax.numpy as jnp
from jax import lax
from jax.experimental import pallas as pl
from jax.experimental.pallas import tpu as pltpu

F32 = jnp.float32
BF16 = jnp.bfloat16

ML_HEADS = 8
ML_CHUNK = 128
ML_GATE_CAP = 15.0
ML_NORM_EPS = 1e-6
RW_HEAD = 64
RW_CHUNK = 64
RW_GN_EPS = 64e-5
LN_EPS = 1e-5
LANES = 128
SUBLANES = 8
VMEM_LIMIT_BYTES = 56 * 1024 * 1024
TILE_VMEM_BUDGET_BYTES = 52 * 1024 * 1024
SWIGLU_VMEM_LIMIT_BYTES = 60 * 1024 * 1024


def _params(*semantics, vmem=VMEM_LIMIT_BYTES):
    return pltpu.CompilerParams(dimension_semantics=semantics, vmem_limit_bytes=vmem)


def _pick(n, candidates):
    for c in candidates:
        if n % c == 0:
            return c
    return n


def _dot(a, b):
    return jnp.dot(a, b, preferred_element_type=F32)


def _dot_nt(a, b):
    return lax.dot_general(a, b, (((1,), (1,)), ((), ())), preferred_element_type=F32)


def _dot_tn(a, b):
    return lax.dot_general(a, b, (((0,), (0,)), ((), ())), preferred_element_type=F32)


def _split3(x):
    hi = x.astype(BF16)
    r = x - hi.astype(F32)
    mid = r.astype(BF16)
    lo = (r - mid.astype(F32)).astype(BF16)
    return hi, mid, lo


def _dot_01(m, x):
    hi, mid, lo = _split3(x)
    return _dot(m, hi) + (_dot(m, mid) + _dot(m, lo))


def _sigmoid(x):
    return 1.0 / (1.0 + jnp.exp(-x))


def _softplus(x):
    return jnp.maximum(x, 0.0) + jnp.log1p(jnp.exp(-jnp.abs(x)))


def _ada_kernel(c_ref, w_ref, b_ref, o_ref):
    c = c_ref[...]
    c_act = (c * _sigmoid(c)).astype(BF16)
    o_ref[0] = _dot(c_act, w_ref[0].astype(BF16)) + b_ref[0]


def _ada(c, ada_w, ada_b):
    depth, d, n = ada_w.shape
    b = c.shape[0]
    tn = _pick(n, (512, 256, 128))
    return pl.pallas_call(
        _ada_kernel,
        out_shape=jax.ShapeDtypeStruct((depth, b, n), F32),
        grid=(depth, n // tn),
        in_specs=[pl.BlockSpec((b, d), lambda l, j: (0, 0)),
                  pl.BlockSpec((1, d, tn), lambda l, j: (l, 0, j)),
                  pl.BlockSpec((1, 1, tn), lambda l, j: (l, 0, j))],
        out_specs=pl.BlockSpec((1, b, tn), lambda l, j: (l, 0, j)),
        compiler_params=_params("parallel", "parallel"),
        name="ada_mod",
    )(c, ada_w, ada_b.reshape(depth, 1, n))


def _modulate_kernel(x_ref, sc_ref, sh_ref, o_ref):
    o_ref[...] = (x_ref[...] * (1.0 + sc_ref[0]) + sh_ref[0]).astype(o_ref.dtype)


def _modulate(x2, sc, sh, seq):
    t, d = x2.shape
    ts = _pick(seq, (512, 256, 128))
    vec = pl.BlockSpec((1, 1, d), lambda i: ((i * ts) // seq, 0, 0))
    return pl.pallas_call(
        _modulate_kernel,
        out_shape=jax.ShapeDtypeStruct((t, d), BF16),
        grid=(t // ts,),
        in_specs=[pl.BlockSpec((ts, d), lambda i: (i, 0)), vec, vec],
        out_specs=pl.BlockSpec((ts, d), lambda i: (i, 0)),
        compiler_params=_params("parallel"),
        name="modulate",
    )(x2, sc, sh)


def _mm_kernel(a_ref, w_ref, b_ref, o_ref, *, act, w_is_nk):
    dot = _dot_nt if w_is_nk else _dot
    y = dot(a_ref[...], w_ref[...].astype(BF16)) + b_ref[...]
    o_ref[...] = act(y).astype(o_ref.dtype)


def _mm(a, w, layer, n, bias, act, out_dtype, name, w_is_nk=False):
    t, k = a.shape
    tm = _pick(t, (2048, 1024, 512, 256, 128))
    out_bytes = jnp.dtype(out_dtype).itemsize
    fits = lambda c: 2 * (tm * k * 2 + k * c * 4 + tm * c * out_bytes) <= TILE_VMEM_BUDGET_BYTES
    tn = _pick(n, [c for c in (512, 256, 128) if fits(c)])
    if w_is_nk:
        w_spec = pl.BlockSpec((None, tn, k), lambda i, j: (layer, j, 0))
    else:
        w_spec = pl.BlockSpec((None, k, tn), lambda i, j: (layer, 0, j))
    return pl.pallas_call(
        functools.partial(_mm_kernel, act=act, w_is_nk=w_is_nk),
        out_shape=jax.ShapeDtypeStruct((t, n), out_dtype),
        grid=(t // tm, n // tn),
        in_specs=[pl.BlockSpec((tm, k), lambda i, j: (i, 0)),
                  w_spec,
                  pl.BlockSpec((1, tn), lambda i, j: (0, j))],
        out_specs=pl.BlockSpec((tm, tn), lambda i, j: (i, j)),
        compiler_params=_params("parallel", "parallel"),
        name=name,
    )(a, w, bias)


def _identity(y):
    return y


def _side_cast(w, layer, steps, step_of):
    rows = w.shape[1] // steps
    assert rows * steps == w.shape[1] and rows % (2 * SUBLANES) == 0, (w.shape, steps)
    in_spec = pl.BlockSpec((None, rows, w.shape[2]), lambda *g: (layer, step_of(*g), 0))
    out_spec = pl.BlockSpec((rows, w.shape[2]), lambda *g: (step_of(*g), 0))
    return in_spec, out_spec, jax.ShapeDtypeStruct(w.shape[1:], BF16)


def _swiglu_kernel(a_ref, wg_ref, wu_ref, wo_ref, o_ref, wo16_ref):
    a = a_ref[...]
    gate = _dot(a, wg_ref[...].astype(BF16))
    up = _dot(a, wu_ref[...].astype(BF16))
    o_ref[...] = (gate * _sigmoid(gate) * up).astype(o_ref.dtype)
    wo16_ref[...] = wo_ref[...].astype(BF16)


def _mm_swiglu(a, w_in, w_out, layer, hidden):
    t, k = a.shape
    tm = _pick(t, (2048, 1024, 512, 256, 128))
    tn = _pick(hidden, (256, 128))
    nj = hidden // tn
    side_in, side_out, side_shape = _side_cast(w_out, layer, (t // tm) * nj, lambda i, j: i * nj + j)
    return pl.pallas_call(
        _swiglu_kernel,
        out_shape=(jax.ShapeDtypeStruct((t, hidden), BF16), side_shape),
        grid=(t // tm, nj),
        in_specs=[pl.BlockSpec((tm, k), lambda i, j: (i, 0)),
                  pl.BlockSpec((None, k, tn), lambda i, j: (layer, 0, j)),
                  pl.BlockSpec((None, k, tn), lambda i, j: (layer, 0, j + nj)),
                  side_in],
        out_specs=(pl.BlockSpec((tm, tn), lambda i, j: (i, j)), side_out),
        compiler_params=_params("parallel", "parallel", vmem=SWIGLU_VMEM_LIMIT_BYTES),
        name="ffn_in_swiglu",
    )(a, w_in, w_in, w_out)


def _ln_apply(z, mu, rstd, g, b):
    return (z - mu) * rstd * g + b


def _resid_kernel(a_ref, w_ref, x_ref, gt_ref, *rest, alpha, normed):
    if normed:
        st_ref, g_ref, b_ref, o_ref = rest
        st = st_ref[...]
        x = _ln_apply(x_ref[...], st[:, 0:1], st[:, 1:2], g_ref[...], b_ref[...])
    else:
        o_ref, = rest
        x = x_ref[...]
    o_ref[...] = alpha * x + gt_ref[0] * _dot(a_ref[...], w_ref[...])


def _mm_resid(a, w, layer, res, gt, alpha, seq, name):
    t, k = a.shape
    n = w.shape[2]
    normed = len(res) > 1
    if k <= 4096:
        tm = _pick(min(t, seq), (1024, 512, 256, 128))
        tn = _pick(n, (512, 256, 128) if normed else (1024, 512, 256, 128))
    else:
        tm, tn = _pick(min(t, seq), (512, 256, 128)), _pick(n, (512, 256, 128))
    in_specs = [pl.BlockSpec((tm, k), lambda i, j: (i, 0)),
                pl.BlockSpec((None, k, tn), lambda i, j: (layer, 0, j)),
                pl.BlockSpec((tm, tn), lambda i, j: (i, j)),
                pl.BlockSpec((1, 1, tn), lambda i, j: ((i * tm) // seq, 0, j))]
    operands = [a, w, res[0], gt]
    if normed:
        in_specs += [pl.BlockSpec((tm, 2), lambda i, j: (i, 0)),
                     pl.BlockSpec((1, tn), lambda i, j: (0, j)),
                     pl.BlockSpec((1, tn), lambda i, j: (0, j))]
        operands += list(res[1:])
    return pl.pallas_call(
        functools.partial(_resid_kernel, alpha=alpha, normed=normed),
        out_shape=jax.ShapeDtypeStruct((t, n), F32),
        grid=(t // tm, n // tn),
        in_specs=in_specs,
        out_specs=pl.BlockSpec((tm, tn), lambda i, j: (i, j)),
        compiler_params=_params("parallel", "parallel"),
        name=name,
    )(*operands)


def _col_chunks(d):
    cw = _pick(d, (512, 256, 128))
    return [slice(c, c + cw) for c in range(0, d, cw)]


def _ln_stats_ref(z_ref):
    d = z_ref.shape[1]
    chunks = _col_chunks(d)
    mu = jnp.sum(sum(z_ref[:, c] for c in chunks), axis=-1, keepdims=True) * (1.0 / d)
    sq = jnp.sum(sum(jnp.square(z_ref[:, c] - mu) for c in chunks), axis=-1, keepdims=True)
    return mu, lax.rsqrt(sq * (1.0 / d) + LN_EPS)


def _ln_stats(z):
    mu = jnp.mean(z, axis=-1, keepdims=True)
    zc = z - mu
    var = jnp.mean(zc * zc, axis=-1, keepdims=True)
    return mu, lax.rsqrt(var + LN_EPS)


def _ln_kernel(z_ref, g_ref, b_ref, x_ref):
    z = z_ref[...]
    mu, rstd = _ln_stats(z)
    x_ref[...] = _ln_apply(z, mu, rstd, g_ref[...], b_ref[...])


def _ln_mod_kernel(z_ref, g_ref, b_ref, sc_ref, sh_ref, st_ref, u_ref):
    z = z_ref[...]
    mu, rstd = _ln_stats(z)
    st_ref[:, 0:1] = mu
    st_ref[:, 1:2] = rstd
    x = _ln_apply(z, mu, rstd, g_ref[...], b_ref[...])
    u_ref[...] = (x * (1.0 + sc_ref[0]) + sh_ref[0]).astype(u_ref.dtype)


def _ln_shift_kernel(z_ref, zp_ref, g_ref, b_ref, sc_ref, sh_ref, mu_ref, st_ref, *mix_refs, ts, seq):
    mu, rstd = _ln_stats_ref(z_ref)
    st_ref[:, 0:1] = mu
    st_ref[:, 1:2] = rstd
    mu_p, rstd_p = _ln_stats_ref(zp_ref)
    first = (pl.program_id(0) * ts) % seq == 0
    row = lax.broadcasted_iota(jnp.int32, (ts, 1), 0)
    for c in _col_chunks(z_ref.shape[1]):
        g, b = g_ref[:, c], b_ref[:, c]
        scale, shift = 1.0 + sc_ref[0, :, c], sh_ref[0, :, c]
        u = _ln_apply(z_ref[:, c], mu, rstd, g, b) * scale + shift
        prev = _ln_apply(zp_ref[:, c], mu_p, rstd_p, g, b)[SUBLANES - 1:SUBLANES, :] * scale + shift
        prev = jnp.where(first, 0.0, prev)
        u_prev = jnp.where(row == 0, prev, pltpu.roll(u, shift=1, axis=0))
        xx = (u_prev - u).astype(BF16)
        u = u.astype(BF16)
        for j, ref in enumerate(mix_refs):
            ref[:, c] = u + xx * mu_ref[j:j + 1, c].astype(BF16)


def _layer_norm(z, g, b, seq, nxt=None):
    t, d = z.shape
    row = pl.BlockSpec((1, d), lambda i: (0, 0))
    stats = jax.ShapeDtypeStruct((t, 2), F32)
    if nxt is None:
        ts = _pick(seq, (256, 128))
        tile = pl.BlockSpec((ts, d), lambda i: (i, 0))
        return pl.pallas_call(
            _ln_kernel, out_shape=jax.ShapeDtypeStruct((t, d), F32), grid=(t // ts,),
            in_specs=[tile, row, row], out_specs=tile,
            compiler_params=_params("parallel"), name="layer_norm",
        )(z, g, b)
    if nxt[0] == "mod":
        ts = _pick(seq, (256, 128))
        tile = pl.BlockSpec((ts, d), lambda i: (i, 0))
        vec = pl.BlockSpec((1, 1, d), lambda i: ((i * ts) // seq, 0, 0))
        return pl.pallas_call(
            _ln_mod_kernel,
            out_shape=(stats, jax.ShapeDtypeStruct((t, d), BF16)),
            grid=(t // ts,),
            in_specs=[tile, row, row, vec, vec],
            out_specs=(pl.BlockSpec((ts, 2), lambda i: (i, 0)), tile),
            compiler_params=_params("parallel"), name="layer_norm_mod",
        )(z, g, b, nxt[1], nxt[2])
    _, sc, sh, mu = nxt
    nmix = mu.shape[0]
    ts = _pick(seq, (128,))
    tile = pl.BlockSpec((ts, d), lambda i: (i, 0))
    prev = pl.BlockSpec((SUBLANES, d), lambda i: (jnp.maximum(i * (ts // SUBLANES) - 1, 0), 0))
    vec = pl.BlockSpec((1, 1, d), lambda i: ((i * ts) // seq, 0, 0))
    return pl.pallas_call(
        functools.partial(_ln_shift_kernel, ts=ts, seq=seq),
        out_shape=(stats,) + (jax.ShapeDtypeStruct((t, d), BF16),) * nmix,
        grid=(t // ts,),
        in_specs=[tile, prev, row, row, vec, vec, pl.BlockSpec((nmix, d), lambda i: (0, 0))],
        out_specs=(pl.BlockSpec((ts, 2), lambda i: (i, 0)),) + (tile,) * nmix,
        compiler_params=_params("parallel"), name="layer_norm_shift",
    )(z, z, g, b, sc, sh, mu)


def _ml_gate_act(y):
    lane = lax.broadcasted_iota(jnp.int32, y.shape, 1)
    i_pre = ML_GATE_CAP * jnp.tanh(y / ML_GATE_CAP)
    log_f = -_softplus(-y)
    return jnp.where(lane < ML_HEADS, i_pre, log_f)


def _mlstm_kernel(q_ref, k_ref, v_ref, o_ref, icol_ref, fcol_ref, irow_ref, frow_ref, ng_ref, wside_ref,
                  out_ref, wside16_ref, c_sc, n_sc, m_sc, *, heads):
    L = ML_CHUNK
    wside16_ref[...] = wside_ref[...].astype(BF16)

    @pl.when(pl.program_id(2) == 0)
    def _():
        c_sc[...] = jnp.zeros_like(c_sc)
        n_sc[...] = jnp.zeros_like(n_sc)
        m_sc[...] = jnp.zeros_like(m_sc)

    dqk = q_ref.shape[1] // heads
    dv = v_ref.shape[1] // heads
    k_scale = dqk ** -0.5
    r_idx = lax.broadcasted_iota(jnp.int32, (L, L), 0)
    c_idx = lax.broadcasted_iota(jnp.int32, (L, L), 1)
    causal = r_idx >= c_idx
    anti = r_idx <= c_idx

    hs = range(heads)
    q = [q_ref[:, h * dqk:(h + 1) * dqk] for h in hs]
    k = [k_ref[:, h * dqk:(h + 1) * dqk] for h in hs]
    v = [v_ref[:, h * dv:(h + 1) * dv] for h in hs]
    i_col, f_col = [icol_ref[h] for h in hs], [fcol_ref[h] for h in hs]
    i_row, f_row = [irow_ref[h] for h in hs], [frow_ref[h] for h in hs]
    c_st, n_st, m_st = [c_sc[h] for h in hs], [n_sc[h] for h in hs], [m_sc[h] for h in hs]

    qk = [_dot_nt(q[h], k[h]) for h in hs]
    q_c = [_dot(q[h], c_st[h].astype(BF16)) for h in hs]
    g_col = [jnp.sum(jnp.where(causal, f_row[h], 0.0), axis=1, keepdims=True) for h in hs]
    g_row = [jnp.sum(jnp.where(anti, f_col[h], 0.0), axis=0, keepdims=True) for h in hs]
    g_last = [jnp.sum(f_row[h], axis=1, keepdims=True) for h in hs]
    log_d = [jnp.where(causal, g_col[h] - g_row[h] + i_row[h], -jnp.inf) for h in hs]
    log_inter = [g_col[h] + m_st[h] for h in hs]
    m_row = [jnp.maximum(jnp.max(log_d[h], axis=1, keepdims=True), log_inter[h]) for h in hs]
    scores = [qk[h] * k_scale * jnp.exp(log_d[h] - m_row[h]) for h in hs]
    inter = [jnp.exp(log_inter[h] - m_row[h]) for h in hs]
    num = [_dot(scores[h].astype(BF16), v[h]) + inter[h] * q_c[h] for h in hs]
    q_n = [jnp.sum(q[h].astype(F32) * n_st[h], axis=1, keepdims=True) for h in hs]
    den = [jnp.sum(scores[h], axis=1, keepdims=True) + inter[h] * q_n[h] for h in hs]
    for h in hs:
        hid = num[h] / jnp.maximum(jnp.abs(den[h]), jnp.exp(-m_row[h]))
        hid = hid * lax.rsqrt(jnp.mean(hid * hid, axis=1, keepdims=True) + ML_NORM_EPS)
        sl = slice(h * dv, (h + 1) * dv)
        out_ref[:, sl] = (hid * ng_ref[:, sl] * _sigmoid(o_ref[:, sl].astype(F32))).astype(out_ref.dtype)

    log_w = [g_last[h] - g_col[h] + i_col[h] for h in hs]
    m_new = [jnp.maximum(g_last[h] + m_st[h], jnp.max(log_w[h], axis=0, keepdims=True)) for h in hs]
    wk = [k[h].astype(F32) * (k_scale * jnp.exp(log_w[h] - m_new[h])) for h in hs]
    kv = [_dot_tn(wk[h].astype(BF16), v[h]) for h in hs]
    for h in hs:
        decay = jnp.exp(g_last[h] + m_st[h] - m_new[h])
        c_sc[h] = decay * c_st[h] + kv[h]
        n_sc[h] = decay * n_st[h] + jnp.sum(wk[h], axis=0, keepdims=True)
        m_sc[h] = m_new[h]


def _mlstm(proj, gates, norm_g, w_side, layer, batch, seq):
    t = proj.shape[0]
    H, L = ML_HEADS, ML_CHUNK
    G = 4
    dv = norm_g.shape[1] // H
    dqk = dv // 2
    nc = seq // L
    ng = H // G
    g = gates[:, :2 * H].reshape(batch, seq, 2 * H).transpose(0, 2, 1)
    i_g, f_g = g[:, :H], g[:, H:]
    col = lambda a: a.reshape(batch, H, seq, 1)
    rowv = lambda a: a.reshape(batch, H, nc, 1, L)
    col_spec = pl.BlockSpec((None, G, L, 1), lambda b, h, c: (b, h, c, 0))
    row_spec = pl.BlockSpec((None, G, None, 1, L), lambda b, h, c: (b, h, c, 0, 0))
    tok = lambda b, c: b * nc + c
    side_in, side_out, side_shape = _side_cast(w_side, layer, batch * ng * nc, lambda b, h, c: (b * ng + h) * nc + c)
    return pl.pallas_call(
        functools.partial(_mlstm_kernel, heads=G),
        out_shape=(jax.ShapeDtypeStruct((t, H * dv), BF16), side_shape),
        grid=(batch, ng, nc),
        in_specs=[pl.BlockSpec((L, G * dqk), lambda b, h, c: (tok(b, c), h)),
                  pl.BlockSpec((L, G * dqk), lambda b, h, c: (tok(b, c), ng + h)),
                  pl.BlockSpec((L, G * dv), lambda b, h, c: (tok(b, c), ng + h)),
                  pl.BlockSpec((L, G * dv), lambda b, h, c: (tok(b, c), 2 * ng + h)),
                  col_spec, col_spec, row_spec, row_spec,
                  pl.BlockSpec((1, G * dv), lambda b, h, c: (0, h)),
                  side_in],
        out_specs=(pl.BlockSpec((L, G * dv), lambda b, h, c: (tok(b, c), h)), side_out),
        scratch_shapes=[pltpu.VMEM((G, dqk, dv), F32), pltpu.VMEM((G, 1, dqk), F32), pltpu.VMEM((G, 1, 1), F32)],
        compiler_params=_params("parallel", "parallel", "arbitrary"),
        name="mlstm_chunk",
    )(proj, proj, proj, proj, col(i_g), col(f_g), rowv(i_g), rowv(f_g), norm_g, w_side)


def _rwkv_kernel(r_ref, k_ref, v_ref, lw_ref, la_ref, lgate_ref, w2_ref, a2_ref, g2_ref, w0_ref, a0_ref,
                 kk_ref, ka_ref, rk_ref, lg_ref, lb_ref, wside_ref, o_ref, wside16_ref, s_sc, *, heads):
    L, N = RW_CHUNK, RW_HEAD
    wside16_ref[...] = wside_ref[...].astype(BF16)
    gw = heads * N

    @pl.when(pl.program_id(2) == 0)
    def _():
        s_sc[...] = jnp.zeros_like(s_sc)

    P = 2 * L
    pairs = heads // 2

    r, k, v = r_ref[...].astype(F32), k_ref[...].astype(F32), v_ref[...].astype(F32)
    w_pre = w0_ref[...] + _dot(lw_ref[...], w2_ref[...])
    a_pre = a0_ref[...] + _dot(la_ref[...], a2_ref[...])
    gate = _dot(lgate_ref[...], g2_ref[...])
    log_decay = -jnp.exp(-_softplus(-w_pre) - 0.5)
    a = _sigmoid(a_pre)

    sw = min(gw, 256)
    li = lax.broadcasted_iota(jnp.int32, (sw, sw), 0) // N
    lj = lax.broadcasted_iota(jnp.int32, (sw, sw), 1) // N
    head_ones = jnp.where(li == lj, 1.0, 0.0).astype(BF16)

    def head_sum(x, split):
        parts = split(x)
        cols = []
        for c0 in range(0, gw, sw):
            acc = None
            for p in parts:
                d = _dot(p[:, c0:c0 + sw], head_ones)
                acc = d if acc is None else acc + d
            cols.append(acc)
        return cols[0] if len(cols) == 1 else jnp.concatenate(cols, axis=1)

    one_pass = lambda x: (x.astype(BF16),)

    t_i = lax.broadcasted_iota(jnp.int32, (L, L), 0)
    t_j = lax.broadcasted_iota(jnp.int32, (L, L), 1)
    tri = jnp.where(t_i >= t_j, 1.0, 0.0).astype(BF16)

    kkr = k * kk_ref[...]
    k = k * (1.0 + (a - 1.0) * ka_ref[...])
    sums = head_sum(jnp.concatenate([kkr * kkr, r * k * rk_ref[...]], axis=0), one_pass)
    kk = kkr * lax.rsqrt(jnp.maximum(sums[:L], 1e-24))
    bonus = sums[L:] * v

    cl = _dot_01(tri, log_decay)
    cl_last = cl[L - 1:L, :]
    gam = jnp.exp(cl)
    inv_gam = jnp.exp(-cl)
    gam_end = jnp.exp(cl_last)
    to_end = gam_end * inv_gam
    kka = kk * a
    r_hat = r * gam
    a_hat = -kk * jnp.exp(cl - log_decay)
    b_hat = kka * inv_gam
    k_hat = k * inv_gam
    b_end = kka * to_end
    k_end = k * to_end

    p_i = lax.broadcasted_iota(jnp.int32, (P, P), 0)
    p_j = lax.broadcasted_iota(jnp.int32, (P, P), 1)
    same = p_i // L == p_j // L
    strict = jnp.logical_and(same, p_i > p_j)
    incl = jnp.logical_and(same, p_i >= p_j)
    eye = jnp.where(p_i == p_j, 1.0, 0.0)
    levels = []
    s = 2
    while s < L:
        levels.append((s, jnp.logical_and(p_i // (2 * s) == p_j // (2 * s),
                                          jnp.logical_and(p_i % (2 * s) >= s, p_j % (2 * s) < s))))
        s *= 2
    own = (lax.broadcasted_iota(jnp.int32, (P, 2 * N), 0) // L
           == lax.broadcasted_iota(jnp.int32, (P, 2 * N), 1) // N)

    def stack(x):
        return jnp.where(own, jnp.concatenate([x, x], axis=0), 0.0).astype(BF16)

    rng = range(pairs)
    sls = [slice(p * 2 * N, (p + 1) * 2 * N) for p in rng]
    st = [s_sc[p] for p in rng]
    ar_s = [jnp.concatenate([stack(a_hat[:, sl]), stack(r_hat[:, sl])], axis=0) for sl in sls]
    v_s = [stack(v[:, sl]) for sl in sls]
    m4 = [_dot_nt(ar_s[p], jnp.concatenate([stack(b_hat[:, sls[p]]), stack(k_hat[:, sls[p]])], axis=0))
          for p in rng]
    a_ab = [jnp.where(strict, m[:P, :P], 0.0) for m in m4]
    a_ak = [jnp.where(strict, m[:P, P:], 0.0).astype(BF16) for m in m4]
    a_r = [jnp.concatenate([jnp.where(incl, m[P:, :P], 0.0), jnp.where(incl, m[P:, P:], 0.0)], axis=1).astype(BF16)
           for m in m4]
    from_state = [_dot_nt(ar_s[p], st[p].astype(BF16)) for p in rng]
    rhs_u = [from_state[p][:P] + _dot(a_ak[p], v_s[p]) for p in rng]
    x = [eye + jnp.where(p_i // 2 == p_j // 2, m, 0.0) for m in a_ab]
    for s, blk in levels:
        xb = [xp.astype(BF16) for xp in x]
        m_s = [jnp.where(blk, a_ab[p], 0.0).astype(BF16) for p in rng]
        if s % SUBLANES:
            half = [_dot(xb[p], m_s[p]).astype(BF16) for p in rng]
            x = [x[p] + _dot(half[p], xb[p]) for p in rng]
        else:
            lower = [jnp.concatenate([xp[b + s:b + 2 * s] for b in range(0, P, 2 * s)], axis=0) for xp in x]
            half = [_dot(lower[p].astype(BF16), m_s[p]).astype(BF16) for p in rng]
            lower = [lower[p] + _dot(half[p], xb[p]) for p in rng]
            x = [jnp.concatenate(
                [blk_rows for i, b in enumerate(range(0, P, 2 * s))
                 for blk_rows in (x[p][b:b + s], lower[p][i * s:(i + 1) * s])], axis=0) for p in rng]
    su = [_dot(x[p].astype(BF16), rhs_u[p].astype(BF16)) for p in rng]
    su_v = [jnp.concatenate([su[p].astype(BF16), v_s[p]], axis=0) for p in rng]
    y_s = [from_state[p][P:] + _dot(a_r[p], su_v[p]) for p in rng]
    for p in rng:
        be = jnp.concatenate([stack(b_end[:, sls[p]]), stack(k_end[:, sls[p]])], axis=0)
        s_sc[p] = st[p] * gam_end[:, sls[p]] + _dot_tn(su_v[p], be)
    ys = [yp[:L] + yp[L:] for yp in y_s]
    y = ys[0] if pairs == 1 else jnp.concatenate(ys, axis=1)

    inv_n = 1.0 / N
    mean = head_sum(y, one_pass) * inv_n
    yc = y - mean
    var = head_sum(yc * yc, one_pass) * inv_n
    yn = yc * lax.rsqrt(var + RW_GN_EPS) * lg_ref[...] + lb_ref[...]
    o_ref[...] = ((yn + bonus) * gate).astype(o_ref.dtype)


def _rwkv(r, k, v, lw, la, lgate, w2, a2, g2, layer, w0, a0, k_k, k_a, r_k, lnx_g, lnx_b, w_side, batch, seq):
    t, d = r.shape
    L = RW_CHUNK
    heads = min(64, d // RW_HEAD)
    gw = heads * RW_HEAD
    nc = seq // L
    nh = d // gw
    tile = pl.BlockSpec((L, gw), lambda b, h, c: (b * nc + c, h))
    row = pl.BlockSpec((1, gw), lambda b, h, c: (0, h))
    lora = lambda f: pl.BlockSpec((L, f.shape[1]), lambda b, h, c: (b * nc + c, 0))
    lora_w = lambda w: pl.BlockSpec((None, w.shape[1], gw), lambda b, h, c: (layer, 0, h))
    side_in, side_out, side_shape = _side_cast(w_side, layer, batch * nh * nc, lambda b, h, c: (b * nh + h) * nc + c)
    return pl.pallas_call(
        functools.partial(_rwkv_kernel, heads=heads),
        out_shape=(jax.ShapeDtypeStruct((t, d), BF16), side_shape),
        grid=(batch, nh, nc),
        in_specs=([tile] * 3 + [lora(lw), lora(la), lora(lgate), lora_w(w2), lora_w(a2), lora_w(g2)] + [row] * 7
                  + [side_in]),
        out_specs=(tile, side_out),
        scratch_shapes=[pltpu.VMEM((heads // 2, 2 * RW_HEAD, 2 * RW_HEAD), F32)],
        compiler_params=_params("parallel", "parallel", "arbitrary"),
        name="rwkv7_chunk",
    )(r, k, v, lw, la, lgate, w2, a2, g2, w0, a0, k_k, k_a, r_k, lnx_g, lnx_b, w_side)


def kernel(x, c, ada_w, ada_b, mix_ln_g, mix_ln_b, ffn_ln_g, ffn_ln_b, ffn_w_in, ffn_w_out, ml_w_in, ml_b_i, ml_b_f, ml_norm_g, ml_w_out, rw_mu, rw_w_r, rw_w_k, rw_w_v, rw_w0, rw_w1, rw_w2, rw_a0, rw_a1, rw_a2, rw_g1, rw_g2, rw_k_k, rw_k_a, rw_r_k, rw_lnx_g, rw_lnx_b, rw_w_o):
    batch, seq, d = x.shape
    depth = ada_w.shape[0]
    t = batch * seq
    alpha = (2 * depth) ** 0.25
    hidden = ffn_w_out.shape[1]
    n_mixers = 2

    mod = _ada(c, ada_w, ada_b).reshape(depth, batch, 6, 1, d)
    row = lambda p: p.reshape(1, -1)
    zero_bias = lambda n: jnp.zeros((1, n), F32)

    res = (x.reshape(t, d),)
    u = None
    for layer in range(depth):
        sh_m, sc_m, gt_m, sh_f, sc_f, gt_f = [mod[layer, :, i] for i in range(6)]
        j = layer // n_mixers
        g_m, b_m = row(mix_ln_g[layer]), row(mix_ln_b[layer])
        if layer % n_mixers == 0:
            if u is None:
                u = _modulate(res[0], sc_m, sh_m, seq)
            qkvo = ml_w_in.shape[2] - 2 * ML_HEADS
            proj = _mm(u, jnp.swapaxes(ml_w_in, 1, 2), j, qkvo, zero_bias(qkvo), _identity, BF16, "mlstm_in",
                       w_is_nk=True)
            w_gate = jnp.pad(ml_w_in[j:j + 1, :, qkvo:], ((0, 0), (0, 0), (0, LANES - 2 * ML_HEADS)))
            b_gate = jnp.pad(jnp.concatenate([ml_b_i[j], ml_b_f[j]]), (0, LANES - 2 * ML_HEADS)).reshape(1, LANES)
            gates = _mm(u, w_gate, 0, LANES, b_gate, _ml_gate_act, F32, "mlstm_gates")
            y, w_o = _mlstm(proj, gates, row(ml_norm_g[j]), ml_w_out, j, batch, seq)
            z = _mm_resid(y, w_o[None], 0, res, gt_m, alpha, seq, "mlstm_out")
        else:
            xr, xw, xk, xv, xa, xg = u
            nb = zero_bias
            r = _mm(xr, rw_w_r, j, d, nb(d), _identity, BF16, "rwkv_r")
            k = _mm(xk, rw_w_k, j, d, nb(d), _identity, BF16, "rwkv_k")
            v = _mm(xv, rw_w_v, j, d, nb(d), _identity, BF16, "rwkv_v")
            n_w, n_a, n_g = rw_w1.shape[2], rw_a1.shape[2], rw_g1.shape[2]
            lw = _mm(xw, rw_w1, j, n_w, nb(n_w), jnp.tanh, BF16, "rwkv_w1")
            la = _mm(xa, rw_a1, j, n_a, nb(n_a), _identity, BF16, "rwkv_a1")
            lg = _mm(xg, rw_g1, j, n_g, nb(n_g), _sigmoid, BF16, "rwkv_g1")
            y, w_o = _rwkv(r, k, v, lw, la, lg, rw_w2.astype(BF16), rw_a2.astype(BF16), rw_g2.astype(BF16), j,
                           row(rw_w0[j]), row(rw_a0[j]), row(rw_k_k[j]), row(rw_k_a[j]), row(rw_r_k[j]),
                           row(rw_lnx_g[j]), row(rw_lnx_b[j]), rw_w_o, batch, seq)
            z = _mm_resid(y, w_o[None], 0, res, gt_m, alpha, seq, "rwkv_out")
        stats, u = _layer_norm(z, g_m, b_m, seq, ("mod", sc_f, sh_f))
        res = (z, stats, g_m, b_m)
        hid, w_down = _mm_swiglu(u, ffn_w_in, ffn_w_out, layer, hidden)
        z = _mm_resid(hid, w_down[None], 0, res, gt_f, alpha, seq, "ffn_out")
        g_f, b_f = row(ffn_ln_g[layer]), row(ffn_ln_b[layer])
        if layer + 1 == depth:
            return _layer_norm(z, g_f, b_f, seq).reshape(batch, seq, d)
        nsh, nsc = mod[layer + 1, :, 0], mod[layer + 1, :, 1]
        if (layer + 1) % n_mixers == 0:
            stats, u = _layer_norm(z, g_f, b_f, seq, ("mod", nsc, nsh))
        else:
            out = _layer_norm(z, g_f, b_f, seq, ("shift", nsc, nsh, rw_mu[(layer + 1) // n_mixers]))
            stats, u = out[0], out[1:]
        res = (z, stats, g_f, b_f)
```

```python
import functools

import jax
import jax.numpy as jnp
from jax import lax
from jax.experimental import pallas as pl
from jax.experimental.pallas import tpu as pltpu

F32 = jnp.float32
BF16 = jnp.bfloat16

ML_HEADS = 8
ML_CHUNK = 128
ML_GATE_CAP = 15.0
ML_NORM_EPS = 1e-6
RW_HEAD = 64
RW_CHUNK = 64
RW_GN_EPS = 64e-5
LN_EPS = 1e-5
LANES = 128
SUBLANES = 8
VMEM_LIMIT_BYTES = 56 * 1024 * 1024
TILE_VMEM_BUDGET_BYTES = 52 * 1024 * 1024
WIDE_VMEM_LIMIT_BYTES = 60 * 1024 * 1024


def _params(*semantics, vmem=VMEM_LIMIT_BYTES):
    return pltpu.CompilerParams(dimension_semantics=semantics, vmem_limit_bytes=vmem)


def _pick(n, candidates):
    for c in candidates:
        if n % c == 0:
            return c
    return n


def _dot(a, b):
    return jnp.dot(a, b, preferred_element_type=F32)


def _dot_nt(a, b):
    return lax.dot_general(a, b, (((1,), (1,)), ((), ())), preferred_element_type=F32)


def _dot_tn(a, b):
    return lax.dot_general(a, b, (((0,), (0,)), ((), ())), preferred_element_type=F32)


def _split3(x):
    hi = x.astype(BF16)
    r = x - hi.astype(F32)
    mid = r.astype(BF16)
    lo = (r - mid.astype(F32)).astype(BF16)
    return hi, mid, lo


def _dot_01(m, x):
    hi, mid, lo = _split3(x)
    return _dot(m, hi) + (_dot(m, mid) + _dot(m, lo))


def _sigmoid(x):
    return 1.0 / (1.0 + jnp.exp(-x))


def _softplus(x):
    return jnp.maximum(x, 0.0) + jnp.log1p(jnp.exp(-jnp.abs(x)))


def _ada_kernel(c_ref, w_ref, b_ref, o_ref):
    c = c_ref[...]
    c_act = (c * _sigmoid(c)).astype(BF16)
    o_ref[0] = _dot(c_act, w_ref[0].astype(BF16)) + b_ref[0]


def _ada(c, ada_w, ada_b):
    depth, d, n = ada_w.shape
    b = c.shape[0]
    tn = _pick(n, (512, 256, 128))
    return pl.pallas_call(
        _ada_kernel,
        out_shape=jax.ShapeDtypeStruct((depth, b, n), F32),
        grid=(depth, n // tn),
        in_specs=[pl.BlockSpec((b, d), lambda l, j: (0, 0)),
                  pl.BlockSpec((1, d, tn), lambda l, j: (l, 0, j)),
                  pl.BlockSpec((1, 1, tn), lambda l, j: (l, 0, j))],
        out_specs=pl.BlockSpec((1, b, tn), lambda l, j: (l, 0, j)),
        compiler_params=_params("parallel", "parallel"),
        name="ada_mod",
    )(c, ada_w, ada_b.reshape(depth, 1, n))


def _modulate_kernel(x_ref, sc_ref, sh_ref, o_ref):
    o_ref[...] = (x_ref[...] * (1.0 + sc_ref[0]) + sh_ref[0]).astype(o_ref.dtype)


def _modulate(x2, sc, sh, seq):
    t, d = x2.shape
    ts = _pick(seq, (512, 256, 128))
    vec = pl.BlockSpec((1, 1, d), lambda i: ((i * ts) // seq, 0, 0))
    return pl.pallas_call(
        _modulate_kernel,
        out_shape=jax.ShapeDtypeStruct((t, d), BF16),
        grid=(t // ts,),
        in_specs=[pl.BlockSpec((ts, d), lambda i: (i, 0)), vec, vec],
        out_specs=pl.BlockSpec((ts, d), lambda i: (i, 0)),
        compiler_params=_params("parallel"),
        name="modulate",
    )(x2, sc, sh)


def _mm_kernel(a_ref, w_ref, b_ref, o_ref, *, act, w_is_nk):
    dot = _dot_nt if w_is_nk else _dot
    y = dot(a_ref[...], w_ref[...].astype(BF16)) + b_ref[...]
    o_ref[...] = act(y).astype(o_ref.dtype)


def _mm(a, w, layer, n, bias, act, out_dtype, name, w_is_nk=False):
    t, k = a.shape
    tm = _pick(t, (2048, 1024, 512, 256, 128))
    out_bytes = jnp.dtype(out_dtype).itemsize
    fits = lambda c: 2 * (tm * k * 2 + k * c * 4 + tm * c * out_bytes) <= TILE_VMEM_BUDGET_BYTES
    tn = _pick(n, [c for c in (512, 256, 128) if fits(c)])
    if w_is_nk:
        w_spec = pl.BlockSpec((None, tn, k), lambda i, j: (layer, j, 0))
    else:
        w_spec = pl.BlockSpec((None, k, tn), lambda i, j: (layer, 0, j))
    return pl.pallas_call(
        functools.partial(_mm_kernel, act=act, w_is_nk=w_is_nk),
        out_shape=jax.ShapeDtypeStruct((t, n), out_dtype),
        grid=(t // tm, n // tn),
        in_specs=[pl.BlockSpec((tm, k), lambda i, j: (i, 0)),
                  w_spec,
                  pl.BlockSpec((1, tn), lambda i, j: (0, j))],
        out_specs=pl.BlockSpec((tm, tn), lambda i, j: (i, j)),
        compiler_params=_params("parallel", "parallel"),
        name=name,
    )(a, w, bias)


def _identity(y):
    return y


def _resid_tn(k, n):
    return _pick(n, (1024, 512, 256, 128)) if k <= 4096 else _pick(n, (512, 256, 128))


def _side_cast(w, layer, steps, step_of):
    k, n = w.shape[1:]
    tn = _resid_tn(k, n)
    rows = k // steps
    assert rows * steps == k and rows % (2 * SUBLANES) == 0, (w.shape, steps)
    in_spec = pl.BlockSpec((None, rows, n), lambda *g: (layer, step_of(*g), 0))
    out_spec = pl.BlockSpec((n // tn, rows, tn), lambda *g: (0, step_of(*g), 0))
    return in_spec, out_spec, jax.ShapeDtypeStruct((n // tn, k, tn), BF16)


def _side_cast_step(src_ref, dst_ref):
    tn = dst_ref.shape[2]
    for jb in range(dst_ref.shape[0]):
        dst_ref[jb] = src_ref[:, jb * tn:(jb + 1) * tn].astype(dst_ref.dtype)


def _swiglu_kernel(a_ref, wg_ref, wu_ref, wo_ref, o_ref, wo16_ref):
    a = a_ref[...]
    gate = _dot(a, wg_ref[...].astype(BF16))
    up = _dot(a, wu_ref[...].astype(BF16))
    o_ref[...] = (gate * _sigmoid(gate) * up).astype(o_ref.dtype)
    _side_cast_step(wo_ref, wo16_ref)


def _mm_swiglu(a, w_in, w_out, layer, hidden):
    t, k = a.shape
    tm = _pick(t, (2048, 1024, 512, 256, 128))
    tn = _pick(hidden, (256, 128))
    nj = hidden // tn
    side_in, side_out, side_shape = _side_cast(w_out, layer, (t // tm) * nj, lambda i, j: i * nj + j)
    return pl.pallas_call(
        _swiglu_kernel,
        out_shape=(jax.ShapeDtypeStruct((t, hidden), BF16), side_shape),
        grid=(t // tm, nj),
        in_specs=[pl.BlockSpec((tm, k), lambda i, j: (i, 0)),
                  pl.BlockSpec((None, k, tn), lambda i, j: (layer, 0, j)),
                  pl.BlockSpec((None, k, tn), lambda i, j: (layer, 0, j + nj)),
                  side_in],
        out_specs=(pl.BlockSpec((tm, tn), lambda i, j: (i, j)), side_out),
        compiler_params=_params("parallel", "parallel", vmem=WIDE_VMEM_LIMIT_BYTES),
        name="ffn_in_swiglu",
    )(a, w_in, w_in, w_out)


def _ln_apply(z, mu, rstd, g, b):
    return (z - mu) * rstd * g + b


def _resid_kernel(a_ref, w_ref, x_ref, gt_ref, *rest, alpha, normed):
    if normed:
        st_ref, g_ref, b_ref, o_ref = rest
        st = st_ref[...]
        x = _ln_apply(x_ref[...], st[:, 0:1], st[:, 1:2], g_ref[...], b_ref[...])
    else:
        o_ref, = rest
        x = x_ref[...]
    o_ref[...] = alpha * x + gt_ref[0] * _dot(a_ref[...], w_ref[...])


def _mm_resid(a, w, res, gt, alpha, seq, name):
    t, k = a.shape
    nblk, _, tn = w.shape
    n = nblk * tn
    normed = len(res) > 1
    tm = _pick(min(t, seq), (1024, 512, 256, 128) if k <= 4096 else (512, 256, 128))
    in_specs = [pl.BlockSpec((tm, k), lambda i, j: (i, 0)),
                pl.BlockSpec((None, k, tn), lambda i, j: (j, 0, 0)),
                pl.BlockSpec((tm, tn), lambda i, j: (i, j)),
                pl.BlockSpec((1, 1, tn), lambda i, j: ((i * tm) // seq, 0, j))]
    operands = [a, w, res[0], gt]
    if normed:
        in_specs += [pl.BlockSpec((tm, 2), lambda i, j: (i, 0)),
                     pl.BlockSpec((1, tn), lambda i, j: (0, j)),
                     pl.BlockSpec((1, tn), lambda i, j: (0, j))]
        operands += list(res[1:])
    return pl.pallas_call(
        functools.partial(_resid_kernel, alpha=alpha, normed=normed),
        out_shape=jax.ShapeDtypeStruct((t, n), F32),
        grid=(t // tm, n // tn),
        in_specs=in_specs,
        out_specs=pl.BlockSpec((tm, tn), lambda i, j: (i, j)),
        compiler_params=_params("parallel", "parallel", vmem=WIDE_VMEM_LIMIT_BYTES),
        name=name,
    )(*operands)


def _col_chunks(d):
    cw = _pick(d, (512, 256, 128))
    return [slice(c, c + cw) for c in range(0, d, cw)]


def _ln_stats_ref(z_ref):
    d = z_ref.shape[1]
    chunks = _col_chunks(d)
    mu = jnp.sum(sum(z_ref[:, c] for c in chunks), axis=-1, keepdims=True) * (1.0 / d)
    sq = jnp.sum(sum(jnp.square(z_ref[:, c] - mu) for c in chunks), axis=-1, keepdims=True)
    return mu, lax.rsqrt(sq * (1.0 / d) + LN_EPS)


def _ln_stats(z):
    mu = jnp.mean(z, axis=-1, keepdims=True)
    zc = z - mu
    var = jnp.mean(zc * zc, axis=-1, keepdims=True)
    return mu, lax.rsqrt(var + LN_EPS)


def _ln_kernel(z_ref, g_ref, b_ref, x_ref):
    z = z_ref[...]
    mu, rstd = _ln_stats(z)
    x_ref[...] = _ln_apply(z, mu, rstd, g_ref[...], b_ref[...])


def _ln_mod_kernel(z_ref, g_ref, b_ref, sc_ref, sh_ref, st_ref, u_ref):
    z = z_ref[...]
    mu, rstd = _ln_stats(z)
    st_ref[:, 0:1] = mu
    st_ref[:, 1:2] = rstd
    x = _ln_apply(z, mu, rstd, g_ref[...], b_ref[...])
    u_ref[...] = (x * (1.0 + sc_ref[0]) + sh_ref[0]).astype(u_ref.dtype)


def _ln_shift_kernel(z_ref, zp_ref, g_ref, b_ref, sc_ref, sh_ref, mu_ref, st_ref, *mix_refs, ts, seq):
    mu, rstd = _ln_stats_ref(z_ref)
    st_ref[:, 0:1] = mu
    st_ref[:, 1:2] = rstd
    mu_p, rstd_p = _ln_stats_ref(zp_ref)
    first = (pl.program_id(0) * ts) % seq == 0
    row = lax.broadcasted_iota(jnp.int32, (ts, 1), 0)
    for c in _col_chunks(z_ref.shape[1]):
        g, b = g_ref[:, c], b_ref[:, c]
        scale, shift = 1.0 + sc_ref[0, :, c], sh_ref[0, :, c]
        u = _ln_apply(z_ref[:, c], mu, rstd, g, b) * scale + shift
        prev = _ln_apply(zp_ref[:, c], mu_p, rstd_p, g, b)[SUBLANES - 1:SUBLANES, :] * scale + shift
        prev = jnp.where(first, 0.0, prev)
        u_prev = jnp.where(row == 0, prev, pltpu.roll(u, shift=1, axis=0))
        xx = (u_prev - u).astype(BF16)
        u = u.astype(BF16)
        for j, ref in enumerate(mix_refs):
            ref[:, c] = u + xx * mu_ref[j:j + 1, c].astype(BF16)


def _layer_norm(z, g, b, seq, nxt=None):
    t, d = z.shape
    row = pl.BlockSpec((1, d), lambda i: (0, 0))
    stats = jax.ShapeDtypeStruct((t, 2), F32)
    if nxt is None:
        ts = _pick(seq, (256, 128))
        tile = pl.BlockSpec((ts, d), lambda i: (i, 0))
        return pl.pallas_call(
            _ln_kernel, out_shape=jax.ShapeDtypeStruct((t, d), F32), grid=(t // ts,),
            in_specs=[tile, row, row], out_specs=tile,
            compiler_params=_params("parallel"), name="layer_norm",
        )(z, g, b)
    if nxt[0] == "mod":
        ts = _pick(seq, (256, 128))
        tile = pl.BlockSpec((ts, d), lambda i: (i, 0))
        vec = pl.BlockSpec((1, 1, d), lambda i: ((i * ts) // seq, 0, 0))
        return pl.pallas_call(
            _ln_mod_kernel,
            out_shape=(stats, jax.ShapeDtypeStruct((t, d), BF16)),
            grid=(t // ts,),
            in_specs=[tile, row, row, vec, vec],
            out_specs=(pl.BlockSpec((ts, 2), lambda i: (i, 0)), tile),
            compiler_params=_params("parallel"), name="layer_norm_mod",
        )(z, g, b, nxt[1], nxt[2])
    _, sc, sh, mu = nxt
    nmix = mu.shape[0]
    ts = _pick(seq, (128,))
    tile = pl.BlockSpec((ts, d), lambda i: (i, 0))
    prev = pl.BlockSpec((SUBLANES, d), lambda i: (jnp.maximum(i * (ts // SUBLANES) - 1, 0), 0))
    vec = pl.BlockSpec((1, 1, d), lambda i: ((i * ts) // seq, 0, 0))
    return pl.pallas_call(
        functools.partial(_ln_shift_kernel, ts=ts, seq=seq),
        out_shape=(stats,) + (jax.ShapeDtypeStruct((t, d), BF16),) * nmix,
        grid=(t // ts,),
        in_specs=[tile, prev, row, row, vec, vec, pl.BlockSpec((nmix, d), lambda i: (0, 0))],
        out_specs=(pl.BlockSpec((ts, 2), lambda i: (i, 0)),) + (tile,) * nmix,
        compiler_params=_params("parallel"), name="layer_norm_shift",
    )(z, z, g, b, sc, sh, mu)


def _ml_gate_act(y):
    lane = lax.broadcasted_iota(jnp.int32, y.shape, 1)
    i_pre = ML_GATE_CAP * jnp.tanh(y / ML_GATE_CAP)
    log_f = -_softplus(-y)
    return jnp.where(lane < ML_HEADS, i_pre, log_f)


def _mlstm_kernel(q_ref, k_ref, v_ref, o_ref, icol_ref, fcol_ref, irow_ref, frow_ref, ng_ref, wside_ref,
                  out_ref, wside16_ref, c_sc, n_sc, m_sc, *, heads):
    L = ML_CHUNK
    _side_cast_step(wside_ref, wside16_ref)

    @pl.when(pl.program_id(2) == 0)
    def _():
        c_sc[...] = jnp.zeros_like(c_sc)
        n_sc[...] = jnp.zeros_like(n_sc)
        m_sc[...] = jnp.zeros_like(m_sc)

    dqk = q_ref.shape[1] // heads
    dv = v_ref.shape[1] // heads
    k_scale = dqk ** -0.5
    r_idx = lax.broadcasted_iota(jnp.int32, (L, L), 0)
    c_idx = lax.broadcasted_iota(jnp.int32, (L, L), 1)
    causal = r_idx >= c_idx
    anti = r_idx <= c_idx

    hs = range(heads)
    q = [q_ref[:, h * dqk:(h + 1) * dqk] for h in hs]
    k = [k_ref[:, h * dqk:(h + 1) * dqk] for h in hs]
    v = [v_ref[:, h * dv:(h + 1) * dv] for h in hs]
    i_col, f_col = [icol_ref[h] for h in hs], [fcol_ref[h] for h in hs]
    i_row, f_row = [irow_ref[h] for h in hs], [frow_ref[h] for h in hs]
    c_st, n_st, m_st = [c_sc[h] for h in hs], [n_sc[h] for h in hs], [m_sc[h] for h in hs]

    qk = [_dot_nt(q[h], k[h]) for h in hs]
    q_c = [_dot(q[h], c_st[h].astype(BF16)) for h in hs]
    g_col = [jnp.sum(jnp.where(causal, f_row[h], 0.0), axis=1, keepdims=True) for h in hs]
    g_row = [jnp.sum(jnp.where(anti, f_col[h], 0.0), axis=0, keepdims=True) for h in hs]
    g_last = [jnp.sum(f_row[h], axis=1, keepdims=True) for h in hs]
    log_d = [jnp.where(causal, g_col[h] - g_row[h] + i_row[h], -jnp.inf) for h in hs]
    log_inter = [g_col[h] + m_st[h] for h in hs]
    m_row = [jnp.maximum(jnp.max(log_d[h], axis=1, keepdims=True), log_inter[h]) for h in hs]
    scores = [qk[h] * k_scale * jnp.exp(log_d[h] - m_row[h]) for h in hs]
    inter = [jnp.exp(log_inter[h] - m_row[h]) for h in hs]
    num = [_dot(scores[h].astype(BF16), v[h]) + inter[h] * q_c[h] for h in hs]
    q_n = [jnp.sum(q[h].astype(F32) * n_st[h], axis=1, keepdims=True) for h in hs]
    den = [jnp.sum(scores[h], axis=1, keepdims=True) + inter[h] * q_n[h] for h in hs]
    for h in hs:
        hid = num[h] / jnp.maximum(jnp.abs(den[h]), jnp.exp(-m_row[h]))
        hid = hid * lax.rsqrt(jnp.mean(hid * hid, axis=1, keepdims=True) + ML_NORM_EPS)
        sl = slice(h * dv, (h + 1) * dv)
        out_ref[:, sl] = (hid * ng_ref[:, sl] * _sigmoid(o_ref[:, sl].astype(F32))).astype(out_ref.dtype)

    log_w = [g_last[h] - g_col[h] + i_col[h] for h in hs]
    m_new = [jnp.maximum(g_last[h] + m_st[h], jnp.max(log_w[h], axis=0, keepdims=True)) for h in hs]
    wk = [k[h].astype(F32) * (k_scale * jnp.exp(log_w[h] - m_new[h])) for h in hs]
    kv = [_dot_tn(wk[h].astype(BF16), v[h]) for h in hs]
    for h in hs:
        decay = jnp.exp(g_last[h] + m_st[h] - m_new[h])
        c_sc[h] = decay * c_st[h] + kv[h]
        n_sc[h] = decay * n_st[h] + jnp.sum(wk[h], axis=0, keepdims=True)
        m_sc[h] = m_new[h]


def _mlstm(proj, gates, norm_g, w_side, layer, batch, seq):
    t = proj.shape[0]
    H, L = ML_HEADS, ML_CHUNK
    G = 8
    dv = norm_g.shape[1] // H
    dqk = dv // 2
    nc = seq // L
    ng = H // G
    g = gates[:, :2 * H].reshape(batch, seq, 2 * H).transpose(0, 2, 1)
    i_g, f_g = g[:, :H], g[:, H:]
    col = lambda a: a.reshape(batch, H, seq, 1)
    rowv = lambda a: a.reshape(batch, H, nc, 1, L)
    col_spec = pl.BlockSpec((None, G, L, 1), lambda b, h, c: (b, h, c, 0))
    row_spec = pl.BlockSpec((None, G, None, 1, L), lambda b, h, c: (b, h, c, 0, 0))
    tok = lambda b, c: b * nc + c
    side_in, side_out, side_shape = _side_cast(w_side, layer, batch * ng * nc, lambda b, h, c: (b * ng + h) * nc + c)
    return pl.pallas_call(
        functools.partial(_mlstm_kernel, heads=G),
        out_shape=(jax.ShapeDtypeStruct((t, H * dv), BF16), side_shape),
        grid=(batch, ng, nc),
        in_specs=[pl.BlockSpec((L, G * dqk), lambda b, h, c: (tok(b, c), h)),
                  pl.BlockSpec((L, G * dqk), lambda b, h, c: (tok(b, c), ng + h)),
                  pl.BlockSpec((L, G * dv), lambda b, h, c: (tok(b, c), ng + h)),
                  pl.BlockSpec((L, G * dv), lambda b, h, c: (tok(b, c), 2 * ng + h)),
                  col_spec, col_spec, row_spec, row_spec,
                  pl.BlockSpec((1, G * dv), lambda b, h, c: (0, h)),
                  side_in],
        out_specs=(pl.BlockSpec((L, G * dv), lambda b, h, c: (tok(b, c), h)), side_out),
        scratch_shapes=[pltpu.VMEM((G, dqk, dv), F32), pltpu.VMEM((G, 1, dqk), F32), pltpu.VMEM((G, 1, 1), F32)],
        compiler_params=_params("parallel", "parallel", "arbitrary"),
        name="mlstm_chunk",
    )(proj, proj, proj, proj, col(i_g), col(f_g), rowv(i_g), rowv(f_g), norm_g, w_side)


def _rwkv_kernel(r_ref, k_ref, v_ref, lw_ref, la_ref, lgate_ref, w2_ref, a2_ref, g2_ref, w0_ref, a0_ref,
                 kk_ref, ka_ref, rk_ref, lg_ref, lb_ref, wside_ref, o_ref, wside16_ref, s_sc, *, heads):
    L, N = RW_CHUNK, RW_HEAD
    _side_cast_step(wside_ref, wside16_ref)
    gw = heads * N

    @pl.when(pl.program_id(2) == 0)
    def _():
        s_sc[...] = jnp.zeros_like(s_sc)

    P = 2 * L
    pairs = heads // 2

    r, k, v = r_ref[...].astype(F32), k_ref[...].astype(F32), v_ref[...].astype(F32)
    w_pre = w0_ref[...] + _dot(lw_ref[...], w2_ref[...])
    a_pre = a0_ref[...] + _dot(la_ref[...], a2_ref[...])
    gate = _dot(lgate_ref[...], g2_ref[...])
    log_decay = -jnp.exp(-_softplus(-w_pre) - 0.5)
    a = _sigmoid(a_pre)

    sw = min(gw, 256)
    li = lax.broadcasted_iota(jnp.int32, (sw, sw), 0) // N
    lj = lax.broadcasted_iota(jnp.int32, (sw, sw), 1) // N
    head_ones = jnp.where(li == lj, 1.0, 0.0).astype(BF16)

    def head_sum(x, split):
        parts = split(x)
        cols = []
        for c0 in range(0, gw, sw):
            acc = None
            for p in parts:
                d = _dot(p[:, c0:c0 + sw], head_ones)
                acc = d if acc is None else acc + d
            cols.append(acc)
        return cols[0] if len(cols) == 1 else jnp.concatenate(cols, axis=1)

    one_pass = lambda x: (x.astype(BF16),)

    t_i = lax.broadcasted_iota(jnp.int32, (L, L), 0)
    t_j = lax.broadcasted_iota(jnp.int32, (L, L), 1)
    tri = jnp.where(t_i >= t_j, 1.0, 0.0).astype(BF16)

    kkr = k * kk_ref[...]
    k = k * (1.0 + (a - 1.0) * ka_ref[...])
    sums = head_sum(jnp.concatenate([kkr * kkr, r * k * rk_ref[...]], axis=0), one_pass)
    kk = kkr * lax.rsqrt(jnp.maximum(sums[:L], 1e-24))
    bonus = sums[L:] * v

    cl = _dot_01(tri, log_decay)
    cl_last = cl[L - 1:L, :]
    gam = jnp.exp(cl)
    inv_gam = jnp.exp(-cl)
    gam_end = jnp.exp(cl_last)
    to_end = gam_end * inv_gam
    kka = kk * a
    r_hat = r * gam
    a_hat = -kk * jnp.exp(cl - log_decay)
    b_hat = kka * inv_gam
    k_hat = k * inv_gam
    b_end = kka * to_end
    k_end = k * to_end

    p_i = lax.broadcasted_iota(jnp.int32, (P, P), 0)
    p_j = lax.broadcasted_iota(jnp.int32, (P, P), 1)
    same = p_i // L == p_j // L
    strict = jnp.logical_and(same, p_i > p_j)
    incl = jnp.logical_and(same, p_i >= p_j)
    eye = jnp.where(p_i == p_j, 1.0, 0.0)
    levels = []
    s = 2
    while s < L:
        levels.append((s, jnp.logical_and(p_i // (2 * s) == p_j // (2 * s),
                                          jnp.logical_and(p_i % (2 * s) >= s, p_j % (2 * s) < s))))
        s *= 2
    own = (lax.broadcasted_iota(jnp.int32, (P, 2 * N), 0) // L
           == lax.broadcasted_iota(jnp.int32, (P, 2 * N), 1) // N)

    def stack(x):
        return jnp.where(own, jnp.concatenate([x, x], axis=0), 0.0).astype(BF16)

    rng = range(pairs)
    sls = [slice(p * 2 * N, (p + 1) * 2 * N) for p in rng]
    st = [s_sc[p] for p in rng]
    ar_s = [jnp.concatenate([stack(a_hat[:, sl]), stack(r_hat[:, sl])], axis=0) for sl in sls]
    v_s = [stack(v[:, sl]) for sl in sls]
    m4 = [_dot_nt(ar_s[p], jnp.concatenate([stack(b_hat[:, sls[p]]), stack(k_hat[:, sls[p]])], axis=0))
          for p in rng]
    a_ab = [jnp.where(strict, m[:P, :P], 0.0) for m in m4]
    a_ak = [jnp.where(strict, m[:P, P:], 0.0).astype(BF16) for m in m4]
    a_r = [jnp.concatenate([jnp.where(incl, m[P:, :P], 0.0), jnp.where(incl, m[P:, P:], 0.0)], axis=1).astype(BF16)
           for m in m4]
    from_state = [_dot_nt(ar_s[p], st[p].astype(BF16)) for p in rng]
    rhs_u = [from_state[p][:P] + _dot(a_ak[p], v_s[p]) for p in rng]
    x = [eye + jnp.where(p_i // 2 == p_j // 2, m, 0.0) for m in a_ab]
    for s, blk in levels:
        xb = [xp.astype(BF16) for xp in x]
        m_s = [jnp.where(blk, a_ab[p], 0.0).astype(BF16) for p in rng]
        if s % SUBLANES:
            half = [_dot(xb[p], m_s[p]).astype(BF16) for p in rng]
            x = [x[p] + _dot(half[p], xb[p]) for p in rng]
        else:
            lower = [jnp.concatenate([xp[b + s:b + 2 * s] for b in range(0, P, 2 * s)], axis=0) for xp in x]
            half = [_dot(lower[p].astype(BF16), m_s[p]).astype(BF16) for p in rng]
            lower = [lower[p] + _dot(half[p], xb[p]) for p in rng]
            x = [jnp.concatenate(
                [blk_rows for i, b in enumerate(range(0, P, 2 * s))
                 for blk_rows in (x[p][b:b + s], lower[p][i * s:(i + 1) * s])], axis=0) for p in rng]
    su = [_dot(x[p].astype(BF16), rhs_u[p].astype(BF16)) for p in rng]
    su_v = [jnp.concatenate([su[p].astype(BF16), v_s[p]], axis=0) for p in rng]
    y_s = [from_state[p][P:] + _dot(a_r[p], su_v[p]) for p in rng]
    for p in rng:
        be = jnp.concatenate([stack(b_end[:, sls[p]]), stack(k_end[:, sls[p]])], axis=0)
        s_sc[p] = st[p] * gam_end[:, sls[p]] + _dot_tn(su_v[p], be)
    ys = [yp[:L] + yp[L:] for yp in y_s]
    y = ys[0] if pairs == 1 else jnp.concatenate(ys, axis=1)

    inv_n = 1.0 / N
    mean = head_sum(y, one_pass) * inv_n
    yc = y - mean
    var = head_sum(yc * yc, one_pass) * inv_n
    yn = yc * lax.rsqrt(var + RW_GN_EPS) * lg_ref[...] + lb_ref[...]
    o_ref[...] = ((yn + bonus) * gate).astype(o_ref.dtype)


def _rwkv(r, k, v, lw, la, lgate, w2, a2, g2, layer, w0, a0, k_k, k_a, r_k, lnx_g, lnx_b, w_side, batch, seq):
    t, d = r.shape
    L = RW_CHUNK
    heads = min(64, d // RW_HEAD)
    gw = heads * RW_HEAD
    nc = seq // L
    nh = d // gw
    tile = pl.BlockSpec((L, gw), lambda b, h, c: (b * nc + c, h))
    row = pl.BlockSpec((1, gw), lambda b, h, c: (0, h))
    lora = lambda f: pl.BlockSpec((L, f.shape[1]), lambda b, h, c: (b * nc + c, 0))
    lora_w = lambda w: pl.BlockSpec((None, w.shape[1], gw), lambda b, h, c: (layer, 0, h))
    side_in, side_out, side_shape = _side_cast(w_side, layer, batch * nh * nc, lambda b, h, c: (b * nh + h) * nc + c)
    return pl.pallas_call(
        functools.partial(_rwkv_kernel, heads=heads),
        out_shape=(jax.ShapeDtypeStruct((t, d), BF16), side_shape),
        grid=(batch, nh, nc),
        in_specs=([tile] * 3 + [lora(lw), lora(la), lora(lgate), lora_w(w2), lora_w(a2), lora_w(g2)] + [row] * 7
                  + [side_in]),
        out_specs=(tile, side_out),
        scratch_shapes=[pltpu.VMEM((heads // 2, 2 * RW_HEAD, 2 * RW_HEAD), F32)],
        compiler_params=_params("parallel", "parallel", "arbitrary"),
        name="rwkv7_chunk",
    )(r, k, v, lw, la, lgate, w2, a2, g2, w0, a0, k_k, k_a, r_k, lnx_g, lnx_b, w_side)


def kernel(x, c, ada_w, ada_b, mix_ln_g, mix_ln_b, ffn_ln_g, ffn_ln_b, ffn_w_in, ffn_w_out, ml_w_in, ml_b_i, ml_b_f, ml_norm_g, ml_w_out, rw_mu, rw_w_r, rw_w_k, rw_w_v, rw_w0, rw_w1, rw_w2, rw_a0, rw_a1, rw_a2, rw_g1, rw_g2, rw_k_k, rw_k_a, rw_r_k, rw_lnx_g, rw_lnx_b, rw_w_o):
    batch, seq, d = x.shape
    depth = ada_w.shape[0]
    t = batch * seq
    alpha = (2 * depth) ** 0.25
    hidden = ffn_w_out.shape[1]
    n_mixers = 2

    mod = _ada(c, ada_w, ada_b).reshape(depth, batch, 6, 1, d)
    row = lambda p: p.reshape(1, -1)
    zero_bias = lambda n: jnp.zeros((1, n), F32)

    res = (x.reshape(t, d),)
    u = None
    for layer in range(depth):
        sh_m, sc_m, gt_m, sh_f, sc_f, gt_f = [mod[layer, :, i] for i in range(6)]
        j = layer // n_mixers
        g_m, b_m = row(mix_ln_g[layer]), row(mix_ln_b[layer])
        if layer % n_mixers == 0:
            if u is None:
                u = _modulate(res[0], sc_m, sh_m, seq)
            qkvo = ml_w_in.shape[2] - 2 * ML_HEADS
            proj = _mm(u, jnp.swapaxes(ml_w_in, 1, 2), j, qkvo, zero_bias(qkvo), _identity, BF16, "mlstm_in",
                       w_is_nk=True)
            w_gate = jnp.pad(ml_w_in[j:j + 1, :, qkvo:], ((0, 0), (0, 0), (0, LANES - 2 * ML_HEADS)))
            b_gate = jnp.pad(jnp.concatenate([ml_b_i[j], ml_b_f[j]]), (0, LANES - 2 * ML_HEADS)).reshape(1, LANES)
            gates = _mm(u, w_gate, 0, LANES, b_gate, _ml_gate_act, F32, "mlstm_gates")
            y, w_o = _mlstm(proj, gates, row(ml_norm_g[j]), ml_w_out, j, batch, seq)
            z = _mm_resid(y, w_o, res, gt_m, alpha, seq, "mlstm_out")
        else:
            xr, xw, xk, xv, xa, xg = u
            nb = zero_bias
            r = _mm(xr, rw_w_r, j, d, nb(d), _identity, BF16, "rwkv_r")
            k = _mm(xk, rw_w_k, j, d, nb(d), _identity, BF16, "rwkv_k")
            v = _mm(xv, rw_w_v, j, d, nb(d), _identity, BF16, "rwkv_v")
            n_w, n_a, n_g = rw_w1.shape[2], rw_a1.shape[2], rw_g1.shape[2]
            lw = _mm(xw, rw_w1, j, n_w, nb(n_w), jnp.tanh, BF16, "rwkv_w1")
            la = _mm(xa, rw_a1, j, n_a, nb(n_a), _identity, BF16, "rwkv_a1")
            lg = _mm(xg, rw_g1, j, n_g, nb(n_g), _sigmoid, BF16, "rwkv_g1")
            y, w_o = _rwkv(r, k, v, lw, la, lg, rw_w2.astype(BF16), rw_a2.astype(BF16), rw_g2.astype(BF16), j,
                           row(rw_w0[j]), row(rw_a0[j]), row(rw_k_k[j]), row(rw_k_a[j]), row(rw_r_k[j]),
                           row(rw_lnx_g[j]), row(rw_lnx_b[j]), rw_w_o, batch, seq)
            z = _mm_resid(y, w_o, res, gt_m, alpha, seq, "rwkv_out")
        stats, u = _layer_norm(z, g_m, b_m, seq, ("mod", sc_f, sh_f))
        res = (z, stats, g_m, b_m)
        hid, w_down = _mm_swiglu(u, ffn_w_in, ffn_w_out, layer, hidden)
        z = _mm_resid(hid, w_down, res, gt_f, alpha, seq, "ffn_out")
        g_f, b_f = row(ffn_ln_g[layer]), row(ffn_ln_b[layer])
        if layer + 1 == depth:
            return _layer_norm(z, g_f, b_f, seq).reshape(batch, seq, d)
        nsh, nsc = mod[layer + 1, :, 0], mod[layer + 1, :, 1]
        if (layer + 1) % n_mixers == 0:
            stats, u = _layer_norm(z, g_f, b_f, seq, ("mod", nsc, nsh))
        else:
            out = _layer_norm(z, g_f, b_f, seq, ("shift", nsc, nsh, rw_mu[(layer + 1) // n_mixers]))
            stats, u = out[0], out[1:]
        res = (z, stats, g_f, b_f)
```

```python
import functools

import jax
import jax.numpy as jnp
from jax import lax
from jax.experimental import pallas as pl
from jax.experimental.pallas import tpu as pltpu

F32 = jnp.float32
BF16 = jnp.bfloat16

ML_HEADS = 8
ML_CHUNK = 128
ML_GATE_CAP = 15.0
ML_NORM_EPS = 1e-6
RW_HEAD = 64
RW_CHUNK = 64
RW_GN_EPS = 64e-5
LN_EPS = 1e-5
LANES = 128
SUBLANES = 8
VMEM_LIMIT_BYTES = 56 * 1024 * 1024
TILE_VMEM_BUDGET_BYTES = 52 * 1024 * 1024
WIDE_VMEM_LIMIT_BYTES = 60 * 1024 * 1024
SHORT_K = 4096


def _params(*semantics, vmem=VMEM_LIMIT_BYTES):
    return pltpu.CompilerParams(dimension_semantics=semantics, vmem_limit_bytes=vmem)


def _pick(n, candidates):
    for c in candidates:
        if n % c == 0:
            return c
    return n


def _dot(a, b):
    return jnp.dot(a, b, preferred_element_type=F32)


def _dot_nt(a, b):
    return lax.dot_general(a, b, (((1,), (1,)), ((), ())), preferred_element_type=F32)


def _dot_tn(a, b):
    return lax.dot_general(a, b, (((0,), (0,)), ((), ())), preferred_element_type=F32)


def _split3(x):
    hi = x.astype(BF16)
    r = x - hi.astype(F32)
    mid = r.astype(BF16)
    lo = (r - mid.astype(F32)).astype(BF16)
    return hi, mid, lo


def _dot_01(m, x):
    hi, mid, lo = _split3(x)
    return _dot(m, hi) + (_dot(m, mid) + _dot(m, lo))


def _sigmoid(x):
    return 1.0 / (1.0 + jnp.exp(-x))


def _softplus(x):
    return jnp.maximum(x, 0.0) + jnp.log1p(jnp.exp(-jnp.abs(x)))


def _ada_kernel(c_ref, w_ref, b_ref, o_ref):
    c = c_ref[...]
    c_act = (c * _sigmoid(c)).astype(BF16)
    o_ref[0] = _dot(c_act, w_ref[0].astype(BF16)) + b_ref[0]


def _ada(c, ada_w, ada_b):
    depth, d, n = ada_w.shape
    b = c.shape[0]
    tn = _pick(n, (1024, 512, 256, 128))
    return pl.pallas_call(
        _ada_kernel,
        out_shape=jax.ShapeDtypeStruct((depth, b, n), F32),
        grid=(depth, n // tn),
        in_specs=[pl.BlockSpec((b, d), lambda l, j: (0, 0)),
                  pl.BlockSpec((1, d, tn), lambda l, j: (l, 0, j)),
                  pl.BlockSpec((1, 1, tn), lambda l, j: (l, 0, j))],
        out_specs=pl.BlockSpec((1, b, tn), lambda l, j: (l, 0, j)),
        compiler_params=_params("parallel", "parallel"),
        name="ada_mod",
    )(c, ada_w, ada_b.reshape(depth, 1, n))


def _modulate_kernel(x_ref, sc_ref, sh_ref, o_ref):
    o_ref[...] = (x_ref[...] * (1.0 + sc_ref[0]) + sh_ref[0]).astype(o_ref.dtype)


def _modulate(x2, sc, sh, seq):
    t, d = x2.shape
    ts = _pick(seq, (512, 256, 128))
    vec = pl.BlockSpec((1, 1, d), lambda i: ((i * ts) // seq, 0, 0))
    return pl.pallas_call(
        _modulate_kernel,
        out_shape=jax.ShapeDtypeStruct((t, d), BF16),
        grid=(t // ts,),
        in_specs=[pl.BlockSpec((ts, d), lambda i: (i, 0)), vec, vec],
        out_specs=pl.BlockSpec((ts, d), lambda i: (i, 0)),
        compiler_params=_params("parallel"),
        name="modulate",
    )(x2, sc, sh)


def _mm_kernel(a_ref, w_ref, b_ref, o_ref, *, act, w_is_nk):
    dot = _dot_nt if w_is_nk else _dot
    y = dot(a_ref[...], w_ref[...].astype(BF16)) + b_ref[...]
    o_ref[...] = act(y).astype(o_ref.dtype)


def _mm(a, w, layer, n, bias, act, out_dtype, name, w_is_nk=False):
    t, k = a.shape
    tm = _pick(t, (2048, 1024, 512, 256, 128))
    out_bytes = jnp.dtype(out_dtype).itemsize
    fits = lambda c: 2 * (tm * k * 2 + k * c * 4 + tm * c * out_bytes) <= TILE_VMEM_BUDGET_BYTES
    tn = _pick(n, [c for c in (512, 256, 128) if fits(c)])
    if w_is_nk:
        w_spec = pl.BlockSpec((None, tn, k), lambda i, j: (layer, j, 0))
    else:
        w_spec = pl.BlockSpec((None, k, tn), lambda i, j: (layer, 0, j))
    return pl.pallas_call(
        functools.partial(_mm_kernel, act=act, w_is_nk=w_is_nk),
        out_shape=jax.ShapeDtypeStruct((t, n), out_dtype),
        grid=(t // tm, n // tn),
        in_specs=[pl.BlockSpec((tm, k), lambda i, j: (i, 0)),
                  w_spec,
                  pl.BlockSpec((1, tn), lambda i, j: (0, j))],
        out_specs=pl.BlockSpec((tm, tn), lambda i, j: (i, j)),
        compiler_params=_params("parallel", "parallel"),
        name=name,
    )(a, w, bias)


def _identity(y):
    return y


def _resid_tn(k, n):
    return _pick(n, (1024, 512, 256, 128)) if k <= SHORT_K else _pick(n, (512, 256, 128))


def _side_cast(w, layer, steps, step_of):
    k, n = w.shape[1:]
    tn = _resid_tn(k, n)
    rows = k // steps
    assert rows * steps == k and rows % (2 * SUBLANES) == 0, (w.shape, steps)
    in_spec = pl.BlockSpec((None, rows, n), lambda *g: (layer, step_of(*g), 0))
    out_spec = pl.BlockSpec((n // tn, rows, tn), lambda *g: (0, step_of(*g), 0))
    return in_spec, out_spec, jax.ShapeDtypeStruct((n // tn, k, tn), BF16)


def _side_cast_step(src_ref, dst_ref):
    tn = dst_ref.shape[2]
    for jb in range(dst_ref.shape[0]):
        dst_ref[jb] = src_ref[:, jb * tn:(jb + 1) * tn].astype(dst_ref.dtype)


def _swiglu_kernel(a_ref, wg_ref, wu_ref, wo_ref, o_ref, wo16_ref):
    a = a_ref[...]
    gate = _dot(a, wg_ref[...].astype(BF16))
    up = _dot(a, wu_ref[...].astype(BF16))
    o_ref[...] = (gate * _sigmoid(gate) * up).astype(o_ref.dtype)
    _side_cast_step(wo_ref, wo16_ref)


def _mm_swiglu(a, w_in, w_out, layer, hidden):
    t, k = a.shape
    tm = _pick(t, (2048, 1024, 512, 256, 128))
    tn = _pick(hidden, (256, 128))
    nj = hidden // tn
    side_in, side_out, side_shape = _side_cast(w_out, layer, (t // tm) * nj, lambda i, j: i * nj + j)
    return pl.pallas_call(
        _swiglu_kernel,
        out_shape=(jax.ShapeDtypeStruct((t, hidden), BF16), side_shape),
        grid=(t // tm, nj),
        in_specs=[pl.BlockSpec((tm, k), lambda i, j: (i, 0)),
                  pl.BlockSpec((None, k, tn), lambda i, j: (layer, 0, j)),
                  pl.BlockSpec((None, k, tn), lambda i, j: (layer, 0, j + nj)),
                  side_in],
        out_specs=(pl.BlockSpec((tm, tn), lambda i, j: (i, j)), side_out),
        compiler_params=_params("parallel", "parallel", vmem=WIDE_VMEM_LIMIT_BYTES),
        name="ffn_in_swiglu",
    )(a, w_in, w_in, w_out)


def _ln_apply(z, mu, rstd, g, b):
    return (z - mu) * rstd * g + b


def _resid_kernel(a_ref, w_ref, x_ref, gt_ref, *rest, alpha, normed):
    if normed:
        st_ref, g_ref, b_ref, o_ref = rest
        st = st_ref[...]
        x = _ln_apply(x_ref[...], st[:, 0:1], st[:, 1:2], g_ref[...], b_ref[...])
    else:
        o_ref, = rest
        x = x_ref[...]
    o_ref[...] = alpha * x + gt_ref[0] * _dot(a_ref[...], w_ref[...])


def _mm_resid(a, w, res, gt, alpha, seq, name):
    t, k = a.shape
    nblk, _, tn = w.shape
    n = nblk * tn
    normed = len(res) > 1
    tm = _pick(min(t, seq), (1024, 512, 256, 128) if k <= SHORT_K else (512, 256, 128))
    in_specs = [pl.BlockSpec((tm, k), lambda i, j: (i, 0)),
                pl.BlockSpec((None, k, tn), lambda i, j: (j, 0, 0)),
                pl.BlockSpec((tm, tn), lambda i, j: (i, j)),
                pl.BlockSpec((1, 1, tn), lambda i, j: ((i * tm) // seq, 0, j))]
    operands = [a, w, res[0], gt]
    if normed:
        in_specs += [pl.BlockSpec((tm, 2), lambda i, j: (i, 0)),
                     pl.BlockSpec((1, tn), lambda i, j: (0, j)),
                     pl.BlockSpec((1, tn), lambda i, j: (0, j))]
        operands += list(res[1:])
    return pl.pallas_call(
        functools.partial(_resid_kernel, alpha=alpha, normed=normed),
        out_shape=jax.ShapeDtypeStruct((t, n), F32),
        grid=(t // tm, n // tn),
        in_specs=in_specs,
        out_specs=pl.BlockSpec((tm, tn), lambda i, j: (i, j)),
        compiler_params=_params("parallel", "parallel", vmem=WIDE_VMEM_LIMIT_BYTES),
        name=name,
    )(*operands)


def _col_chunks(d):
    cw = _pick(d, (512, 256, 128))
    return [slice(c, c + cw) for c in range(0, d, cw)]


def _ln_stats_ref(z_ref):
    d = z_ref.shape[1]
    chunks = _col_chunks(d)
    mu = jnp.sum(sum(z_ref[:, c] for c in chunks), axis=-1, keepdims=True) * (1.0 / d)
    sq = jnp.sum(sum(jnp.square(z_ref[:, c] - mu) for c in chunks), axis=-1, keepdims=True)
    return mu, lax.rsqrt(sq * (1.0 / d) + LN_EPS)


def _ln_stats(z):
    mu = jnp.mean(z, axis=-1, keepdims=True)
    zc = z - mu
    var = jnp.mean(zc * zc, axis=-1, keepdims=True)
    return mu, lax.rsqrt(var + LN_EPS)


def _ln_kernel(z_ref, g_ref, b_ref, x_ref):
    z = z_ref[...]
    mu, rstd = _ln_stats(z)
    x_ref[...] = _ln_apply(z, mu, rstd, g_ref[...], b_ref[...])


def _ln_mod_kernel(z_ref, g_ref, b_ref, sc_ref, sh_ref, st_ref, u_ref):
    z = z_ref[...]
    mu, rstd = _ln_stats(z)
    st_ref[:, 0:1] = mu
    st_ref[:, 1:2] = rstd
    scale = 1.0 + sc_ref[0]
    u_ref[...] = _ln_apply(z, mu, rstd, g_ref[...] * scale, b_ref[...] * scale + sh_ref[0]).astype(u_ref.dtype)


def _ln_shift_kernel(z_ref, zp_ref, g_ref, b_ref, sc_ref, sh_ref, mu_ref, st_ref, *mix_refs, ts, seq):
    mu, rstd = _ln_stats_ref(z_ref)
    st_ref[:, 0:1] = mu
    st_ref[:, 1:2] = rstd
    mu_p, rstd_p = _ln_stats_ref(zp_ref)
    first = (pl.program_id(0) * ts) % seq == 0
    row = lax.broadcasted_iota(jnp.int32, (ts, 1), 0)
    for c in _col_chunks(z_ref.shape[1]):
        scale = 1.0 + sc_ref[0, :, c]
        g, b = g_ref[:, c] * scale, b_ref[:, c] * scale + sh_ref[0, :, c]
        u = _ln_apply(z_ref[:, c], mu, rstd, g, b)
        prev = _ln_apply(zp_ref[:, c], mu_p, rstd_p, g, b)[SUBLANES - 1:SUBLANES, :]
        prev = jnp.where(first, 0.0, prev)
        u_prev = jnp.where(row == 0, prev, pltpu.roll(u, shift=1, axis=0))
        xx = (u_prev - u).astype(BF16)
        u = u.astype(BF16)
        for j, ref in enumerate(mix_refs):
            ref[:, c] = u + xx * mu_ref[j:j + 1, c].astype(BF16)


def _layer_norm(z, g, b, seq, nxt=None):
    t, d = z.shape
    row = pl.BlockSpec((1, d), lambda i: (0, 0))
    stats = jax.ShapeDtypeStruct((t, 2), F32)
    if nxt is None:
        ts = _pick(seq, (256, 128))
        tile = pl.BlockSpec((ts, d), lambda i: (i, 0))
        return pl.pallas_call(
            _ln_kernel, out_shape=jax.ShapeDtypeStruct((t, d), F32), grid=(t // ts,),
            in_specs=[tile, row, row], out_specs=tile,
            compiler_params=_params("parallel"), name="layer_norm",
        )(z, g, b)
    if nxt[0] == "mod":
        ts = _pick(seq, (256, 128))
        tile = pl.BlockSpec((ts, d), lambda i: (i, 0))
        vec = pl.BlockSpec((1, 1, d), lambda i: ((i * ts) // seq, 0, 0))
        return pl.pallas_call(
            _ln_mod_kernel,
            out_shape=(stats, jax.ShapeDtypeStruct((t, d), BF16)),
            grid=(t // ts,),
            in_specs=[tile, row, row, vec, vec],
            out_specs=(pl.BlockSpec((ts, 2), lambda i: (i, 0)), tile),
            compiler_params=_params("parallel"), name="layer_norm_mod",
        )(z, g, b, nxt[1], nxt[2])
    _, sc, sh, mu = nxt
    nmix = mu.shape[0]
    ts = _pick(seq, (128,))
    tile = pl.BlockSpec((ts, d), lambda i: (i, 0))
    prev = pl.BlockSpec((SUBLANES, d), lambda i: (jnp.maximum(i * (ts // SUBLANES) - 1, 0), 0))
    vec = pl.BlockSpec((1, 1, d), lambda i: ((i * ts) // seq, 0, 0))
    return pl.pallas_call(
        functools.partial(_ln_shift_kernel, ts=ts, seq=seq),
        out_shape=(stats,) + (jax.ShapeDtypeStruct((t, d), BF16),) * nmix,
        grid=(t // ts,),
        in_specs=[tile, prev, row, row, vec, vec, pl.BlockSpec((nmix, d), lambda i: (0, 0))],
        out_specs=(pl.BlockSpec((ts, 2), lambda i: (i, 0)),) + (tile,) * nmix,
        compiler_params=_params("parallel"), name="layer_norm_shift",
    )(z, z, g, b, sc, sh, mu)


def _ml_gate_act(y):
    lane = lax.broadcasted_iota(jnp.int32, y.shape, 1)
    i_pre = ML_GATE_CAP * jnp.tanh(y / ML_GATE_CAP)
    log_f = -_softplus(-y)
    return jnp.where(lane < ML_HEADS, i_pre, log_f)


def _mlstm_kernel(q_ref, k_ref, v_ref, o_ref, icol_ref, fcol_ref, irow_ref, frow_ref, ng_ref, wside_ref,
                  out_ref, wside16_ref, c_sc, n_sc, m_sc, *, heads):
    L = ML_CHUNK
    _side_cast_step(wside_ref, wside16_ref)

    @pl.when(pl.program_id(2) == 0)
    def _():
        c_sc[...] = jnp.zeros_like(c_sc)
        n_sc[...] = jnp.zeros_like(n_sc)
        m_sc[...] = jnp.zeros_like(m_sc)

    dqk = q_ref.shape[1] // heads
    dv = v_ref.shape[1] // heads
    k_scale = dqk ** -0.5
    r_idx = lax.broadcasted_iota(jnp.int32, (L, L), 0)
    c_idx = lax.broadcasted_iota(jnp.int32, (L, L), 1)
    causal = r_idx >= c_idx
    anti = r_idx <= c_idx

    hs = range(heads)
    q = [q_ref[:, h * dqk:(h + 1) * dqk] for h in hs]
    k = [k_ref[:, h * dqk:(h + 1) * dqk] for h in hs]
    v = [v_ref[:, h * dv:(h + 1) * dv] for h in hs]
    i_col, f_col = [icol_ref[h] for h in hs], [fcol_ref[h] for h in hs]
    i_row, f_row = [irow_ref[h] for h in hs], [frow_ref[h] for h in hs]
    c_st, n_st, m_st = [c_sc[h] for h in hs], [n_sc[h] for h in hs], [m_sc[h] for h in hs]

    qk = [_dot_nt(q[h], k[h]) for h in hs]
    q_c = [_dot(q[h], c_st[h].astype(BF16)) for h in hs]
    g_col = [jnp.sum(jnp.where(causal, f_row[h], 0.0), axis=1, keepdims=True) for h in hs]
    g_row = [jnp.sum(jnp.where(anti, f_col[h], 0.0), axis=0, keepdims=True) for h in hs]
    g_last = [jnp.sum(f_row[h], axis=1, keepdims=True) for h in hs]
    log_d = [jnp.where(causal, g_col[h] - g_row[h] + i_row[h], -jnp.inf) for h in hs]
    log_inter = [g_col[h] + m_st[h] for h in hs]
    m_row = [jnp.maximum(jnp.max(log_d[h], axis=1, keepdims=True), log_inter[h]) for h in hs]
    scores = [qk[h] * k_scale * jnp.exp(log_d[h] - m_row[h]) for h in hs]
    inter = [jnp.exp(log_inter[h] - m_row[h]) for h in hs]
    num = [_dot(scores[h].astype(BF16), v[h]) + inter[h] * q_c[h] for h in hs]
    q_n = [jnp.sum(q[h].astype(F32) * n_st[h], axis=1, keepdims=True) for h in hs]
    den = [jnp.sum(scores[h], axis=1, keepdims=True) + inter[h] * q_n[h] for h in hs]
    for h in hs:
        hid = num[h] / jnp.maximum(jnp.abs(den[h]), jnp.exp(-m_row[h]))
        hid = hid * lax.rsqrt(jnp.mean(hid * hid, axis=1, keepdims=True) + ML_NORM_EPS)
        sl = slice(h * dv, (h + 1) * dv)
        out_ref[:, sl] = (hid * ng_ref[:, sl] * _sigmoid(o_ref[:, sl].astype(F32))).astype(out_ref.dtype)

    log_w = [g_last[h] - g_col[h] + i_col[h] for h in hs]
    m_new = [jnp.maximum(g_last[h] + m_st[h], jnp.max(log_w[h], axis=0, keepdims=True)) for h in hs]
    wk = [k[h].astype(F32) * (k_scale * jnp.exp(log_w[h] - m_new[h])) for h in hs]
    kv = [_dot_tn(wk[h].astype(BF16), v[h]) for h in hs]
    for h in hs:
        decay = jnp.exp(g_last[h] + m_st[h] - m_new[h])
        c_sc[h] = decay * c_st[h] + kv[h]
        n_sc[h] = decay * n_st[h] + jnp.sum(wk[h], axis=0, keepdims=True)
        m_sc[h] = m_new[h]


def _mlstm(proj, gates, norm_g, w_side, layer, batch, seq):
    t = proj.shape[0]
    H, L = ML_HEADS, ML_CHUNK
    G = 8
    dv = norm_g.shape[1] // H
    dqk = dv // 2
    nc = seq // L
    ng = H // G
    g = gates[:, :2 * H].reshape(batch, seq, 2 * H).transpose(0, 2, 1)
    i_g, f_g = g[:, :H], g[:, H:]
    col = lambda a: a.reshape(batch, H, seq, 1)
    rowv = lambda a: a.reshape(batch, H, nc, 1, L)
    col_spec = pl.BlockSpec((None, G, L, 1), lambda b, h, c: (b, h, c, 0))
    row_spec = pl.BlockSpec((None, G, None, 1, L), lambda b, h, c: (b, h, c, 0, 0))
    tok = lambda b, c: b * nc + c
    side_in, side_out, side_shape = _side_cast(w_side, layer, batch * ng * nc, lambda b, h, c: (b * ng + h) * nc + c)
    return pl.pallas_call(
        functools.partial(_mlstm_kernel, heads=G),
        out_shape=(jax.ShapeDtypeStruct((t, H * dv), BF16), side_shape),
        grid=(batch, ng, nc),
        in_specs=[pl.BlockSpec((L, G * dqk), lambda b, h, c: (tok(b, c), h)),
                  pl.BlockSpec((L, G * dqk), lambda b, h, c: (tok(b, c), ng + h)),
                  pl.BlockSpec((L, G * dv), lambda b, h, c: (tok(b, c), ng + h)),
                  pl.BlockSpec((L, G * dv), lambda b, h, c: (tok(b, c), 2 * ng + h)),
                  col_spec, col_spec, row_spec, row_spec,
                  pl.BlockSpec((1, G * dv), lambda b, h, c: (0, h)),
                  side_in],
        out_specs=(pl.BlockSpec((L, G * dv), lambda b, h, c: (tok(b, c), h)), side_out),
        scratch_shapes=[pltpu.VMEM((G, dqk, dv), F32), pltpu.VMEM((G, 1, dqk), F32), pltpu.VMEM((G, 1, 1), F32)],
        compiler_params=_params("parallel", "parallel", "arbitrary"),
        name="mlstm_chunk",
    )(proj, proj, proj, proj, col(i_g), col(f_g), rowv(i_g), rowv(f_g), norm_g, w_side)


def _rwkv_kernel(r_ref, k_ref, v_ref, lw_ref, la_ref, lgate_ref, w2_ref, a2_ref, g2_ref, w0_ref, a0_ref,
                 kk_ref, ka_ref, rk_ref, lg_ref, lb_ref, wside_ref, o_ref, wside16_ref, s_sc, *, heads):
    L, N = RW_CHUNK, RW_HEAD
    _side_cast_step(wside_ref, wside16_ref)
    gw = heads * N

    @pl.when(pl.program_id(2) == 0)
    def _():
        s_sc[...] = jnp.zeros_like(s_sc)

    P = 2 * L
    pairs = heads // 2

    r, k, v = r_ref[...].astype(F32), k_ref[...].astype(F32), v_ref[...].astype(F32)
    w_pre = w0_ref[...] + _dot(lw_ref[...], w2_ref[...])
    a_pre = a0_ref[...] + _dot(la_ref[...], a2_ref[...])
    gate = _dot(lgate_ref[...], g2_ref[...])
    log_decay = -jnp.exp(-_softplus(-w_pre) - 0.5)
    a = _sigmoid(a_pre)

    sw = min(gw, 256)
    li = lax.broadcasted_iota(jnp.int32, (sw, sw), 0) // N
    lj = lax.broadcasted_iota(jnp.int32, (sw, sw), 1) // N
    head_ones = jnp.where(li == lj, 1.0, 0.0).astype(BF16)

    def head_sum(x, split):
        parts = split(x)
        cols = []
        for c0 in range(0, gw, sw):
            acc = None
            for p in parts:
                d = _dot(p[:, c0:c0 + sw], head_ones)
                acc = d if acc is None else acc + d
            cols.append(acc)
        return cols[0] if len(cols) == 1 else jnp.concatenate(cols, axis=1)

    one_pass = lambda x: (x.astype(BF16),)

    t_i = lax.broadcasted_iota(jnp.int32, (L, L), 0)
    t_j = lax.broadcasted_iota(jnp.int32, (L, L), 1)
    tri = jnp.where(t_i >= t_j, 1.0, 0.0).astype(BF16)

    kkr = k * kk_ref[...]
    k = k * (1.0 + (a - 1.0) * ka_ref[...])
    sums = head_sum(jnp.concatenate([kkr * kkr, r * k * rk_ref[...]], axis=0), one_pass)
    kk = kkr * lax.rsqrt(jnp.maximum(sums[:L], 1e-24))
    bonus = sums[L:] * v

    cl = _dot_01(tri, log_decay)
    cl_last = cl[L - 1:L, :]
    gam = jnp.exp(cl)
    inv_gam = jnp.exp(-cl)
    gam_end = jnp.exp(cl_last)
    to_end = gam_end * inv_gam
    kka = kk * a
    r_hat = r * gam
    a_hat = -kk * jnp.exp(cl - log_decay)
    b_hat = kka * inv_gam
    k_hat = k * inv_gam
    b_end = kka * to_end
    k_end = k * to_end

    p_i = lax.broadcasted_iota(jnp.int32, (P, P), 0)
    p_j = lax.broadcasted_iota(jnp.int32, (P, P), 1)
    same = p_i // L == p_j // L
    strict = jnp.logical_and(same, p_i > p_j)
    incl = jnp.logical_and(same, p_i >= p_j)
    eye = jnp.where(p_i == p_j, 1.0, 0.0)
    levels = []
    s = 2
    while s < L:
        levels.append((s, jnp.logical_and(p_i // (2 * s) == p_j // (2 * s),
                                          jnp.logical_and(p_i % (2 * s) >= s, p_j % (2 * s) < s))))
        s *= 2
    own = (lax.broadcasted_iota(jnp.int32, (P, 2 * N), 0) // L
           == lax.broadcasted_iota(jnp.int32, (P, 2 * N), 1) // N)

    def stack(x):
        return jnp.where(own, jnp.concatenate([x, x], axis=0), 0.0).astype(BF16)

    rng = range(pairs)
    sls = [slice(p * 2 * N, (p + 1) * 2 * N) for p in rng]
    st = [s_sc[p] for p in rng]
    ar_s = [jnp.concatenate([stack(a_hat[:, sl]), stack(r_hat[:, sl])], axis=0) for sl in sls]
    v_s = [stack(v[:, sl]) for sl in sls]
    m4 = [_dot_nt(ar_s[p], jnp.concatenate([stack(b_hat[:, sls[p]]), stack(k_hat[:, sls[p]])], axis=0))
          for p in rng]
    a_ab = [jnp.where(strict, m[:P, :P], 0.0) for m in m4]
    a_ak = [jnp.where(strict, m[:P, P:], 0.0).astype(BF16) for m in m4]
    a_r = [jnp.concatenate([jnp.where(incl, m[P:, :P], 0.0), jnp.where(incl, m[P:, P:], 0.0)], axis=1).astype(BF16)
           for m in m4]
    from_state = [_dot_nt(ar_s[p], st[p].astype(BF16)) for p in rng]
    rhs_u = [from_state[p][:P] + _dot(a_ak[p], v_s[p]) for p in rng]
    x = [eye + jnp.where(p_i // 2 == p_j // 2, m, 0.0) for m in a_ab]
    for s, blk in levels:
        xb = [xp.astype(BF16) for xp in x]
        m_s = [jnp.where(blk, a_ab[p], 0.0).astype(BF16) for p in rng]
        if s % SUBLANES:
            half = [_dot(xb[p], m_s[p]).astype(BF16) for p in rng]
            x = [x[p] + _dot(half[p], xb[p]) for p in rng]
        else:
            lower = [jnp.concatenate([xp[b + s:b + 2 * s] for b in range(0, P, 2 * s)], axis=0) for xp in x]
            half = [_dot(lower[p].astype(BF16), m_s[p]).astype(BF16) for p in rng]
            lower = [lower[p] + _dot(half[p], xb[p]) for p in rng]
            x = [jnp.concatenate(
                [blk_rows for i, b in enumerate(range(0, P, 2 * s))
                 for blk_rows in (x[p][b:b + s], lower[p][i * s:(i + 1) * s])], axis=0) for p in rng]
    su = [_dot(x[p].astype(BF16), rhs_u[p].astype(BF16)) for p in rng]
    su_v = [jnp.concatenate([su[p].astype(BF16), v_s[p]], axis=0) for p in rng]
    y_s = [from_state[p][P:] + _dot(a_r[p], su_v[p]) for p in rng]
    for p in rng:
        be = jnp.concatenate([stack(b_end[:, sls[p]]), stack(k_end[:, sls[p]])], axis=0)
        s_sc[p] = st[p] * gam_end[:, sls[p]] + _dot_tn(su_v[p], be)
    ys = [yp[:L] + yp[L:] for yp in y_s]
    y = ys[0] if pairs == 1 else jnp.concatenate(ys, axis=1)

    inv_n = 1.0 / N
    mean = head_sum(y, one_pass) * inv_n
    yc = y - mean
    var = head_sum(yc * yc, one_pass) * inv_n
    yn = yc * lax.rsqrt(var + RW_GN_EPS) * lg_ref[...] + lb_ref[...]
    o_ref[...] = ((yn + bonus) * gate).astype(o_ref.dtype)


def _rwkv(r, k, v, lw, la, lgate, w2, a2, g2, layer, w0, a0, k_k, k_a, r_k, lnx_g, lnx_b, w_side, batch, seq):
    t, d = r.shape
    L = RW_CHUNK
    heads = min(64, d // RW_HEAD)
    gw = heads * RW_HEAD
    nc = seq // L
    nh = d // gw
    tile = pl.BlockSpec((L, gw), lambda b, h, c: (b * nc + c, h))
    row = pl.BlockSpec((1, gw), lambda b, h, c: (0, h))
    lora = lambda f: pl.BlockSpec((L, f.shape[1]), lambda b, h, c: (b * nc + c, 0))
    lora_w = lambda w: pl.BlockSpec((None, w.shape[1], gw), lambda b, h, c: (layer, 0, h))
    side_in, side_out, side_shape = _side_cast(w_side, layer, batch * nh * nc, lambda b, h, c: (b * nh + h) * nc + c)
    return pl.pallas_call(
        functools.partial(_rwkv_kernel, heads=heads),
        out_shape=(jax.ShapeDtypeStruct((t, d), BF16), side_shape),
        grid=(batch, nh, nc),
        in_specs=([tile] * 3 + [lora(lw), lora(la), lora(lgate), lora_w(w2), lora_w(a2), lora_w(g2)] + [row] * 7
                  + [side_in]),
        out_specs=(tile, side_out),
        scratch_shapes=[pltpu.VMEM((heads // 2, 2 * RW_HEAD, 2 * RW_HEAD), F32)],
        compiler_params=_params("parallel", "parallel", "arbitrary"),
        name="rwkv7_chunk",
    )(r, k, v, lw, la, lgate, w2, a2, g2, w0, a0, k_k, k_a, r_k, lnx_g, lnx_b, w_side)


def kernel(x, c, ada_w, ada_b, mix_ln_g, mix_ln_b, ffn_ln_g, ffn_ln_b, ffn_w_in, ffn_w_out, ml_w_in, ml_b_i, ml_b_f, ml_norm_g, ml_w_out, rw_mu, rw_w_r, rw_w_k, rw_w_v, rw_w0, rw_w1, rw_w2, rw_a0, rw_a1, rw_a2, rw_g1, rw_g2, rw_k_k, rw_k_a, rw_r_k, rw_lnx_g, rw_lnx_b, rw_w_o):
    batch, seq, d = x.shape
    depth = ada_w.shape[0]
    t = batch * seq
    alpha = (2 * depth) ** 0.25
    hidden = ffn_w_out.shape[1]
    n_mixers = 2

    mod = _ada(c, ada_w, ada_b).reshape(depth, batch, 6, 1, d)
    row = lambda p: p.reshape(1, -1)
    zero_bias = lambda n: jnp.zeros((1, n), F32)

    res = (x.reshape(t, d),)
    u = None
    for layer in range(depth):
        sh_m, sc_m, gt_m, sh_f, sc_f, gt_f = [mod[layer, :, i] for i in range(6)]
        j = layer // n_mixers
        g_m, b_m = row(mix_ln_g[layer]), row(mix_ln_b[layer])
        if layer % n_mixers == 0:
            if u is None:
                u = _modulate(res[0], sc_m, sh_m, seq)
            qkvo = ml_w_in.shape[2] - 2 * ML_HEADS
            proj = _mm(u, jnp.swapaxes(ml_w_in, 1, 2), j, qkvo, zero_bias(qkvo), _identity, BF16, "mlstm_in",
                       w_is_nk=True)
            w_gate = jnp.pad(ml_w_in[j:j + 1, :, qkvo:], ((0, 0), (0, 0), (0, LANES - 2 * ML_HEADS)))
            b_gate = jnp.pad(jnp.concatenate([ml_b_i[j], ml_b_f[j]]), (0, LANES - 2 * ML_HEADS)).reshape(1, LANES)
            gates = _mm(u, w_gate, 0, LANES, b_gate, _ml_gate_act, F32, "mlstm_gates")
            y, w_o = _mlstm(proj, gates, row(ml_norm_g[j]), ml_w_out, j, batch, seq)
            z = _mm_resid(y, w_o, res, gt_m, alpha, seq, "mlstm_out")
        else:
            xr, xw, xk, xv, xa, xg = u
            nb = zero_bias
            r = _mm(xr, rw_w_r, j, d, nb(d), _identity, BF16, "rwkv_r")
            k = _mm(xk, rw_w_k, j, d, nb(d), _identity, BF16, "rwkv_k")
            v = _mm(xv, rw_w_v, j, d, nb(d), _identity, BF16, "rwkv_v")
            n_w, n_a, n_g = rw_w1.shape[2], rw_a1.shape[2], rw_g1.shape[2]
            lw = _mm(xw, rw_w1, j, n_w, nb(n_w), jnp.tanh, BF16, "rwkv_w1")
            la = _mm(xa, rw_a1, j, n_a, nb(n_a), _identity, BF16, "rwkv_a1")
            lg = _mm(xg, rw_g1, j, n_g, nb(n_g), _sigmoid, BF16, "rwkv_g1")
            y, w_o = _rwkv(r, k, v, lw, la, lg, rw_w2.astype(BF16), rw_a2.astype(BF16), rw_g2.astype(BF16), j,
                           row(rw_w0[j]), row(rw_a0[j]), row(rw_k_k[j]), row(rw_k_a[j]), row(rw_r_k[j]),
                           row(rw_lnx_g[j]), row(rw_lnx_b[j]), rw_w_o, batch, seq)
            z = _mm_resid(y, w_o, res, gt_m, alpha, seq, "rwkv_out")
        stats, u = _layer_norm(z, g_m, b_m, seq, ("mod", sc_f, sh_f))
        res = (z, stats, g_m, b_m)
        hid, w_down = _mm_swiglu(u, ffn_w_in, ffn_w_out, layer, hidden)
        z = _mm_resid(hid, w_down, res, gt_f, alpha, seq, "ffn_out")
        g_f, b_f = row(ffn_ln_g[layer]), row(ffn_ln_b[layer])
        if layer + 1 == depth:
            return _layer_norm(z, g_f, b_f, seq).reshape(batch, seq, d)
        nsh, nsc = mod[layer + 1, :, 0], mod[layer + 1, :, 1]
        if (layer + 1) % n_mixers == 0:
            stats, u = _layer_norm(z, g_f, b_f, seq, ("mod", nsc, nsh))
        else:
            out = _layer_norm(z, g_f, b_f, seq, ("shift", nsc, nsh, rw_mu[(layer + 1) // n_mixers]))
            stats, u = out[0], out[1:]
        res = (z, stats, g_f, b_f)
```

```python
import functools

import jax
import jax.numpy as jnp
from jax import lax
from jax.experimental import pallas as pl
from jax.experimental.pallas import tpu as pltpu

F32 = jnp.float32
BF16 = jnp.bfloat16

ML_HEADS = 8
ML_CHUNK = 128
ML_GATE_CAP = 15.0
ML_NORM_EPS = 1e-6
RW_HEAD = 64
RW_CHUNK = 64
RW_GN_EPS = 64e-5
LN_EPS = 1e-5
LANES = 128
SUBLANES = 8
VMEM_LIMIT_BYTES = 56 * 1024 * 1024
TILE_VMEM_BUDGET_BYTES = 52 * 1024 * 1024
WIDE_VMEM_LIMIT_BYTES = 60 * 1024 * 1024
SHORT_K = 4096


def _params(*semantics, vmem=VMEM_LIMIT_BYTES):
    return pltpu.CompilerParams(dimension_semantics=semantics, vmem_limit_bytes=vmem)


def _pick(n, candidates):
    for c in candidates:
        if n % c == 0:
            return c
    return n


def _dot(a, b):
    return jnp.dot(a, b, preferred_element_type=F32)


def _dot_nt(a, b):
    return lax.dot_general(a, b, (((1,), (1,)), ((), ())), preferred_element_type=F32)


def _dot_tn(a, b):
    return lax.dot_general(a, b, (((0,), (0,)), ((), ())), preferred_element_type=F32)


def _split3(x):
    hi = x.astype(BF16)
    r = x - hi.astype(F32)
    mid = r.astype(BF16)
    lo = (r - mid.astype(F32)).astype(BF16)
    return hi, mid, lo


def _dot_01(m, x):
    hi, mid, lo = _split3(x)
    return _dot(m, hi) + (_dot(m, mid) + _dot(m, lo))


def _sigmoid(x):
    return 1.0 / (1.0 + jnp.exp(-x))


def _softplus(x):
    return jnp.maximum(x, 0.0) + jnp.log1p(jnp.exp(-jnp.abs(x)))


def _ada_kernel(c_ref, w_ref, b_ref, o_ref):
    c = c_ref[...]
    c_act = (c * _sigmoid(c)).astype(BF16)
    o_ref[0] = _dot(c_act, w_ref[0].astype(BF16)) + b_ref[0]


def _ada(c, ada_w, ada_b):
    depth, d, n = ada_w.shape
    b = c.shape[0]
    tn = _pick(n, (1024, 512, 256, 128))
    return pl.pallas_call(
        _ada_kernel,
        out_shape=jax.ShapeDtypeStruct((depth, b, n), F32),
        grid=(depth, n // tn),
        in_specs=[pl.BlockSpec((b, d), lambda l, j: (0, 0)),
                  pl.BlockSpec((1, d, tn), lambda l, j: (l, 0, j)),
                  pl.BlockSpec((1, 1, tn), lambda l, j: (l, 0, j))],
        out_specs=pl.BlockSpec((1, b, tn), lambda l, j: (l, 0, j)),
        compiler_params=_params("parallel", "parallel"),
        name="ada_mod",
    )(c, ada_w, ada_b.reshape(depth, 1, n))


def _modulate_kernel(x_ref, sc_ref, sh_ref, o_ref):
    o_ref[...] = (x_ref[...] * (1.0 + sc_ref[0]) + sh_ref[0]).astype(o_ref.dtype)


def _modulate(x2, sc, sh, seq):
    t, d = x2.shape
    ts = _pick(seq, (512, 256, 128))
    vec = pl.BlockSpec((1, 1, d), lambda i: ((i * ts) // seq, 0, 0))
    return pl.pallas_call(
        _modulate_kernel,
        out_shape=jax.ShapeDtypeStruct((t, d), BF16),
        grid=(t // ts,),
        in_specs=[pl.BlockSpec((ts, d), lambda i: (i, 0)), vec, vec],
        out_specs=pl.BlockSpec((ts, d), lambda i: (i, 0)),
        compiler_params=_params("parallel"),
        name="modulate",
    )(x2, sc, sh)


def _mm_kernel(a_ref, w_ref, b_ref, o_ref, *, act, w_is_nk):
    dot = _dot_nt if w_is_nk else _dot
    y = dot(a_ref[...], w_ref[...].astype(BF16)) + b_ref[...]
    o_ref[...] = act(y).astype(o_ref.dtype)


def _mm(a, w, layer, n, bias, act, out_dtype, name, w_is_nk=False):
    t, k = a.shape
    tm = _pick(t, (2048, 1024, 512, 256, 128))
    out_bytes = jnp.dtype(out_dtype).itemsize
    fits = lambda c: 2 * (tm * k * 2 + k * c * 4 + tm * c * out_bytes) <= TILE_VMEM_BUDGET_BYTES
    tn = _pick(n, [c for c in (512, 256, 128) if fits(c)])
    if w_is_nk:
        w_spec = pl.BlockSpec((None, tn, k), lambda i, j: (layer, j, 0))
    else:
        w_spec = pl.BlockSpec((None, k, tn), lambda i, j: (layer, 0, j))
    return pl.pallas_call(
        functools.partial(_mm_kernel, act=act, w_is_nk=w_is_nk),
        out_shape=jax.ShapeDtypeStruct((t, n), out_dtype),
        grid=(t // tm, n // tn),
        in_specs=[pl.BlockSpec((tm, k), lambda i, j: (i, 0)),
                  w_spec,
                  pl.BlockSpec((1, tn), lambda i, j: (0, j))],
        out_specs=pl.BlockSpec((tm, tn), lambda i, j: (i, j)),
        compiler_params=_params("parallel", "parallel"),
        name=name,
    )(a, w, bias)


def _identity(y):
    return y


def _side_cast(w, layer, steps, step_of):
    k, n = w.shape[1:]
    rows = k // steps
    assert rows * steps == k and rows % (2 * SUBLANES) == 0, (w.shape, steps)
    in_spec = pl.BlockSpec((None, rows, n), lambda *g: (layer, step_of(*g), 0))
    out_spec = pl.BlockSpec((rows, n), lambda *g: (step_of(*g), 0))
    return in_spec, out_spec, jax.ShapeDtypeStruct((k, n), BF16)


def _side_cast_step(src_ref, dst_ref):
    dst_ref[...] = src_ref[...].astype(dst_ref.dtype)


def _swiglu_kernel(a_ref, wg_ref, wu_ref, wo_ref, o_ref, wo16_ref):
    a = a_ref[...]
    gate = _dot(a, wg_ref[...].astype(BF16))
    up = _dot(a, wu_ref[...].astype(BF16))
    o_ref[...] = (gate * _sigmoid(gate) * up).astype(o_ref.dtype)
    _side_cast_step(wo_ref, wo16_ref)


def _mm_swiglu(a, w_in, w_out, layer, hidden):
    t, k = a.shape
    tm = _pick(t, (2048, 1024, 512, 256, 128))
    tn = _pick(hidden, (256, 128))
    nj = hidden // tn
    side_in, side_out, side_shape = _side_cast(w_out, layer, (t // tm) * nj, lambda i, j: i * nj + j)
    return pl.pallas_call(
        _swiglu_kernel,
        out_shape=(jax.ShapeDtypeStruct((t, hidden), BF16), side_shape),
        grid=(t // tm, nj),
        in_specs=[pl.BlockSpec((tm, k), lambda i, j: (i, 0)),
                  pl.BlockSpec((None, k, tn), lambda i, j: (layer, 0, j)),
                  pl.BlockSpec((None, k, tn), lambda i, j: (layer, 0, j + nj)),
                  side_in],
        out_specs=(pl.BlockSpec((tm, tn), lambda i, j: (i, j)), side_out),
        compiler_params=_params("parallel", "parallel", vmem=WIDE_VMEM_LIMIT_BYTES),
        name="ffn_in_swiglu",
    )(a, w_in, w_in, w_out)


def _ln_apply(z, mu, rstd, g, b):
    return (z - mu) * rstd * g + b


def _resid_kernel(a_ref, w_ref, x_ref, gt_ref, *rest, alpha, normed):
    if normed:
        st_ref, g_ref, b_ref, o_ref = rest
        st = st_ref[...]
        x = _ln_apply(x_ref[...], st[:, 0:1], st[:, 1:2], g_ref[...], b_ref[...])
    else:
        o_ref, = rest
        x = x_ref[...]
    o_ref[...] = alpha * x + gt_ref[0] * _dot(a_ref[...], w_ref[...])


def _mm_resid(a, w, res, gt, alpha, seq, name):
    t, k = a.shape
    n = w.shape[1]
    normed = len(res) > 1
    sizes = (1024, 512, 256, 128) if k <= SHORT_K else (512, 256, 128)
    tm, tn = _pick(min(t, seq), sizes), _pick(n, sizes)
    in_specs = [pl.BlockSpec((tm, k), lambda i, j: (i, 0)),
                pl.BlockSpec((k, tn), lambda i, j: (0, j)),
                pl.BlockSpec((tm, tn), lambda i, j: (i, j)),
                pl.BlockSpec((1, 1, tn), lambda i, j: ((i * tm) // seq, 0, j))]
    operands = [a, w, res[0], gt]
    if normed:
        in_specs += [pl.BlockSpec((tm, 2), lambda i, j: (i, 0)),
                     pl.BlockSpec((1, tn), lambda i, j: (0, j)),
                     pl.BlockSpec((1, tn), lambda i, j: (0, j))]
        operands += list(res[1:])
    return pl.pallas_call(
        functools.partial(_resid_kernel, alpha=alpha, normed=normed),
        out_shape=jax.ShapeDtypeStruct((t, n), F32),
        grid=(t // tm, n // tn),
        in_specs=in_specs,
        out_specs=pl.BlockSpec((tm, tn), lambda i, j: (i, j)),
        compiler_params=_params("parallel", "parallel", vmem=WIDE_VMEM_LIMIT_BYTES),
        name=name,
    )(*operands)


def _col_chunks(d):
    cw = _pick(d, (512, 256, 128))
    return [slice(c, c + cw) for c in range(0, d, cw)]


def _ln_stats_ref(z_ref):
    d = z_ref.shape[1]
    chunks = _col_chunks(d)
    mu = jnp.sum(sum(z_ref[:, c] for c in chunks), axis=-1, keepdims=True) * (1.0 / d)
    sq = jnp.sum(sum(jnp.square(z_ref[:, c] - mu) for c in chunks), axis=-1, keepdims=True)
    return mu, lax.rsqrt(sq * (1.0 / d) + LN_EPS)


def _ln_stats(z):
    mu = jnp.mean(z, axis=-1, keepdims=True)
    zc = z - mu
    var = jnp.mean(zc * zc, axis=-1, keepdims=True)
    return mu, lax.rsqrt(var + LN_EPS)


def _ln_kernel(z_ref, g_ref, b_ref, x_ref):
    z = z_ref[...]
    mu, rstd = _ln_stats(z)
    x_ref[...] = _ln_apply(z, mu, rstd, g_ref[...], b_ref[...])


def _ln_mod_kernel(z_ref, g_ref, b_ref, sc_ref, sh_ref, st_ref, u_ref):
    z = z_ref[...]
    mu, rstd = _ln_stats(z)
    st_ref[:, 0:1] = mu
    st_ref[:, 1:2] = rstd
    scale = 1.0 + sc_ref[0]
    u_ref[...] = _ln_apply(z, mu, rstd, g_ref[...] * scale, b_ref[...] * scale + sh_ref[0]).astype(u_ref.dtype)


def _ln_shift_kernel(z_ref, zp_ref, g_ref, b_ref, sc_ref, sh_ref, mu_ref, st_ref, *mix_refs, ts, seq):
    mu, rstd = _ln_stats_ref(z_ref)
    st_ref[:, 0:1] = mu
    st_ref[:, 1:2] = rstd
    mu_p, rstd_p = _ln_stats_ref(zp_ref)
    first = (pl.program_id(0) * ts) % seq == 0
    row = lax.broadcasted_iota(jnp.int32, (ts, 1), 0)
    for c in _col_chunks(z_ref.shape[1]):
        scale = 1.0 + sc_ref[0, :, c]
        g, b = g_ref[:, c] * scale, b_ref[:, c] * scale + sh_ref[0, :, c]
        u = _ln_apply(z_ref[:, c], mu, rstd, g, b)
        prev = _ln_apply(zp_ref[:, c], mu_p, rstd_p, g, b)[SUBLANES - 1:SUBLANES, :]
        prev = jnp.where(first, 0.0, prev)
        u_prev = jnp.where(row == 0, prev, pltpu.roll(u, shift=1, axis=0))
        xx = (u_prev - u).astype(BF16)
        u = u.astype(BF16)
        for j, ref in enumerate(mix_refs):
            ref[:, c] = u + xx * mu_ref[j:j + 1, c].astype(BF16)


def _layer_norm(z, g, b, seq, nxt=None):
    t, d = z.shape
    row = pl.BlockSpec((1, d), lambda i: (0, 0))
    stats = jax.ShapeDtypeStruct((t, 2), F32)
    if nxt is None:
        ts = _pick(seq, (256, 128))
        tile = pl.BlockSpec((ts, d), lambda i: (i, 0))
        return pl.pallas_call(
            _ln_kernel, out_shape=jax.ShapeDtypeStruct((t, d), F32), grid=(t // ts,),
            in_specs=[tile, row, row], out_specs=tile,
            compiler_params=_params("parallel"), name="layer_norm",
        )(z, g, b)
    if nxt[0] == "mod":
        ts = _pick(seq, (256, 128))
        tile = pl.BlockSpec((ts, d), lambda i: (i, 0))
        vec = pl.BlockSpec((1, 1, d), lambda i: ((i * ts) // seq, 0, 0))
        return pl.pallas_call(
            _ln_mod_kernel,
            out_shape=(stats, jax.ShapeDtypeStruct((t, d), BF16)),
            grid=(t // ts,),
            in_specs=[tile, row, row, vec, vec],
            out_specs=(pl.BlockSpec((ts, 2), lambda i: (i, 0)), tile),
            compiler_params=_params("parallel"), name="layer_norm_mod",
        )(z, g, b, nxt[1], nxt[2])
    _, sc, sh, mu = nxt
    nmix = mu.shape[0]
    ts = _pick(seq, (128,))
    tile = pl.BlockSpec((ts, d), lambda i: (i, 0))
    prev = pl.BlockSpec((SUBLANES, d), lambda i: (jnp.maximum(i * (ts // SUBLANES) - 1, 0), 0))
    vec = pl.BlockSpec((1, 1, d), lambda i: ((i * ts) // seq, 0, 0))
    return pl.pallas_call(
        functools.partial(_ln_shift_kernel, ts=ts, seq=seq),
        out_shape=(stats,) + (jax.ShapeDtypeStruct((t, d), BF16),) * nmix,
        grid=(t // ts,),
        in_specs=[tile, prev, row, row, vec, vec, pl.BlockSpec((nmix, d), lambda i: (0, 0))],
        out_specs=(pl.BlockSpec((ts, 2), lambda i: (i, 0)),) + (tile,) * nmix,
        compiler_params=_params("parallel"), name="layer_norm_shift",
    )(z, z, g, b, sc, sh, mu)


def _ml_gate_act(y):
    lane = lax.broadcasted_iota(jnp.int32, y.shape, 1)
    i_pre = ML_GATE_CAP * jnp.tanh(y / ML_GATE_CAP)
    log_f = -_softplus(-y)
    return jnp.where(lane < ML_HEADS, i_pre, log_f)


def _mlstm_kernel(q_ref, k_ref, v_ref, o_ref, icol_ref, fcol_ref, irow_ref, frow_ref, ng_ref, wside_ref,
                  out_ref, wside16_ref, c_sc, n_sc, m_sc, *, heads):
    L = ML_CHUNK
    _side_cast_step(wside_ref, wside16_ref)

    @pl.when(pl.program_id(2) == 0)
    def _():
        c_sc[...] = jnp.zeros_like(c_sc)
        n_sc[...] = jnp.zeros_like(n_sc)
        m_sc[...] = jnp.zeros_like(m_sc)

    dqk = q_ref.shape[1] // heads
    dv = v_ref.shape[1] // heads
    k_scale = dqk ** -0.5
    r_idx = lax.broadcasted_iota(jnp.int32, (L, L), 0)
    c_idx = lax.broadcasted_iota(jnp.int32, (L, L), 1)
    causal = r_idx >= c_idx
    anti = r_idx <= c_idx

    hs = range(heads)
    q = [q_ref[:, h * dqk:(h + 1) * dqk] for h in hs]
    k = [k_ref[:, h * dqk:(h + 1) * dqk] for h in hs]
    v = [v_ref[:, h * dv:(h + 1) * dv] for h in hs]
    i_col, f_col = [icol_ref[h] for h in hs], [fcol_ref[h] for h in hs]
    i_row, f_row = [irow_ref[h] for h in hs], [frow_ref[h] for h in hs]
    c_st, n_st, m_st = [c_sc[h] for h in hs], [n_sc[h] for h in hs], [m_sc[h] for h in hs]

    qk = [_dot_nt(q[h], k[h]) for h in hs]
    q_c = [_dot(q[h], c_st[h].astype(BF16)) for h in hs]
    g_col = [jnp.sum(jnp.where(causal, f_row[h], 0.0), axis=1, keepdims=True) for h in hs]
    g_row = [jnp.sum(jnp.where(anti, f_col[h], 0.0), axis=0, keepdims=True) for h in hs]
    g_last = [jnp.sum(f_row[h], axis=1, keepdims=True) for h in hs]
    log_d = [jnp.where(causal, g_col[h] - g_row[h] + i_row[h], -jnp.inf) for h in hs]
    log_inter = [g_col[h] + m_st[h] for h in hs]
    m_row = [jnp.maximum(jnp.max(log_d[h], axis=1, keepdims=True), log_inter[h]) for h in hs]
    scores = [qk[h] * k_scale * jnp.exp(log_d[h] - m_row[h]) for h in hs]
    inter = [jnp.exp(log_inter[h] - m_row[h]) for h in hs]
    num = [_dot(scores[h].astype(BF16), v[h]) + inter[h] * q_c[h] for h in hs]
    q_n = [jnp.sum(q[h].astype(F32) * n_st[h], axis=1, keepdims=True) for h in hs]
    den = [jnp.sum(scores[h], axis=1, keepdims=True) + inter[h] * q_n[h] for h in hs]
    for h in hs:
        hid = num[h] / jnp.maximum(jnp.abs(den[h]), jnp.exp(-m_row[h]))
        hid = hid * lax.rsqrt(jnp.mean(hid * hid, axis=1, keepdims=True) + ML_NORM_EPS)
        sl = slice(h * dv, (h + 1) * dv)
        out_ref[:, sl] = (hid * ng_ref[:, sl] * _sigmoid(o_ref[:, sl].astype(F32))).astype(out_ref.dtype)

    log_w = [g_last[h] - g_col[h] + i_col[h] for h in hs]
    m_new = [jnp.maximum(g_last[h] + m_st[h], jnp.max(log_w[h], axis=0, keepdims=True)) for h in hs]
    wk = [k[h].astype(F32) * (k_scale * jnp.exp(log_w[h] - m_new[h])) for h in hs]
    kv = [_dot_tn(wk[h].astype(BF16), v[h]) for h in hs]
    for h in hs:
        decay = jnp.exp(g_last[h] + m_st[h] - m_new[h])
        c_sc[h] = decay * c_st[h] + kv[h]
        n_sc[h] = decay * n_st[h] + jnp.sum(wk[h], axis=0, keepdims=True)
        m_sc[h] = m_new[h]


def _mlstm(proj, gates, norm_g, w_side, layer, batch, seq):
    t = proj.shape[0]
    H, L = ML_HEADS, ML_CHUNK
    G = 8
    dv = norm_g.shape[1] // H
    dqk = dv // 2
    nc = seq // L
    ng = H // G
    g = gates[:, :2 * H].reshape(batch, seq, 2 * H).transpose(0, 2, 1)
    i_g, f_g = g[:, :H], g[:, H:]
    col = lambda a: a.reshape(batch, H, seq, 1)
    rowv = lambda a: a.reshape(batch, H, nc, 1, L)
    col_spec = pl.BlockSpec((None, G, L, 1), lambda b, h, c: (b, h, c, 0))
    row_spec = pl.BlockSpec((None, G, None, 1, L), lambda b, h, c: (b, h, c, 0, 0))
    tok = lambda b, c: b * nc + c
    side_in, side_out, side_shape = _side_cast(w_side, layer, batch * ng * nc, lambda b, h, c: (b * ng + h) * nc + c)
    return pl.pallas_call(
        functools.partial(_mlstm_kernel, heads=G),
        out_shape=(jax.ShapeDtypeStruct((t, H * dv), BF16), side_shape),
        grid=(batch, ng, nc),
        in_specs=[pl.BlockSpec((L, G * dqk), lambda b, h, c: (tok(b, c), h)),
                  pl.BlockSpec((L, G * dqk), lambda b, h, c: (tok(b, c), ng + h)),
                  pl.BlockSpec((L, G * dv), lambda b, h, c: (tok(b, c), ng + h)),
                  pl.BlockSpec((L, G * dv), lambda b, h, c: (tok(b, c), 2 * ng + h)),
                  col_spec, col_spec, row_spec, row_spec,
                  pl.BlockSpec((1, G * dv), lambda b, h, c: (0, h)),
                  side_in],
        out_specs=(pl.BlockSpec((L, G * dv), lambda b, h, c: (tok(b, c), h)), side_out),
        scratch_shapes=[pltpu.VMEM((G, dqk, dv), F32), pltpu.VMEM((G, 1, dqk), F32), pltpu.VMEM((G, 1, 1), F32)],
        compiler_params=_params("parallel", "parallel", "arbitrary"),
        name="mlstm_chunk",
    )(proj, proj, proj, proj, col(i_g), col(f_g), rowv(i_g), rowv(f_g), norm_g, w_side)


def _rwkv_kernel(r_ref, k_ref, v_ref, lw_ref, la_ref, lgate_ref, w2_ref, a2_ref, g2_ref, w0_ref, a0_ref,
                 kk_ref, ka_ref, rk_ref, lg_ref, lb_ref, wside_ref, o_ref, wside16_ref, s_sc, *, heads):
    L, N = RW_CHUNK, RW_HEAD
    _side_cast_step(wside_ref, wside16_ref)
    gw = heads * N

    @pl.when(pl.program_id(2) == 0)
    def _():
        s_sc[...] = jnp.zeros_like(s_sc)

    P = 2 * L
    pairs = heads // 2

    r, k, v = r_ref[...].astype(F32), k_ref[...].astype(F32), v_ref[...].astype(F32)
    w_pre = w0_ref[...] + _dot(lw_ref[...], w2_ref[...])
    a_pre = a0_ref[...] + _dot(la_ref[...], a2_ref[...])
    gate = _dot(lgate_ref[...], g2_ref[...])
    log_decay = -jnp.exp(-_softplus(-w_pre) - 0.5)
    a = _sigmoid(a_pre)

    sw = min(gw, 256)
    li = lax.broadcasted_iota(jnp.int32, (sw, sw), 0) // N
    lj = lax.broadcasted_iota(jnp.int32, (sw, sw), 1) // N
    head_ones = jnp.where(li == lj, 1.0, 0.0).astype(BF16)

    def head_sum(x, split):
        parts = split(x)
        cols = []
        for c0 in range(0, gw, sw):
            acc = None
            for p in parts:
                d = _dot(p[:, c0:c0 + sw], head_ones)
                acc = d if acc is None else acc + d
            cols.append(acc)
        return cols[0] if len(cols) == 1 else jnp.concatenate(cols, axis=1)

    one_pass = lambda x: (x.astype(BF16),)

    t_i = lax.broadcasted_iota(jnp.int32, (L, L), 0)
    t_j = lax.broadcasted_iota(jnp.int32, (L, L), 1)
    tri = jnp.where(t_i >= t_j, 1.0, 0.0).astype(BF16)

    kkr = k * kk_ref[...]
    k = k * (1.0 + (a - 1.0) * ka_ref[...])
    sums = head_sum(jnp.concatenate([kkr * kkr, r * k * rk_ref[...]], axis=0), one_pass)
    kk = kkr * lax.rsqrt(jnp.maximum(sums[:L], 1e-24))
    bonus = sums[L:] * v

    cl = _dot_01(tri, log_decay)
    cl_last = cl[L - 1:L, :]
    gam = jnp.exp(cl)
    inv_gam = jnp.exp(-cl)
    gam_end = jnp.exp(cl_last)
    to_end = gam_end * inv_gam
    kka = kk * a
    r_hat = r * gam
    a_hat = -kk * jnp.exp(cl - log_decay)
    b_hat = kka * inv_gam
    k_hat = k * inv_gam
    b_end = kka * to_end
    k_end = k * to_end

    p_i = lax.broadcasted_iota(jnp.int32, (P, P), 0)
    p_j = lax.broadcasted_iota(jnp.int32, (P, P), 1)
    same = p_i // L == p_j // L
    strict = jnp.logical_and(same, p_i > p_j)
    incl = jnp.logical_and(same, p_i >= p_j)
    eye = jnp.where(p_i == p_j, 1.0, 0.0)
    levels = []
    s = 2
    while s < L:
        levels.append((s, jnp.logical_and(p_i // (2 * s) == p_j // (2 * s),
                                          jnp.logical_and(p_i % (2 * s) >= s, p_j % (2 * s) < s))))
        s *= 2
    own = (lax.broadcasted_iota(jnp.int32, (P, 2 * N), 0) // L
           == lax.broadcasted_iota(jnp.int32, (P, 2 * N), 1) // N)

    def stack(x):
        return jnp.where(own, jnp.concatenate([x, x], axis=0), 0.0).astype(BF16)

    rng = range(pairs)
    sls = [slice(p * 2 * N, (p + 1) * 2 * N) for p in rng]
    st = [s_sc[p] for p in rng]
    ar_s = [jnp.concatenate([stack(a_hat[:, sl]), stack(r_hat[:, sl])], axis=0) for sl in sls]
    v_s = [stack(v[:, sl]) for sl in sls]
    m4 = [_dot_nt(ar_s[p], jnp.concatenate([stack(b_hat[:, sls[p]]), stack(k_hat[:, sls[p]])], axis=0))
          for p in rng]
    a_ab = [jnp.where(strict, m[:P, :P], 0.0) for m in m4]
    a_ak = [jnp.where(strict, m[:P, P:], 0.0).astype(BF16) for m in m4]
    a_r = [jnp.concatenate([jnp.where(incl, m[P:, :P], 0.0), jnp.where(incl, m[P:, P:], 0.0)], axis=1).astype(BF16)
           for m in m4]
    from_state = [_dot_nt(ar_s[p], st[p].astype(BF16)) for p in rng]
    rhs_u = [from_state[p][:P] + _dot(a_ak[p], v_s[p]) for p in rng]
    x = [eye + jnp.where(p_i // 2 == p_j // 2, m, 0.0) for m in a_ab]
    for s, blk in levels:
        xb = [xp.astype(BF16) for xp in x]
        m_s = [jnp.where(blk, a_ab[p], 0.0).astype(BF16) for p in rng]
        if s % SUBLANES:
            half = [_dot(xb[p], m_s[p]).astype(BF16) for p in rng]
            x = [x[p] + _dot(half[p], xb[p]) for p in rng]
        else:
            lower = [jnp.concatenate([xp[b + s:b + 2 * s] for b in range(0, P, 2 * s)], axis=0) for xp in x]
            half = [_dot(lower[p].astype(BF16), m_s[p]).astype(BF16) for p in rng]
            lower = [lower[p] + _dot(half[p], xb[p]) for p in rng]
            x = [jnp.concatenate(
                [blk_rows for i, b in enumerate(range(0, P, 2 * s))
                 for blk_rows in (x[p][b:b + s], lower[p][i * s:(i + 1) * s])], axis=0) for p in rng]
    su = [_dot(x[p].astype(BF16), rhs_u[p].astype(BF16)) for p in rng]
    su_v = [jnp.concatenate([su[p].astype(BF16), v_s[p]], axis=0) for p in rng]
    y_s = [from_state[p][P:] + _dot(a_r[p], su_v[p]) for p in rng]
    for p in rng:
        be = jnp.concatenate([stack(b_end[:, sls[p]]), stack(k_end[:, sls[p]])], axis=0)
        s_sc[p] = st[p] * gam_end[:, sls[p]] + _dot_tn(su_v[p], be)
    ys = [yp[:L] + yp[L:] for yp in y_s]
    y = ys[0] if pairs == 1 else jnp.concatenate(ys, axis=1)

    inv_n = 1.0 / N
    mean = head_sum(y, one_pass) * inv_n
    yc = y - mean
    var = head_sum(yc * yc, one_pass) * inv_n
    yn = yc * lax.rsqrt(var + RW_GN_EPS) * lg_ref[...] + lb_ref[...]
    o_ref[...] = ((yn + bonus) * gate).astype(o_ref.dtype)


def _rwkv(r, k, v, lw, la, lgate, w2, a2, g2, layer, w0, a0, k_k, k_a, r_k, lnx_g, lnx_b, w_side, batch, seq):
    t, d = r.shape
    L = RW_CHUNK
    heads = min(64, d // RW_HEAD)
    gw = heads * RW_HEAD
    nc = seq // L
    nh = d // gw
    tile = pl.BlockSpec((L, gw), lambda b, h, c: (b * nc + c, h))
    row = pl.BlockSpec((1, gw), lambda b, h, c: (0, h))
    lora = lambda f: pl.BlockSpec((L, f.shape[1]), lambda b, h, c: (b * nc + c, 0))
    lora_w = lambda w: pl.BlockSpec((None, w.shape[1], gw), lambda b, h, c: (layer, 0, h))
    side_in, side_out, side_shape = _side_cast(w_side, layer, batch * nh * nc, lambda b, h, c: (b * nh + h) * nc + c)
    return pl.pallas_call(
        functools.partial(_rwkv_kernel, heads=heads),
        out_shape=(jax.ShapeDtypeStruct((t, d), BF16), side_shape),
        grid=(batch, nh, nc),
        in_specs=([tile] * 3 + [lora(lw), lora(la), lora(lgate), lora_w(w2), lora_w(a2), lora_w(g2)] + [row] * 7
                  + [side_in]),
        out_specs=(tile, side_out),
        scratch_shapes=[pltpu.VMEM((heads // 2, 2 * RW_HEAD, 2 * RW_HEAD), F32)],
        compiler_params=_params("parallel", "parallel", "arbitrary"),
        name="rwkv7_chunk",
    )(r, k, v, lw, la, lgate, w2, a2, g2, w0, a0, k_k, k_a, r_k, lnx_g, lnx_b, w_side)


def kernel(x, c, ada_w, ada_b, mix_ln_g, mix_ln_b, ffn_ln_g, ffn_ln_b, ffn_w_in, ffn_w_out, ml_w_in, ml_b_i, ml_b_f, ml_norm_g, ml_w_out, rw_mu, rw_w_r, rw_w_k, rw_w_v, rw_w0, rw_w1, rw_w2, rw_a0, rw_a1, rw_a2, rw_g1, rw_g2, rw_k_k, rw_k_a, rw_r_k, rw_lnx_g, rw_lnx_b, rw_w_o):
    batch, seq, d = x.shape
    depth = ada_w.shape[0]
    t = batch * seq
    alpha = (2 * depth) ** 0.25
    hidden = ffn_w_out.shape[1]
    n_mixers = 2

    mod = _ada(c, ada_w, ada_b).reshape(depth, batch, 6, 1, d)
    row = lambda p: p.reshape(1, -1)
    zero_bias = lambda n: jnp.zeros((1, n), F32)

    res = (x.reshape(t, d),)
    u = None
    for layer in range(depth):
        sh_m, sc_m, gt_m, sh_f, sc_f, gt_f = [mod[layer, :, i] for i in range(6)]
        j = layer // n_mixers
        g_m, b_m = row(mix_ln_g[layer]), row(mix_ln_b[layer])
        if layer % n_mixers == 0:
            if u is None:
                u = _modulate(res[0], sc_m, sh_m, seq)
            qkvo = ml_w_in.shape[2] - 2 * ML_HEADS
            proj = _mm(u, jnp.swapaxes(ml_w_in, 1, 2), j, qkvo, zero_bias(qkvo), _identity, BF16, "mlstm_in",
                       w_is_nk=True)
            w_gate = jnp.pad(ml_w_in[j:j + 1, :, qkvo:], ((0, 0), (0, 0), (0, LANES - 2 * ML_HEADS)))
            b_gate = jnp.pad(jnp.concatenate([ml_b_i[j], ml_b_f[j]]), (0, LANES - 2 * ML_HEADS)).reshape(1, LANES)
            gates = _mm(u, w_gate, 0, LANES, b_gate, _ml_gate_act, F32, "mlstm_gates")
            y, w_o = _mlstm(proj, gates, row(ml_norm_g[j]), ml_w_out, j, batch, seq)
            z = _mm_resid(y, w_o, res, gt_m, alpha, seq, "mlstm_out")
        else:
            xr, xw, xk, xv, xa, xg = u
            nb = zero_bias
            r = _mm(xr, rw_w_r, j, d, nb(d), _identity, BF16, "rwkv_r")
            k = _mm(xk, rw_w_k, j, d, nb(d), _identity, BF16, "rwkv_k")
            v = _mm(xv, rw_w_v, j, d, nb(d), _identity, BF16, "rwkv_v")
            n_w, n_a, n_g = rw_w1.shape[2], rw_a1.shape[2], rw_g1.shape[2]
            lw = _mm(xw, rw_w1, j, n_w, nb(n_w), jnp.tanh, BF16, "rwkv_w1")
            la = _mm(xa, rw_a1, j, n_a, nb(n_a), _identity, BF16, "rwkv_a1")
            lg = _mm(xg, rw_g1, j, n_g, nb(n_g), _sigmoid, BF16, "rwkv_g1")
            y, w_o = _rwkv(r, k, v, lw, la, lg, rw_w2.astype(BF16), rw_a2.astype(BF16), rw_g2.astype(BF16), j,
                           row(rw_w0[j]), row(rw_a0[j]), row(rw_k_k[j]), row(rw_k_a[j]), row(rw_r_k[j]),
                           row(rw_lnx_g[j]), row(rw_lnx_b[j]), rw_w_o, batch, seq)
            z = _mm_resid(y, w_o, res, gt_m, alpha, seq, "rwkv_out")
        stats, u = _layer_norm(z, g_m, b_m, seq, ("mod", sc_f, sh_f))
        res = (z, stats, g_m, b_m)
        hid, w_down = _mm_swiglu(u, ffn_w_in, ffn_w_out, layer, hidden)
        z = _mm_resid(hid, w_down, res, gt_f, alpha, seq, "ffn_out")
        g_f, b_f = row(ffn_ln_g[layer]), row(ffn_ln_b[layer])
        if layer + 1 == depth:
            return _layer_norm(z, g_f, b_f, seq).reshape(batch, seq, d)
        nsh, nsc = mod[layer + 1, :, 0], mod[layer + 1, :, 1]
        if (layer + 1) % n_mixers == 0:
            stats, u = _layer_norm(z, g_f, b_f, seq, ("mod", nsc, nsh))
        else:
            out = _layer_norm(z, g_f, b_f, seq, ("shift", nsc, nsh, rw_mu[(layer + 1) // n_mixers]))
            stats, u = out[0], out[1:]
        res = (z, stats, g_f, b_f)
```

```python
import functools

import jax
import jax.numpy as jnp
from jax import lax
from jax.experimental import pallas as pl
from jax.experimental.pallas import tpu as pltpu

F32 = jnp.float32
BF16 = jnp.bfloat16

ML_HEADS = 8
ML_CHUNK = 128
ML_GATE_CAP = 15.0
ML_NORM_EPS = 1e-6
RW_HEAD = 64
RW_CHUNK = 64
RW_GN_EPS = 64e-5
LN_EPS = 1e-5
LANES = 128
SUBLANES = 8
VMEM_LIMIT_BYTES = 56 * 1024 * 1024
TILE_VMEM_BUDGET_BYTES = 52 * 1024 * 1024
WIDE_VMEM_LIMIT_BYTES = 60 * 1024 * 1024
SHORT_K = 4096


def _params(*semantics, vmem=VMEM_LIMIT_BYTES):
    return pltpu.CompilerParams(dimension_semantics=semantics, vmem_limit_bytes=vmem)


def _pick(n, candidates):
    for c in candidates:
        if n % c == 0:
            return c
    return n


def _dot(a, b):
    return jnp.dot(a, b, preferred_element_type=F32)


def _dot_nt(a, b):
    return lax.dot_general(a, b, (((1,), (1,)), ((), ())), preferred_element_type=F32)


def _dot_tn(a, b):
    return lax.dot_general(a, b, (((0,), (0,)), ((), ())), preferred_element_type=F32)


def _split3(x):
    hi = x.astype(BF16)
    r = x - hi.astype(F32)
    mid = r.astype(BF16)
    lo = (r - mid.astype(F32)).astype(BF16)
    return hi, mid, lo


def _dot_01(m, x):
    hi, mid, lo = _split3(x)
    return _dot(m, hi) + (_dot(m, mid) + _dot(m, lo))


def _sigmoid(x):
    return 1.0 / (1.0 + jnp.exp(-x))


def _softplus(x):
    return jnp.maximum(x, 0.0) + jnp.log1p(jnp.exp(-jnp.abs(x)))


def _ada_kernel(c_ref, w_ref, b_ref, o_ref):
    c = c_ref[...]
    c_act = (c * _sigmoid(c)).astype(BF16)
    o_ref[0] = _dot(c_act, w_ref[0].astype(BF16)) + b_ref[0]


def _ada(c, ada_w, ada_b):
    depth, d, n = ada_w.shape
    b = c.shape[0]
    tn = _pick(n, (1024, 512, 256, 128))
    return pl.pallas_call(
        _ada_kernel,
        out_shape=jax.ShapeDtypeStruct((depth, b, n), F32),
        grid=(depth, n // tn),
        in_specs=[pl.BlockSpec((b, d), lambda l, j: (0, 0)),
                  pl.BlockSpec((1, d, tn), lambda l, j: (l, 0, j)),
                  pl.BlockSpec((1, 1, tn), lambda l, j: (l, 0, j))],
        out_specs=pl.BlockSpec((1, b, tn), lambda l, j: (l, 0, j)),
        compiler_params=_params("parallel", "parallel"),
        name="ada_mod",
    )(c, ada_w, ada_b.reshape(depth, 1, n))


def _modulate_kernel(x_ref, sc_ref, sh_ref, w_ref, b_ref, o_ref, g_ref, *, act):
    u = (x_ref[...] * (1.0 + sc_ref[0]) + sh_ref[0]).astype(o_ref.dtype)
    o_ref[...] = u
    g_ref[...] = act(_dot(u, w_ref[0].astype(BF16)) + b_ref[...])


def _modulate(x2, sc, sh, seq, w, bias, act):
    t, d = x2.shape
    n = w.shape[2]
    ts = _pick(seq, (512, 256, 128))
    vec = pl.BlockSpec((1, 1, d), lambda i: ((i * ts) // seq, 0, 0))
    return pl.pallas_call(
        functools.partial(_modulate_kernel, act=act),
        out_shape=(jax.ShapeDtypeStruct((t, d), BF16), jax.ShapeDtypeStruct((t, n), F32)),
        grid=(t // ts,),
        in_specs=[pl.BlockSpec((ts, d), lambda i: (i, 0)), vec, vec,
                  pl.BlockSpec((1, d, n), lambda i: (0, 0, 0)), pl.BlockSpec((1, n), lambda i: (0, 0))],
        out_specs=(pl.BlockSpec((ts, d), lambda i: (i, 0)), pl.BlockSpec((ts, n), lambda i: (i, 0))),
        compiler_params=_params("parallel"),
        name="modulate",
    )(x2, sc, sh, w, bias)


def _mm_kernel(a_ref, w_ref, b_ref, o_ref, *, act, w_is_nk):
    dot = _dot_nt if w_is_nk else _dot
    y = dot(a_ref[...], w_ref[...].astype(BF16)) + b_ref[...]
    o_ref[...] = act(y).astype(o_ref.dtype)


def _mm(a, w, layer, n, bias, act, out_dtype, name, w_is_nk=False):
    t, k = a.shape
    tm = _pick(t, (2048, 1024, 512, 256, 128))
    out_bytes = jnp.dtype(out_dtype).itemsize
    fits = lambda c: 2 * (tm * k * 2 + k * c * 4 + tm * c * out_bytes) <= TILE_VMEM_BUDGET_BYTES
    tn = _pick(n, [c for c in (512, 256, 128) if fits(c)])
    if w_is_nk:
        w_spec = pl.BlockSpec((None, tn, k), lambda i, j: (layer, j, 0))
    else:
        w_spec = pl.BlockSpec((None, k, tn), lambda i, j: (layer, 0, j))
    return pl.pallas_call(
        functools.partial(_mm_kernel, act=act, w_is_nk=w_is_nk),
        out_shape=jax.ShapeDtypeStruct((t, n), out_dtype),
        grid=(t // tm, n // tn),
        in_specs=[pl.BlockSpec((tm, k), lambda i, j: (i, 0)),
                  w_spec,
                  pl.BlockSpec((1, tn), lambda i, j: (0, j))],
        out_specs=pl.BlockSpec((tm, tn), lambda i, j: (i, j)),
        compiler_params=_params("parallel", "parallel"),
        name=name,
    )(a, w, bias)


def _identity(y):
    return y


def _side_cast(w, layer, steps, step_of):
    k, n = w.shape[1:]
    rows = k // steps
    assert rows * steps == k and rows % (2 * SUBLANES) == 0, (w.shape, steps)
    in_spec = pl.BlockSpec((None, rows, n), lambda *g: (layer, step_of(*g), 0))
    out_spec = pl.BlockSpec((rows, n), lambda *g: (step_of(*g), 0))
    return in_spec, out_spec, jax.ShapeDtypeStruct((k, n), BF16)


def _side_cast_step(src_ref, dst_ref):
    dst_ref[...] = src_ref[...].astype(dst_ref.dtype)


def _swiglu_kernel(a_ref, wg_ref, wu_ref, wo_ref, o_ref, wo16_ref):
    a = a_ref[...]
    gate = _dot(a, wg_ref[...].astype(BF16))
    up = _dot(a, wu_ref[...].astype(BF16))
    o_ref[...] = (gate * _sigmoid(gate) * up).astype(o_ref.dtype)
    _side_cast_step(wo_ref, wo16_ref)


def _mm_swiglu(a, w_in, w_out, layer, hidden):
    t, k = a.shape
    tm = _pick(t, (2048, 1024, 512, 256, 128))
    tn = _pick(hidden, (256, 128))
    nj = hidden // tn
    side_in, side_out, side_shape = _side_cast(w_out, layer, (t // tm) * nj, lambda i, j: i * nj + j)
    return pl.pallas_call(
        _swiglu_kernel,
        out_shape=(jax.ShapeDtypeStruct((t, hidden), BF16), side_shape),
        grid=(t // tm, nj),
        in_specs=[pl.BlockSpec((tm, k), lambda i, j: (i, 0)),
                  pl.BlockSpec((None, k, tn), lambda i, j: (layer, 0, j)),
                  pl.BlockSpec((None, k, tn), lambda i, j: (layer, 0, j + nj)),
                  side_in],
        out_specs=(pl.BlockSpec((tm, tn), lambda i, j: (i, j)), side_out),
        compiler_params=_params("parallel", "parallel", vmem=WIDE_VMEM_LIMIT_BYTES),
        name="ffn_in_swiglu",
    )(a, w_in, w_in, w_out)


def _ln_apply(z, mu, rstd, g, b):
    return (z - mu) * rstd * g + b


def _resid_kernel(a_ref, w_ref, x_ref, gt_ref, *rest, alpha, normed):
    if normed:
        st_ref, g_ref, b_ref, o_ref = rest
        st = st_ref[...]
        x = _ln_apply(x_ref[...], st[:, 0:1], st[:, 1:2], g_ref[...], b_ref[...])
    else:
        o_ref, = rest
        x = x_ref[...]
    o_ref[...] = alpha * x + gt_ref[0] * _dot(a_ref[...], w_ref[...])


def _mm_resid(a, w, res, gt, alpha, seq, name):
    t, k = a.shape
    n = w.shape[1]
    normed = len(res) > 1
    sizes = (1024, 512, 256, 128) if k <= SHORT_K else (512, 256, 128)
    tm, tn = _pick(min(t, seq), sizes), _pick(n, sizes)
    in_specs = [pl.BlockSpec((tm, k), lambda i, j: (i, 0)),
                pl.BlockSpec((k, tn), lambda i, j: (0, j)),
                pl.BlockSpec((tm, tn), lambda i, j: (i, j)),
                pl.BlockSpec((1, 1, tn), lambda i, j: ((i * tm) // seq, 0, j))]
    operands = [a, w, res[0], gt]
    if normed:
        in_specs += [pl.BlockSpec((tm, 2), lambda i, j: (i, 0)),
                     pl.BlockSpec((1, tn), lambda i, j: (0, j)),
                     pl.BlockSpec((1, tn), lambda i, j: (0, j))]
        operands += list(res[1:])
    return pl.pallas_call(
        functools.partial(_resid_kernel, alpha=alpha, normed=normed),
        out_shape=jax.ShapeDtypeStruct((t, n), F32),
        grid=(t // tm, n // tn),
        in_specs=in_specs,
        out_specs=pl.BlockSpec((tm, tn), lambda i, j: (i, j)),
        compiler_params=_params("parallel", "parallel", vmem=WIDE_VMEM_LIMIT_BYTES),
        name=name,
    )(*operands)


def _col_chunks(d):
    cw = _pick(d, (512, 256, 128))
    return [slice(c, c + cw) for c in range(0, d, cw)]


def _ln_stats_ref(z_ref):
    d = z_ref.shape[1]
    chunks = _col_chunks(d)
    mu = jnp.sum(sum(z_ref[:, c] for c in chunks), axis=-1, keepdims=True) * (1.0 / d)
    sq = jnp.sum(sum(jnp.square(z_ref[:, c] - mu) for c in chunks), axis=-1, keepdims=True)
    return mu, lax.rsqrt(sq * (1.0 / d) + LN_EPS)


def _ln_stats(z):
    mu = jnp.mean(z, axis=-1, keepdims=True)
    zc = z - mu
    var = jnp.mean(zc * zc, axis=-1, keepdims=True)
    return mu, lax.rsqrt(var + LN_EPS)


def _ln_kernel(z_ref, g_ref, b_ref, x_ref):
    z = z_ref[...]
    mu, rstd = _ln_stats(z)
    x_ref[...] = _ln_apply(z, mu, rstd, g_ref[...], b_ref[...])


def _ln_mod_kernel(z_ref, g_ref, b_ref, sc_ref, sh_ref, st_ref, u_ref):
    z = z_ref[...]
    mu, rstd = _ln_stats(z)
    st_ref[:, 0:1] = mu
    st_ref[:, 1:2] = rstd
    scale = 1.0 + sc_ref[0]
    u_ref[...] = _ln_apply(z, mu, rstd, g_ref[...] * scale, b_ref[...] * scale + sh_ref[0]).astype(u_ref.dtype)


def _ln_shift_kernel(z_ref, zp_ref, g_ref, b_ref, sc_ref, sh_ref, mu_ref, st_ref, *mix_refs, ts, seq):
    mu, rstd = _ln_stats_ref(z_ref)
    st_ref[:, 0:1] = mu
    st_ref[:, 1:2] = rstd
    mu_p, rstd_p = _ln_stats_ref(zp_ref)
    first = (pl.program_id(0) * ts) % seq == 0
    row = lax.broadcasted_iota(jnp.int32, (ts, 1), 0)
    for c in _col_chunks(z_ref.shape[1]):
        scale = 1.0 + sc_ref[0, :, c]
        g, b = g_ref[:, c] * scale, b_ref[:, c] * scale + sh_ref[0, :, c]
        u = _ln_apply(z_ref[:, c], mu, rstd, g, b)
        prev = _ln_apply(zp_ref[:, c], mu_p, rstd_p, g, b)[SUBLANES - 1:SUBLANES, :]
        prev = jnp.where(first, 0.0, prev)
        u_prev = jnp.where(row == 0, prev, pltpu.roll(u, shift=1, axis=0))
        xx = (u_prev - u).astype(BF16)
        u = u.astype(BF16)
        for j, ref in enumerate(mix_refs):
            ref[:, c] = u + xx * mu_ref[j:j + 1, c].astype(BF16)


def _layer_norm(z, g, b, seq, nxt=None):
    t, d = z.shape
    row = pl.BlockSpec((1, d), lambda i: (0, 0))
    stats = jax.ShapeDtypeStruct((t, 2), F32)
    if nxt is None:
        ts = _pick(seq, (256, 128))
        tile = pl.BlockSpec((ts, d), lambda i: (i, 0))
        return pl.pallas_call(
            _ln_kernel, out_shape=jax.ShapeDtypeStruct((t, d), F32), grid=(t // ts,),
            in_specs=[tile, row, row], out_specs=tile,
            compiler_params=_params("parallel"), name="layer_norm",
        )(z, g, b)
    if nxt[0] == "mod":
        ts = _pick(seq, (256, 128))
        tile = pl.BlockSpec((ts, d), lambda i: (i, 0))
        vec = pl.BlockSpec((1, 1, d), lambda i: ((i * ts) // seq, 0, 0))
        return pl.pallas_call(
            _ln_mod_kernel,
            out_shape=(stats, jax.ShapeDtypeStruct((t, d), BF16)),
            grid=(t // ts,),
            in_specs=[tile, row, row, vec, vec],
            out_specs=(pl.BlockSpec((ts, 2), lambda i: (i, 0)), tile),
            compiler_params=_params("parallel"), name="layer_norm_mod",
        )(z, g, b, nxt[1], nxt[2])
    _, sc, sh, mu = nxt
    nmix = mu.shape[0]
    ts = _pick(seq, (128,))
    tile = pl.BlockSpec((ts, d), lambda i: (i, 0))
    prev = pl.BlockSpec((SUBLANES, d), lambda i: (jnp.maximum(i * (ts // SUBLANES) - 1, 0), 0))
    vec = pl.BlockSpec((1, 1, d), lambda i: ((i * ts) // seq, 0, 0))
    return pl.pallas_call(
        functools.partial(_ln_shift_kernel, ts=ts, seq=seq),
        out_shape=(stats,) + (jax.ShapeDtypeStruct((t, d), BF16),) * nmix,
        grid=(t // ts,),
        in_specs=[tile, prev, row, row, vec, vec, pl.BlockSpec((nmix, d), lambda i: (0, 0))],
        out_specs=(pl.BlockSpec((ts, 2), lambda i: (i, 0)),) + (tile,) * nmix,
        compiler_params=_params("parallel"), name="layer_norm_shift",
    )(z, z, g, b, sc, sh, mu)


def _ml_gate_act(y):
    lane = lax.broadcasted_iota(jnp.int32, y.shape, 1)
    i_pre = ML_GATE_CAP * jnp.tanh(y / ML_GATE_CAP)
    log_f = -_softplus(-y)
    return jnp.where(lane < ML_HEADS, i_pre, log_f)


def _mlstm_kernel(q_ref, k_ref, v_ref, o_ref, icol_ref, fcol_ref, irow_ref, frow_ref, ng_ref, wside_ref,
                  out_ref, wside16_ref, c_sc, n_sc, m_sc, *, heads):
    L = ML_CHUNK
    _side_cast_step(wside_ref, wside16_ref)

    @pl.when(pl.program_id(2) == 0)
    def _():
        c_sc[...] = jnp.zeros_like(c_sc)
        n_sc[...] = jnp.zeros_like(n_sc)
        m_sc[...] = jnp.zeros_like(m_sc)

    dqk = q_ref.shape[1] // heads
    dv = v_ref.shape[1] // heads
    k_scale = dqk ** -0.5
    r_idx = lax.broadcasted_iota(jnp.int32, (L, L), 0)
    c_idx = lax.broadcasted_iota(jnp.int32, (L, L), 1)
    causal = r_idx >= c_idx
    anti = r_idx <= c_idx

    hs = range(heads)
    q = [q_ref[:, h * dqk:(h + 1) * dqk] for h in hs]
    k = [k_ref[:, h * dqk:(h + 1) * dqk] for h in hs]
    v = [v_ref[:, h * dv:(h + 1) * dv] for h in hs]
    i_col, f_col = [icol_ref[h] for h in hs], [fcol_ref[h] for h in hs]
    i_row, f_row = [irow_ref[h] for h in hs], [frow_ref[h] for h in hs]
    c_st, n_st, m_st = [c_sc[h] for h in hs], [n_sc[h] for h in hs], [m_sc[h] for h in hs]

    qk = [_dot_nt(q[h], k[h]) for h in hs]
    q_c = [_dot(q[h], c_st[h].astype(BF16)) for h in hs]
    g_col = [jnp.sum(jnp.where(causal, f_row[h], 0.0), axis=1, keepdims=True) for h in hs]
    g_row = [jnp.sum(jnp.where(anti, f_col[h], 0.0), axis=0, keepdims=True) for h in hs]
    g_last = [jnp.sum(f_row[h], axis=1, keepdims=True) for h in hs]
    log_d = [jnp.where(causal, g_col[h] - g_row[h] + i_row[h], -jnp.inf) for h in hs]
    log_inter = [g_col[h] + m_st[h] for h in hs]
    m_row = [jnp.maximum(jnp.max(log_d[h], axis=1, keepdims=True), log_inter[h]) for h in hs]
    scores = [qk[h] * k_scale * jnp.exp(log_d[h] - m_row[h]) for h in hs]
    inter = [jnp.exp(log_inter[h] - m_row[h]) for h in hs]
    num = [_dot(scores[h].astype(BF16), v[h]) + inter[h] * q_c[h] for h in hs]
    q_n = [jnp.sum(q[h].astype(F32) * n_st[h], axis=1, keepdims=True) for h in hs]
    den = [jnp.sum(scores[h], axis=1, keepdims=True) + inter[h] * q_n[h] for h in hs]
    for h in hs:
        hid = num[h] / jnp.maximum(jnp.abs(den[h]), jnp.exp(-m_row[h]))
        hid = hid * lax.rsqrt(jnp.mean(hid * hid, axis=1, keepdims=True) + ML_NORM_EPS)
        sl = slice(h * dv, (h + 1) * dv)
        out_ref[:, sl] = (hid * ng_ref[:, sl] * _sigmoid(o_ref[:, sl].astype(F32))).astype(out_ref.dtype)

    log_w = [g_last[h] - g_col[h] + i_col[h] for h in hs]
    m_new = [jnp.maximum(g_last[h] + m_st[h], jnp.max(log_w[h], axis=0, keepdims=True)) for h in hs]
    wk = [k[h].astype(F32) * (k_scale * jnp.exp(log_w[h] - m_new[h])) for h in hs]
    kv = [_dot_tn(wk[h].astype(BF16), v[h]) for h in hs]
    for h in hs:
        decay = jnp.exp(g_last[h] + m_st[h] - m_new[h])
        c_sc[h] = decay * c_st[h] + kv[h]
        n_sc[h] = decay * n_st[h] + jnp.sum(wk[h], axis=0, keepdims=True)
        m_sc[h] = m_new[h]


def _mlstm(proj, gates, norm_g, w_side, layer, batch, seq):
    t = proj.shape[0]
    H, L = ML_HEADS, ML_CHUNK
    G = 8
    dv = norm_g.shape[1] // H
    dqk = dv // 2
    nc = seq // L
    ng = H // G
    g = gates[:, :2 * H].reshape(batch, seq, 2 * H).transpose(0, 2, 1)
    i_g, f_g = g[:, :H], g[:, H:]
    col = lambda a: a.reshape(batch, H, seq, 1)
    rowv = lambda a: a.reshape(batch, H, nc, 1, L)
    col_spec = pl.BlockSpec((None, G, L, 1), lambda b, h, c: (b, h, c, 0))
    row_spec = pl.BlockSpec((None, G, None, 1, L), lambda b, h, c: (b, h, c, 0, 0))
    tok = lambda b, c: b * nc + c
    side_in, side_out, side_shape = _side_cast(w_side, layer, batch * ng * nc, lambda b, h, c: (b * ng + h) * nc + c)
    return pl.pallas_call(
        functools.partial(_mlstm_kernel, heads=G),
        out_shape=(jax.ShapeDtypeStruct((t, H * dv), BF16), side_shape),
        grid=(batch, ng, nc),
        in_specs=[pl.BlockSpec((L, G * dqk), lambda b, h, c: (tok(b, c), h)),
                  pl.BlockSpec((L, G * dqk), lambda b, h, c: (tok(b, c), ng + h)),
                  pl.BlockSpec((L, G * dv), lambda b, h, c: (tok(b, c), ng + h)),
                  pl.BlockSpec((L, G * dv), lambda b, h, c: (tok(b, c), 2 * ng + h)),
                  col_spec, col_spec, row_spec, row_spec,
                  pl.BlockSpec((1, G * dv), lambda b, h, c: (0, h)),
                  side_in],
        out_specs=(pl.BlockSpec((L, G * dv), lambda b, h, c: (tok(b, c), h)), side_out),
        scratch_shapes=[pltpu.VMEM((G, dqk, dv), F32), pltpu.VMEM((G, 1, dqk), F32), pltpu.VMEM((G, 1, 1), F32)],
        compiler_params=_params("parallel", "parallel", "arbitrary"),
        name="mlstm_chunk",
    )(proj, proj, proj, proj, col(i_g), col(f_g), rowv(i_g), rowv(f_g), norm_g, w_side)


def _rwkv_kernel(r_ref, k_ref, v_ref, lw_ref, la_ref, lgate_ref, w2_ref, a2_ref, g2_ref, w0_ref, a0_ref,
                 kk_ref, ka_ref, rk_ref, lg_ref, lb_ref, wside_ref, o_ref, wside16_ref, s_sc, *, heads):
    L, N = RW_CHUNK, RW_HEAD
    _side_cast_step(wside_ref, wside16_ref)
    gw = heads * N

    @pl.when(pl.program_id(2) == 0)
    def _():
        s_sc[...] = jnp.zeros_like(s_sc)

    P = 2 * L
    pairs = heads // 2

    r, k, v = r_ref[...].astype(F32), k_ref[...].astype(F32), v_ref[...].astype(F32)
    w_pre = w0_ref[...] + _dot(lw_ref[...], w2_ref[...])
    a_pre = a0_ref[...] + _dot(la_ref[...], a2_ref[...])
    gate = _dot(lgate_ref[...], g2_ref[...])
    log_decay = -jnp.exp(-_softplus(-w_pre) - 0.5)
    a = _sigmoid(a_pre)

    sw = min(gw, 256)
    li = lax.broadcasted_iota(jnp.int32, (sw, sw), 0) // N
    lj = lax.broadcasted_iota(jnp.int32, (sw, sw), 1) // N
    head_ones = jnp.where(li == lj, 1.0, 0.0).astype(BF16)

    def head_sum(x, split):
        parts = split(x)
        cols = []
        for c0 in range(0, gw, sw):
            acc = None
            for p in parts:
                d = _dot(p[:, c0:c0 + sw], head_ones)
                acc = d if acc is None else acc + d
            cols.append(acc)
        return cols[0] if len(cols) == 1 else jnp.concatenate(cols, axis=1)

    one_pass = lambda x: (x.astype(BF16),)

    t_i = lax.broadcasted_iota(jnp.int32, (L, L), 0)
    t_j = lax.broadcasted_iota(jnp.int32, (L, L), 1)
    tri = jnp.where(t_i >= t_j, 1.0, 0.0).astype(BF16)

    kkr = k * kk_ref[...]
    k = k * (1.0 + (a - 1.0) * ka_ref[...])
    sums = head_sum(jnp.concatenate([kkr * kkr, r * k * rk_ref[...]], axis=0), one_pass)
    kk = kkr * lax.rsqrt(jnp.maximum(sums[:L], 1e-24))
    bonus = sums[L:] * v

    cl = _dot_01(tri, log_decay)
    cl_last = cl[L - 1:L, :]
    gam = jnp.exp(cl)
    inv_gam = jnp.exp(-cl)
    gam_end = jnp.exp(cl_last)
    to_end = gam_end * inv_gam
    kka = kk * a
    r_hat = r * gam
    a_hat = -kk * jnp.exp(cl - log_decay)
    b_hat = kka * inv_gam
    k_hat = k * inv_gam
    b_end = kka * to_end
    k_end = k * to_end

    p_i = lax.broadcasted_iota(jnp.int32, (P, P), 0)
    p_j = lax.broadcasted_iota(jnp.int32, (P, P), 1)
    same = p_i // L == p_j // L
    strict = jnp.logical_and(same, p_i > p_j)
    incl = jnp.logical_and(same, p_i >= p_j)
    eye = jnp.where(p_i == p_j, 1.0, 0.0)
    levels = []
    s = 2
    while s < L:
        levels.append((s, jnp.logical_and(p_i // (2 * s) == p_j // (2 * s),
                                          jnp.logical_and(p_i % (2 * s) >= s, p_j % (2 * s) < s))))
        s *= 2
    own = (lax.broadcasted_iota(jnp.int32, (P, 2 * N), 0) // L
           == lax.broadcasted_iota(jnp.int32, (P, 2 * N), 1) // N)

    def stack(x):
        return jnp.where(own, jnp.concatenate([x, x], axis=0), 0.0).astype(BF16)

    rng = range(pairs)
    sls = [slice(p * 2 * N, (p + 1) * 2 * N) for p in rng]
    st = [s_sc[p] for p in rng]
    ar_s = [jnp.concatenate([stack(a_hat[:, sl]), stack(r_hat[:, sl])], axis=0) for sl in sls]
    v_s = [stack(v[:, sl]) for sl in sls]
    m4 = [_dot_nt(ar_s[p], jnp.concatenate([stack(b_hat[:, sls[p]]), stack(k_hat[:, sls[p]])], axis=0))
          for p in rng]
    a_ab = [jnp.where(strict, m[:P, :P], 0.0) for m in m4]
    a_ak = [jnp.where(strict, m[:P, P:], 0.0).astype(BF16) for m in m4]
    a_r = [jnp.concatenate([jnp.where(incl, m[P:, :P], 0.0), jnp.where(incl, m[P:, P:], 0.0)], axis=1).astype(BF16)
           for m in m4]
    from_state = [_dot_nt(ar_s[p], st[p].astype(BF16)) for p in rng]
    rhs_u = [from_state[p][:P] + _dot(a_ak[p], v_s[p]) for p in rng]
    x = [eye + jnp.where(p_i // 2 == p_j // 2, m, 0.0) for m in a_ab]
    for s, blk in levels:
        xb = [xp.astype(BF16) for xp in x]
        m_s = [jnp.where(blk, a_ab[p], 0.0).astype(BF16) for p in rng]
        if s % SUBLANES:
            half = [_dot(xb[p], m_s[p]).astype(BF16) for p in rng]
            x = [x[p] + _dot(half[p], xb[p]) for p in rng]
        else:
            lower = [jnp.concatenate([xp[b + s:b + 2 * s] for b in range(0, P, 2 * s)], axis=0) for xp in x]
            half = [_dot(lower[p].astype(BF16), m_s[p]).astype(BF16) for p in rng]
            lower = [lower[p] + _dot(half[p], xb[p]) for p in rng]
            x = [jnp.concatenate(
                [blk_rows for i, b in enumerate(range(0, P, 2 * s))
                 for blk_rows in (x[p][b:b + s], lower[p][i * s:(i + 1) * s])], axis=0) for p in rng]
    su = [_dot(x[p].astype(BF16), rhs_u[p].astype(BF16)) for p in rng]
    su_v = [jnp.concatenate([su[p].astype(BF16), v_s[p]], axis=0) for p in rng]
    y_s = [from_state[p][P:] + _dot(a_r[p], su_v[p]) for p in rng]
    for p in rng:
        be = jnp.concatenate([stack(b_end[:, sls[p]]), stack(k_end[:, sls[p]])], axis=0)
        s_sc[p] = st[p] * gam_end[:, sls[p]] + _dot_tn(su_v[p], be)
    ys = [yp[:L] + yp[L:] for yp in y_s]
    y = ys[0] if pairs == 1 else jnp.concatenate(ys, axis=1)

    inv_n = 1.0 / N
    mean = head_sum(y, one_pass) * inv_n
    yc = y - mean
    var = head_sum(yc * yc, one_pass) * inv_n
    yn = yc * lax.rsqrt(var + RW_GN_EPS) * lg_ref[...] + lb_ref[...]
    o_ref[...] = ((yn + bonus) * gate).astype(o_ref.dtype)


def _rwkv(r, k, v, lw, la, lgate, w2, a2, g2, layer, w0, a0, k_k, k_a, r_k, lnx_g, lnx_b, w_side, batch, seq):
    t, d = r.shape
    L = RW_CHUNK
    heads = min(64, d // RW_HEAD)
    gw = heads * RW_HEAD
    nc = seq // L
    nh = d // gw
    tile = pl.BlockSpec((L, gw), lambda b, h, c: (b * nc + c, h))
    row = pl.BlockSpec((1, gw), lambda b, h, c: (0, h))
    lora = lambda f: pl.BlockSpec((L, f.shape[1]), lambda b, h, c: (b * nc + c, 0))
    lora_w = lambda w: pl.BlockSpec((None, w.shape[1], gw), lambda b, h, c: (layer, 0, h))
    side_in, side_out, side_shape = _side_cast(w_side, layer, batch * nh * nc, lambda b, h, c: (b * nh + h) * nc + c)
    return pl.pallas_call(
        functools.partial(_rwkv_kernel, heads=heads),
        out_shape=(jax.ShapeDtypeStruct((t, d), BF16), side_shape),
        grid=(batch, nh, nc),
        in_specs=([tile] * 3 + [lora(lw), lora(la), lora(lgate), lora_w(w2), lora_w(a2), lora_w(g2)] + [row] * 7
                  + [side_in]),
        out_specs=(tile, side_out),
        scratch_shapes=[pltpu.VMEM((heads // 2, 2 * RW_HEAD, 2 * RW_HEAD), F32)],
        compiler_params=_params("parallel", "parallel", "arbitrary"),
        name="rwkv7_chunk",
    )(r, k, v, lw, la, lgate, w2, a2, g2, w0, a0, k_k, k_a, r_k, lnx_g, lnx_b, w_side)


def kernel(x, c, ada_w, ada_b, mix_ln_g, mix_ln_b, ffn_ln_g, ffn_ln_b, ffn_w_in, ffn_w_out, ml_w_in, ml_b_i, ml_b_f, ml_norm_g, ml_w_out, rw_mu, rw_w_r, rw_w_k, rw_w_v, rw_w0, rw_w1, rw_w2, rw_a0, rw_a1, rw_a2, rw_g1, rw_g2, rw_k_k, rw_k_a, rw_r_k, rw_lnx_g, rw_lnx_b, rw_w_o):
    batch, seq, d = x.shape
    depth = ada_w.shape[0]
    t = batch * seq
    alpha = (2 * depth) ** 0.25
    hidden = ffn_w_out.shape[1]
    n_mixers = 2

    mod = _ada(c, ada_w, ada_b).reshape(depth, batch, 6, 1, d)
    row = lambda p: p.reshape(1, -1)
    zero_bias = lambda n: jnp.zeros((1, n), F32)

    res = (x.reshape(t, d),)
    u = None
    for layer in range(depth):
        sh_m, sc_m, gt_m, sh_f, sc_f, gt_f = [mod[layer, :, i] for i in range(6)]
        j = layer // n_mixers
        g_m, b_m = row(mix_ln_g[layer]), row(mix_ln_b[layer])
        if layer % n_mixers == 0:
            qkvo = ml_w_in.shape[2] - 2 * ML_HEADS
            w_gate = jnp.pad(ml_w_in[j:j + 1, :, qkvo:], ((0, 0), (0, 0), (0, LANES - 2 * ML_HEADS)))
            b_gate = jnp.pad(jnp.concatenate([ml_b_i[j], ml_b_f[j]]), (0, LANES - 2 * ML_HEADS)).reshape(1, LANES)
            if u is None:
                u, gates = _modulate(res[0], sc_m, sh_m, seq, w_gate, b_gate, _ml_gate_act)
            else:
                gates = _mm(u, w_gate, 0, LANES, b_gate, _ml_gate_act, F32, "mlstm_gates")
            proj = _mm(u, jnp.swapaxes(ml_w_in, 1, 2), j, qkvo, zero_bias(qkvo), _identity, BF16, "mlstm_in",
                       w_is_nk=True)
            y, w_o = _mlstm(proj, gates, row(ml_norm_g[j]), ml_w_out, j, batch, seq)
            z = _mm_resid(y, w_o, res, gt_m, alpha, seq, "mlstm_out")
        else:
            xr, xw, xk, xv, xa, xg = u
            nb = zero_bias
            r = _mm(xr, rw_w_r, j, d, nb(d), _identity, BF16, "rwkv_r")
            k = _mm(xk, rw_w_k, j, d, nb(d), _identity, BF16, "rwkv_k")
            v = _mm(xv, rw_w_v, j, d, nb(d), _identity, BF16, "rwkv_v")
            n_w, n_a, n_g = rw_w1.shape[2], rw_a1.shape[2], rw_g1.shape[2]
            lw = _mm(xw, rw_w1, j, n_w, nb(n_w), jnp.tanh, BF16, "rwkv_w1")
            la = _mm(xa, rw_a1, j, n_a, nb(n_a), _identity, BF16, "rwkv_a1")
            lg = _mm(xg, rw_g1, j, n_g, nb(n_g), _sigmoid, BF16, "rwkv_g1")
            y, w_o = _rwkv(r, k, v, lw, la, lg, rw_w2.astype(BF16), rw_a2.astype(BF16), rw_g2.astype(BF16), j,
                           row(rw_w0[j]), row(rw_a0[j]), row(rw_k_k[j]), row(rw_k_a[j]), row(rw_r_k[j]),
                           row(rw_lnx_g[j]), row(rw_lnx_b[j]), rw_w_o, batch, seq)
            z = _mm_resid(y, w_o, res, gt_m, alpha, seq, "rwkv_out")
        stats, u = _layer_norm(z, g_m, b_m, seq, ("mod", sc_f, sh_f))
        res = (z, stats, g_m, b_m)
        hid, w_down = _mm_swiglu(u, ffn_w_in, ffn_w_out, layer, hidden)
        z = _mm_resid(hid, w_down, res, gt_f, alpha, seq, "ffn_out")
        g_f, b_f = row(ffn_ln_g[layer]), row(ffn_ln_b[layer])
        if layer + 1 == depth:
            return _layer_norm(z, g_f, b_f, seq).reshape(batch, seq, d)
        nsh, nsc = mod[layer + 1, :, 0], mod[layer + 1, :, 1]
        if (layer + 1) % n_mixers == 0:
            stats, u = _layer_norm(z, g_f, b_f, seq, ("mod", nsc, nsh))
        else:
            out = _layer_norm(z, g_f, b_f, seq, ("shift", nsc, nsh, rw_mu[(layer + 1) // n_mixers]))
            stats, u = out[0], out[1:]
        res = (z, stats, g_f, b_f)
```

```python
import functools

import jax
import jax.numpy as jnp
from jax import lax
from jax.experimental import pallas as pl
from jax.experimental.pallas import tpu as pltpu

F32 = jnp.float32
BF16 = jnp.bfloat16

ML_HEADS = 8
ML_CHUNK = 128
ML_GATE_CAP = 15.0
ML_NORM_EPS = 1e-6
RW_HEAD = 64
RW_CHUNK = 64
RW_GN_EPS = 64e-5
LN_EPS = 1e-5
LANES = 128
SUBLANES = 8
VMEM_LIMIT_BYTES = 56 * 1024 * 1024
TILE_VMEM_BUDGET_BYTES = 52 * 1024 * 1024
WIDE_VMEM_LIMIT_BYTES = 60 * 1024 * 1024
SHORT_K = 4096


def _params(*semantics, vmem=VMEM_LIMIT_BYTES):
    return pltpu.CompilerParams(dimension_semantics=semantics, vmem_limit_bytes=vmem)


def _pick(n, candidates):
    for c in candidates:
        if n % c == 0:
            return c
    return n


def _dot(a, b):
    return jnp.dot(a, b, preferred_element_type=F32)


def _dot_nt(a, b):
    return lax.dot_general(a, b, (((1,), (1,)), ((), ())), preferred_element_type=F32)


def _dot_tn(a, b):
    return lax.dot_general(a, b, (((0,), (0,)), ((), ())), preferred_element_type=F32)


def _split3(x):
    hi = x.astype(BF16)
    r = x - hi.astype(F32)
    mid = r.astype(BF16)
    lo = (r - mid.astype(F32)).astype(BF16)
    return hi, mid, lo


def _dot_01(m, x):
    hi, mid, lo = _split3(x)
    return _dot(m, hi) + (_dot(m, mid) + _dot(m, lo))


def _sigmoid(x):
    return 1.0 / (1.0 + jnp.exp(-x))


def _softplus(x):
    return jnp.maximum(x, 0.0) + jnp.log1p(jnp.exp(-jnp.abs(x)))


def _ada_kernel(c_ref, w_ref, b_ref, o_ref):
    c = c_ref[...]
    c_act = (c * _sigmoid(c)).astype(BF16)
    o_ref[0] = _dot(c_act, w_ref[0].astype(BF16)) + b_ref[0]


def _ada(c, ada_w, ada_b):
    depth, d, n = ada_w.shape
    b = c.shape[0]
    tn = _pick(n, (1024, 512, 256, 128))
    return pl.pallas_call(
        _ada_kernel,
        out_shape=jax.ShapeDtypeStruct((depth, b, n), F32),
        grid=(depth, n // tn),
        in_specs=[pl.BlockSpec((b, d), lambda l, j: (0, 0)),
                  pl.BlockSpec((1, d, tn), lambda l, j: (l, 0, j)),
                  pl.BlockSpec((1, 1, tn), lambda l, j: (l, 0, j))],
        out_specs=pl.BlockSpec((1, b, tn), lambda l, j: (l, 0, j)),
        compiler_params=_params("parallel", "parallel"),
        name="ada_mod",
    )(c, ada_w, ada_b.reshape(depth, 1, n))


def _modulate_kernel(x_ref, sc_ref, sh_ref, w_ref, b_ref, o_ref, g_ref, *, act):
    u = (x_ref[...] * (1.0 + sc_ref[0]) + sh_ref[0]).astype(o_ref.dtype)
    o_ref[...] = u
    g_ref[...] = act(_dot(u, w_ref[0].astype(BF16)) + b_ref[...])


def _modulate(x2, sc, sh, seq, w, bias, act):
    t, d = x2.shape
    n = w.shape[2]
    ts = _pick(seq, (512, 256, 128))
    vec = pl.BlockSpec((1, 1, d), lambda i: ((i * ts) // seq, 0, 0))
    return pl.pallas_call(
        functools.partial(_modulate_kernel, act=act),
        out_shape=(jax.ShapeDtypeStruct((t, d), BF16), jax.ShapeDtypeStruct((t, n), F32)),
        grid=(t // ts,),
        in_specs=[pl.BlockSpec((ts, d), lambda i: (i, 0)), vec, vec,
                  pl.BlockSpec((1, d, n), lambda i: (0, 0, 0)), pl.BlockSpec((1, n), lambda i: (0, 0))],
        out_specs=(pl.BlockSpec((ts, d), lambda i: (i, 0)), pl.BlockSpec((ts, n), lambda i: (i, 0))),
        compiler_params=_params("parallel"),
        name="modulate",
    )(x2, sc, sh, w, bias)


def _mm_kernel(a_ref, w_ref, b_ref, o_ref, *, act, w_is_nk):
    dot = _dot_nt if w_is_nk else _dot
    y = dot(a_ref[...], w_ref[...].astype(BF16)) + b_ref[...]
    o_ref[...] = act(y).astype(o_ref.dtype)


def _mm(a, w, layer, n, bias, act, out_dtype, name, w_is_nk=False):
    t, k = a.shape
    tm = _pick(t, (2048, 1024, 512, 256, 128))
    out_bytes = jnp.dtype(out_dtype).itemsize
    fits = lambda c: 2 * (tm * k * 2 + k * c * 4 + tm * c * out_bytes) <= TILE_VMEM_BUDGET_BYTES
    tn = _pick(n, [c for c in (512, 256, 128) if fits(c)])
    if w_is_nk:
        w_spec = pl.BlockSpec((None, tn, k), lambda i, j: (layer, j, 0))
    else:
        w_spec = pl.BlockSpec((None, k, tn), lambda i, j: (layer, 0, j))
    return pl.pallas_call(
        functools.partial(_mm_kernel, act=act, w_is_nk=w_is_nk),
        out_shape=jax.ShapeDtypeStruct((t, n), out_dtype),
        grid=(t // tm, n // tn),
        in_specs=[pl.BlockSpec((tm, k), lambda i, j: (i, 0)),
                  w_spec,
                  pl.BlockSpec((1, tn), lambda i, j: (0, j))],
        out_specs=pl.BlockSpec((tm, tn), lambda i, j: (i, j)),
        compiler_params=_params("parallel", "parallel"),
        name=name,
    )(a, w, bias)


def _identity(y):
    return y


def _side_cast(w, layer, steps, step_of):
    k, n = w.shape[1:]
    rows = k // steps
    assert rows * steps == k and rows % (2 * SUBLANES) == 0, (w.shape, steps)
    in_spec = pl.BlockSpec((None, rows, n), lambda *g: (layer, step_of(*g), 0))
    out_spec = pl.BlockSpec((rows, n), lambda *g: (step_of(*g), 0))
    return in_spec, out_spec, jax.ShapeDtypeStruct((k, n), BF16)


def _side_cast_step(src_ref, dst_ref):
    dst_ref[...] = src_ref[...].astype(dst_ref.dtype)


def _swiglu_kernel(a_ref, wg_ref, wu_ref, wo_ref, o_ref, wo16_ref):
    a = a_ref[...]
    gate = _dot(a, wg_ref[...].astype(BF16))
    up = _dot(a, wu_ref[...].astype(BF16))
    o_ref[...] = (gate * _sigmoid(gate) * up).astype(o_ref.dtype)
    _side_cast_step(wo_ref, wo16_ref)


def _mm_swiglu(a, w_in, w_out, layer, hidden):
    t, k = a.shape
    tm = _pick(t, (2048, 1024, 512, 256, 128))
    tn = _pick(hidden, (256, 128))
    nj = hidden // tn
    side_in, side_out, side_shape = _side_cast(w_out, layer, (t // tm) * nj, lambda i, j: i * nj + j)
    return pl.pallas_call(
        _swiglu_kernel,
        out_shape=(jax.ShapeDtypeStruct((t, hidden), BF16), side_shape),
        grid=(t // tm, nj),
        in_specs=[pl.BlockSpec((tm, k), lambda i, j: (i, 0)),
                  pl.BlockSpec((None, k, tn), lambda i, j: (layer, 0, j)),
                  pl.BlockSpec((None, k, tn), lambda i, j: (layer, 0, j + nj)),
                  side_in],
        out_specs=(pl.BlockSpec((tm, tn), lambda i, j: (i, j)), side_out),
        compiler_params=_params("parallel", "parallel", vmem=WIDE_VMEM_LIMIT_BYTES),
        name="ffn_in_swiglu",
    )(a, w_in, w_in, w_out)


def _ln_apply(z, mu, rstd, g, b):
    return (z - mu) * rstd * g + b


def _resid_kernel(a_ref, w_ref, x_ref, gt_ref, *rest, alpha, normed):
    if normed:
        st_ref, g_ref, b_ref, o_ref = rest
        st = st_ref[...]
        x = _ln_apply(x_ref[...], st[:, 0:1], st[:, 1:2], g_ref[...], b_ref[...])
    else:
        o_ref, = rest
        x = x_ref[...]
    o_ref[...] = alpha * x + gt_ref[0] * _dot(a_ref[...], w_ref[...])


def _mm_resid(a, w, res, gt, alpha, seq, name):
    t, k = a.shape
    n = w.shape[1]
    normed = len(res) > 1
    sizes = (1024, 512, 256, 128) if k <= SHORT_K else (512, 256, 128)
    tm, tn = _pick(min(t, seq), sizes), _pick(n, sizes)
    in_specs = [pl.BlockSpec((tm, k), lambda i, j: (i, 0)),
                pl.BlockSpec((k, tn), lambda i, j: (0, j)),
                pl.BlockSpec((tm, tn), lambda i, j: (i, j)),
                pl.BlockSpec((1, 1, tn), lambda i, j: ((i * tm) // seq, 0, j))]
    operands = [a, w, res[0], gt]
    if normed:
        in_specs += [pl.BlockSpec((tm, 2), lambda i, j: (i, 0)),
                     pl.BlockSpec((1, tn), lambda i, j: (0, j)),
                     pl.BlockSpec((1, tn), lambda i, j: (0, j))]
        operands += list(res[1:])
    return pl.pallas_call(
        functools.partial(_resid_kernel, alpha=alpha, normed=normed),
        out_shape=jax.ShapeDtypeStruct((t, n), F32),
        grid=(t // tm, n // tn),
        in_specs=in_specs,
        out_specs=pl.BlockSpec((tm, tn), lambda i, j: (i, j)),
        compiler_params=_params("parallel", "parallel", vmem=WIDE_VMEM_LIMIT_BYTES),
        name=name,
    )(*operands)


def _col_chunks(d):
    cw = _pick(d, (512, 256, 128))
    return [slice(c, c + cw) for c in range(0, d, cw)]


def _ln_stats_ref(z_ref):
    d = z_ref.shape[1]
    chunks = _col_chunks(d)
    mu = jnp.sum(sum(z_ref[:, c] for c in chunks), axis=-1, keepdims=True) * (1.0 / d)
    sq = jnp.sum(sum(jnp.square(z_ref[:, c] - mu) for c in chunks), axis=-1, keepdims=True)
    return mu, lax.rsqrt(sq * (1.0 / d) + LN_EPS)


def _ln_stats(z):
    mu = jnp.mean(z, axis=-1, keepdims=True)
    zc = z - mu
    var = jnp.mean(zc * zc, axis=-1, keepdims=True)
    return mu, lax.rsqrt(var + LN_EPS)


def _ln_kernel(z_ref, g_ref, b_ref, x_ref):
    z = z_ref[...]
    mu, rstd = _ln_stats(z)
    x_ref[...] = _ln_apply(z, mu, rstd, g_ref[...], b_ref[...])


def _ln_mod_kernel(z_ref, g_ref, b_ref, sc_ref, sh_ref, st_ref, u_ref):
    z = z_ref[...]
    mu, rstd = _ln_stats(z)
    st_ref[:, 0:1] = mu
    st_ref[:, 1:2] = rstd
    scale = 1.0 + sc_ref[0]
    u_ref[...] = _ln_apply(z, mu, rstd, g_ref[...] * scale, b_ref[...] * scale + sh_ref[0]).astype(u_ref.dtype)


def _ln_shift_kernel(z_ref, zp_ref, g_ref, b_ref, sc_ref, sh_ref, mu_ref, st_ref, *mix_refs, ts, seq):
    mu, rstd = _ln_stats_ref(z_ref)
    st_ref[:, 0:1] = mu
    st_ref[:, 1:2] = rstd
    mu_p, rstd_p = _ln_stats_ref(zp_ref)
    first = (pl.program_id(0) * ts) % seq == 0
    row = lax.broadcasted_iota(jnp.int32, (ts, 1), 0)
    for c in _col_chunks(z_ref.shape[1]):
        scale = 1.0 + sc_ref[0, :, c]
        g, b = g_ref[:, c] * scale, b_ref[:, c] * scale + sh_ref[0, :, c]
        u = _ln_apply(z_ref[:, c], mu, rstd, g, b)
        prev = _ln_apply(zp_ref[:, c], mu_p, rstd_p, g, b)[SUBLANES - 1:SUBLANES, :]
        prev = jnp.where(first, 0.0, prev)
        u_prev = jnp.where(row == 0, prev, pltpu.roll(u, shift=1, axis=0))
        xx = (u_prev - u).astype(BF16)
        u = u.astype(BF16)
        for j, ref in enumerate(mix_refs):
            ref[:, c] = u + xx * mu_ref[j:j + 1, c].astype(BF16)


def _layer_norm(z, g, b, seq, nxt=None):
    t, d = z.shape
    row = pl.BlockSpec((1, d), lambda i: (0, 0))
    stats = jax.ShapeDtypeStruct((t, 2), F32)
    if nxt is None:
        ts = _pick(seq, (256, 128))
        tile = pl.BlockSpec((ts, d), lambda i: (i, 0))
        return pl.pallas_call(
            _ln_kernel, out_shape=jax.ShapeDtypeStruct((t, d), F32), grid=(t // ts,),
            in_specs=[tile, row, row], out_specs=tile,
            compiler_params=_params("parallel"), name="layer_norm",
        )(z, g, b)
    if nxt[0] == "mod":
        ts = _pick(seq, (256, 128))
        tile = pl.BlockSpec((ts, d), lambda i: (i, 0))
        vec = pl.BlockSpec((1, 1, d), lambda i: ((i * ts) // seq, 0, 0))
        return pl.pallas_call(
            _ln_mod_kernel,
            out_shape=(stats, jax.ShapeDtypeStruct((t, d), BF16)),
            grid=(t // ts,),
            in_specs=[tile, row, row, vec, vec],
            out_specs=(pl.BlockSpec((ts, 2), lambda i: (i, 0)), tile),
            compiler_params=_params("parallel"), name="layer_norm_mod",
        )(z, g, b, nxt[1], nxt[2])
    _, sc, sh, mu = nxt
    nmix = mu.shape[0]
    ts = _pick(seq, (128,))
    tile = pl.BlockSpec((ts, d), lambda i: (i, 0))
    prev = pl.BlockSpec((SUBLANES, d), lambda i: (jnp.maximum(i * (ts // SUBLANES) - 1, 0), 0))
    vec = pl.BlockSpec((1, 1, d), lambda i: ((i * ts) // seq, 0, 0))
    return pl.pallas_call(
        functools.partial(_ln_shift_kernel, ts=ts, seq=seq),
        out_shape=(stats,) + (jax.ShapeDtypeStruct((t, d), BF16),) * nmix,
        grid=(t // ts,),
        in_specs=[tile, prev, row, row, vec, vec, pl.BlockSpec((nmix, d), lambda i: (0, 0))],
        out_specs=(pl.BlockSpec((ts, 2), lambda i: (i, 0)),) + (tile,) * nmix,
        compiler_params=_params("parallel"), name="layer_norm_shift",
    )(z, z, g, b, sc, sh, mu)


def _ml_gate_act(y):
    lane = lax.broadcasted_iota(jnp.int32, y.shape, 1)
    i_pre = ML_GATE_CAP * jnp.tanh(y / ML_GATE_CAP)
    log_f = -_softplus(-y)
    return jnp.where(lane < ML_HEADS, i_pre, log_f)


def _mlstm_kernel(q_ref, k_ref, v_ref, o_ref, gate_ref, irow_ref, frow_ref, ng_ref, wside_ref,
                  out_ref, wside16_ref, c_sc, n_sc, m_sc, *, heads):
    L = ML_CHUNK
    _side_cast_step(wside_ref, wside16_ref)

    @pl.when(pl.program_id(2) == 0)
    def _():
        c_sc[...] = jnp.zeros_like(c_sc)
        n_sc[...] = jnp.zeros_like(n_sc)
        m_sc[...] = jnp.zeros_like(m_sc)

    dqk = q_ref.shape[1] // heads
    dv = v_ref.shape[1] // heads
    k_scale = dqk ** -0.5
    r_idx = lax.broadcasted_iota(jnp.int32, (L, L), 0)
    c_idx = lax.broadcasted_iota(jnp.int32, (L, L), 1)
    causal = r_idx >= c_idx
    anti = r_idx <= c_idx

    hs = range(heads)
    q = [q_ref[:, h * dqk:(h + 1) * dqk] for h in hs]
    k = [k_ref[:, h * dqk:(h + 1) * dqk] for h in hs]
    v = [v_ref[:, h * dv:(h + 1) * dv] for h in hs]
    i_col = [gate_ref[:, h:h + 1] for h in hs]
    f_col = [gate_ref[:, heads + h:heads + h + 1] for h in hs]
    i_row, f_row = [irow_ref[h] for h in hs], [frow_ref[h] for h in hs]
    c_st, n_st, m_st = [c_sc[h] for h in hs], [n_sc[h] for h in hs], [m_sc[h] for h in hs]

    qk = [_dot_nt(q[h], k[h]) for h in hs]
    q_c = [_dot(q[h], c_st[h].astype(BF16)) for h in hs]
    g_col = [jnp.sum(jnp.where(causal, f_row[h], 0.0), axis=1, keepdims=True) for h in hs]
    g_row = [jnp.sum(jnp.where(anti, f_col[h], 0.0), axis=0, keepdims=True) for h in hs]
    g_last = [jnp.sum(f_row[h], axis=1, keepdims=True) for h in hs]
    log_d = [jnp.where(causal, g_col[h] - g_row[h] + i_row[h], -jnp.inf) for h in hs]
    log_inter = [g_col[h] + m_st[h] for h in hs]
    m_row = [jnp.maximum(jnp.max(log_d[h], axis=1, keepdims=True), log_inter[h]) for h in hs]
    scores = [qk[h] * k_scale * jnp.exp(log_d[h] - m_row[h]) for h in hs]
    inter = [jnp.exp(log_inter[h] - m_row[h]) for h in hs]
    num = [_dot(scores[h].astype(BF16), v[h]) + inter[h] * q_c[h] for h in hs]
    q_n = [jnp.sum(q[h].astype(F32) * n_st[h], axis=1, keepdims=True) for h in hs]
    den = [jnp.sum(scores[h], axis=1, keepdims=True) + inter[h] * q_n[h] for h in hs]
    for h in hs:
        hid = num[h] / jnp.maximum(jnp.abs(den[h]), jnp.exp(-m_row[h]))
        hid = hid * lax.rsqrt(jnp.mean(hid * hid, axis=1, keepdims=True) + ML_NORM_EPS)
        sl = slice(h * dv, (h + 1) * dv)
        out_ref[:, sl] = (hid * ng_ref[:, sl] * _sigmoid(o_ref[:, sl].astype(F32))).astype(out_ref.dtype)

    log_w = [g_last[h] - g_col[h] + i_col[h] for h in hs]
    m_new = [jnp.maximum(g_last[h] + m_st[h], jnp.max(log_w[h], axis=0, keepdims=True)) for h in hs]
    wk = [k[h].astype(F32) * (k_scale * jnp.exp(log_w[h] - m_new[h])) for h in hs]
    kv = [_dot_tn(wk[h].astype(BF16), v[h]) for h in hs]
    for h in hs:
        decay = jnp.exp(g_last[h] + m_st[h] - m_new[h])
        c_sc[h] = decay * c_st[h] + kv[h]
        n_sc[h] = decay * n_st[h] + jnp.sum(wk[h], axis=0, keepdims=True)
        m_sc[h] = m_new[h]


def _mlstm(proj, gates, norm_g, w_side, layer, batch, seq):
    t = proj.shape[0]
    H, L = ML_HEADS, ML_CHUNK
    G = H
    dv = norm_g.shape[1] // H
    dqk = dv // 2
    nc = seq // L
    ng = H // G
    g = gates[:, :2 * H].reshape(batch, seq, 2 * H).transpose(0, 2, 1)
    i_g, f_g = g[:, :H], g[:, H:]
    rowv = lambda a: a.reshape(batch, H, nc, 1, L)
    col_spec = pl.BlockSpec((L, gates.shape[1]), lambda b, h, c: (b * nc + c, 0))
    row_spec = pl.BlockSpec((None, G, None, 1, L), lambda b, h, c: (b, h, c, 0, 0))
    tok = lambda b, c: b * nc + c
    side_in, side_out, side_shape = _side_cast(w_side, layer, batch * ng * nc, lambda b, h, c: (b * ng + h) * nc + c)
    return pl.pallas_call(
        functools.partial(_mlstm_kernel, heads=G),
        out_shape=(jax.ShapeDtypeStruct((t, H * dv), BF16), side_shape),
        grid=(batch, ng, nc),
        in_specs=[pl.BlockSpec((L, G * dqk), lambda b, h, c: (tok(b, c), h)),
                  pl.BlockSpec((L, G * dqk), lambda b, h, c: (tok(b, c), ng + h)),
                  pl.BlockSpec((L, G * dv), lambda b, h, c: (tok(b, c), ng + h)),
                  pl.BlockSpec((L, G * dv), lambda b, h, c: (tok(b, c), 2 * ng + h)),
                  col_spec, row_spec, row_spec,
                  pl.BlockSpec((1, G * dv), lambda b, h, c: (0, h)),
                  side_in],
        out_specs=(pl.BlockSpec((L, G * dv), lambda b, h, c: (tok(b, c), h)), side_out),
        scratch_shapes=[pltpu.VMEM((G, dqk, dv), F32), pltpu.VMEM((G, 1, dqk), F32), pltpu.VMEM((G, 1, 1), F32)],
        compiler_params=_params("parallel", "parallel", "arbitrary"),
        name="mlstm_chunk",
    )(proj, proj, proj, proj, gates, rowv(i_g), rowv(f_g), norm_g, w_side)


def _rwkv_kernel(r_ref, k_ref, v_ref, lw_ref, la_ref, lgate_ref, w2_ref, a2_ref, g2_ref, w0_ref, a0_ref,
                 kk_ref, ka_ref, rk_ref, lg_ref, lb_ref, wside_ref, o_ref, wside16_ref, s_sc, *, heads):
    L, N = RW_CHUNK, RW_HEAD
    _side_cast_step(wside_ref, wside16_ref)
    gw = heads * N

    @pl.when(pl.program_id(2) == 0)
    def _():
        s_sc[...] = jnp.zeros_like(s_sc)

    P = 2 * L
    pairs = heads // 2

    r, k, v = r_ref[...].astype(F32), k_ref[...].astype(F32), v_ref[...].astype(F32)
    w_pre = w0_ref[...] + _dot(lw_ref[...], w2_ref[...])
    a_pre = a0_ref[...] + _dot(la_ref[...], a2_ref[...])
    gate = _dot(lgate_ref[...], g2_ref[...])
    log_decay = -jnp.exp(-_softplus(-w_pre) - 0.5)
    a = _sigmoid(a_pre)

    sw = min(gw, 256)
    li = lax.broadcasted_iota(jnp.int32, (sw, sw), 0) // N
    lj = lax.broadcasted_iota(jnp.int32, (sw, sw), 1) // N
    head_ones = jnp.where(li == lj, 1.0, 0.0).astype(BF16)

    def head_sum(x, split):
        parts = split(x)
        cols = []
        for c0 in range(0, gw, sw):
            acc = None
            for p in parts:
                d = _dot(p[:, c0:c0 + sw], head_ones)
                acc = d if acc is None else acc + d
            cols.append(acc)
        return cols[0] if len(cols) == 1 else jnp.concatenate(cols, axis=1)

    one_pass = lambda x: (x.astype(BF16),)

    t_i = lax.broadcasted_iota(jnp.int32, (L, L), 0)
    t_j = lax.broadcasted_iota(jnp.int32, (L, L), 1)
    tri = jnp.where(t_i >= t_j, 1.0, 0.0).astype(BF16)

    kkr = k * kk_ref[...]
    k = k * (1.0 + (a - 1.0) * ka_ref[...])
    sums = head_sum(jnp.concatenate([kkr * kkr, r * k * rk_ref[...]], axis=0), one_pass)
    kk = kkr * lax.rsqrt(jnp.maximum(sums[:L], 1e-24))
    bonus = sums[L:] * v

    cl = _dot_01(tri, log_decay)
    cl_last = cl[L - 1:L, :]
    gam = jnp.exp(cl)
    inv_gam = jnp.exp(-cl)
    gam_end = jnp.exp(cl_last)
    to_end = gam_end * inv_gam
    kka = kk * a
    r_hat = r * gam
    a_hat = -kk * jnp.exp(cl - log_decay)
    b_hat = kka * inv_gam
    k_hat = k * inv_gam
    b_end = kka * to_end
    k_end = k * to_end

    p_i = lax.broadcasted_iota(jnp.int32, (P, P), 0)
    p_j = lax.broadcasted_iota(jnp.int32, (P, P), 1)
    same = p_i // L == p_j // L
    strict = jnp.logical_and(same, p_i > p_j)
    incl = jnp.logical_and(same, p_i >= p_j)
    eye = jnp.where(p_i == p_j, 1.0, 0.0)
    levels = []
    s = 2
    while s < L:
        levels.append((s, jnp.logical_and(p_i // (2 * s) == p_j // (2 * s),
                                          jnp.logical_and(p_i % (2 * s) >= s, p_j % (2 * s) < s))))
        s *= 2
    own = (lax.broadcasted_iota(jnp.int32, (P, 2 * N), 0) // L
           == lax.broadcasted_iota(jnp.int32, (P, 2 * N), 1) // N)

    def stack(x):
        return jnp.where(own, jnp.concatenate([x, x], axis=0), 0.0).astype(BF16)

    rng = range(pairs)
    sls = [slice(p * 2 * N, (p + 1) * 2 * N) for p in rng]
    st = [s_sc[p] for p in rng]
    ar_s = [jnp.concatenate([stack(a_hat[:, sl]), stack(r_hat[:, sl])], axis=0) for sl in sls]
    v_s = [stack(v[:, sl]) for sl in sls]
    m4 = [_dot_nt(ar_s[p], jnp.concatenate([stack(b_hat[:, sls[p]]), stack(k_hat[:, sls[p]])], axis=0))
          for p in rng]
    a_ab = [jnp.where(strict, m[:P, :P], 0.0) for m in m4]
    a_ak = [jnp.where(strict, m[:P, P:], 0.0).astype(BF16) for m in m4]
    a_r = [jnp.concatenate([jnp.where(incl, m[P:, :P], 0.0), jnp.where(incl, m[P:, P:], 0.0)], axis=1).astype(BF16)
           for m in m4]
    from_state = [_dot_nt(ar_s[p], st[p].astype(BF16)) for p in rng]
    rhs_u = [from_state[p][:P] + _dot(a_ak[p], v_s[p]) for p in rng]
    x = [eye + jnp.where(p_i // 2 == p_j // 2, m, 0.0) for m in a_ab]
    for s, blk in levels:
        xb = [xp.astype(BF16) for xp in x]
        m_s = [jnp.where(blk, a_ab[p], 0.0).astype(BF16) for p in rng]
        if s % SUBLANES:
            half = [_dot(xb[p], m_s[p]).astype(BF16) for p in rng]
            x = [x[p] + _dot(half[p], xb[p]) for p in rng]
        else:
            lower = [jnp.concatenate([xp[b + s:b + 2 * s] for b in range(0, P, 2 * s)], axis=0) for xp in x]
            half = [_dot(lower[p].astype(BF16), m_s[p]).astype(BF16) for p in rng]
            lower = [lower[p] + _dot(half[p], xb[p]) for p in rng]
            x = [jnp.concatenate(
                [blk_rows for i, b in enumerate(range(0, P, 2 * s))
                 for blk_rows in (x[p][b:b + s], lower[p][i * s:(i + 1) * s])], axis=0) for p in rng]
    su = [_dot(x[p].astype(BF16), rhs_u[p].astype(BF16)) for p in rng]
    su_v = [jnp.concatenate([su[p].astype(BF16), v_s[p]], axis=0) for p in rng]
    y_s = [from_state[p][P:] + _dot(a_r[p], su_v[p]) for p in rng]
    for p in rng:
        be = jnp.concatenate([stack(b_end[:, sls[p]]), stack(k_end[:, sls[p]])], axis=0)
        s_sc[p] = st[p] * gam_end[:, sls[p]] + _dot_tn(su_v[p], be)
    ys = [yp[:L] + yp[L:] for yp in y_s]
    y = ys[0] if pairs == 1 else jnp.concatenate(ys, axis=1)

    inv_n = 1.0 / N
    mean = head_sum(y, one_pass) * inv_n
    yc = y - mean
    var = head_sum(yc * yc, one_pass) * inv_n
    yn = yc * lax.rsqrt(var + RW_GN_EPS) * lg_ref[...] + lb_ref[...]
    o_ref[...] = ((yn + bonus) * gate).astype(o_ref.dtype)


def _rwkv(r, k, v, lw, la, lgate, w2, a2, g2, layer, w0, a0, k_k, k_a, r_k, lnx_g, lnx_b, w_side, batch, seq):
    t, d = r.shape
    L = RW_CHUNK
    heads = min(64, d // RW_HEAD)
    gw = heads * RW_HEAD
    nc = seq // L
    nh = d // gw
    tile = pl.BlockSpec((L, gw), lambda b, h, c: (b * nc + c, h))
    row = pl.BlockSpec((1, gw), lambda b, h, c: (0, h))
    lora = lambda f: pl.BlockSpec((L, f.shape[1]), lambda b, h, c: (b * nc + c, 0))
    lora_w = lambda w: pl.BlockSpec((None, w.shape[1], gw), lambda b, h, c: (layer, 0, h))
    side_in, side_out, side_shape = _side_cast(w_side, layer, batch * nh * nc, lambda b, h, c: (b * nh + h) * nc + c)
    return pl.pallas_call(
        functools.partial(_rwkv_kernel, heads=heads),
        out_shape=(jax.ShapeDtypeStruct((t, d), BF16), side_shape),
        grid=(batch, nh, nc),
        in_specs=([tile] * 3 + [lora(lw), lora(la), lora(lgate), lora_w(w2), lora_w(a2), lora_w(g2)] + [row] * 7
                  + [side_in]),
        out_specs=(tile, side_out),
        scratch_shapes=[pltpu.VMEM((heads // 2, 2 * RW_HEAD, 2 * RW_HEAD), F32)],
        compiler_params=_params("parallel", "parallel", "arbitrary"),
        name="rwkv7_chunk",
    )(r, k, v, lw, la, lgate, w2, a2, g2, w0, a0, k_k, k_a, r_k, lnx_g, lnx_b, w_side)


def kernel(x, c, ada_w, ada_b, mix_ln_g, mix_ln_b, ffn_ln_g, ffn_ln_b, ffn_w_in, ffn_w_out, ml_w_in, ml_b_i, ml_b_f, ml_norm_g, ml_w_out, rw_mu, rw_w_r, rw_w_k, rw_w_v, rw_w0, rw_w1, rw_w2, rw_a0, rw_a1, rw_a2, rw_g1, rw_g2, rw_k_k, rw_k_a, rw_r_k, rw_lnx_g, rw_lnx_b, rw_w_o):
    batch, seq, d = x.shape
    depth = ada_w.shape[0]
    t = batch * seq
    alpha = (2 * depth) ** 0.25
    hidden = ffn_w_out.shape[1]
    n_mixers = 2

    mod = _ada(c, ada_w, ada_b).reshape(depth, batch, 6, 1, d)
    row = lambda p: p.reshape(1, -1)
    zero_bias = lambda n: jnp.zeros((1, n), F32)

    res = (x.reshape(t, d),)
    u = None
    for layer in range(depth):
        sh_m, sc_m, gt_m, sh_f, sc_f, gt_f = [mod[layer, :, i] for i in range(6)]
        j = layer // n_mixers
        g_m, b_m = row(mix_ln_g[layer]), row(mix_ln_b[layer])
        if layer % n_mixers == 0:
            qkvo = ml_w_in.shape[2] - 2 * ML_HEADS
            w_gate = jnp.pad(ml_w_in[j:j + 1, :, qkvo:], ((0, 0), (0, 0), (0, LANES - 2 * ML_HEADS)))
            b_gate = jnp.pad(jnp.concatenate([ml_b_i[j], ml_b_f[j]]), (0, LANES - 2 * ML_HEADS)).reshape(1, LANES)
            if u is None:
                u, gates = _modulate(res[0], sc_m, sh_m, seq, w_gate, b_gate, _ml_gate_act)
            else:
                gates = _mm(u, w_gate, 0, LANES, b_gate, _ml_gate_act, F32, "mlstm_gates")
            proj = _mm(u, jnp.swapaxes(ml_w_in, 1, 2), j, qkvo, zero_bias(qkvo), _identity, BF16, "mlstm_in",
                       w_is_nk=True)
            y, w_o = _mlstm(proj, gates, row(ml_norm_g[j]), ml_w_out, j, batch, seq)
            z = _mm_resid(y, w_o, res, gt_m, alpha, seq, "mlstm_out")
        else:
            xr, xw, xk, xv, xa, xg = u
            nb = zero_bias
            r = _mm(xr, rw_w_r, j, d, nb(d), _identity, BF16, "rwkv_r")
            k = _mm(xk, rw_w_k, j, d, nb(d), _identity, BF16, "rwkv_k")
            v = _mm(xv, rw_w_v, j, d, nb(d), _identity, BF16, "rwkv_v")
            n_w, n_a, n_g = rw_w1.shape[2], rw_a1.shape[2], rw_g1.shape[2]
            lw = _mm(xw, rw_w1, j, n_w, nb(n_w), jnp.tanh, BF16, "rwkv_w1")
            la = _mm(xa, rw_a1, j, n_a, nb(n_a), _identity, BF16, "rwkv_a1")
            lg = _mm(xg, rw_g1, j, n_g, nb(n_g), _sigmoid, BF16, "rwkv_g1")
            y, w_o = _rwkv(r, k, v, lw, la, lg, rw_w2.astype(BF16), rw_a2.astype(BF16), rw_g2.astype(BF16), j,
                           row(rw_w0[j]), row(rw_a0[j]), row(rw_k_k[j]), row(rw_k_a[j]), row(rw_r_k[j]),
                           row(rw_lnx_g[j]), row(rw_lnx_b[j]), rw_w_o, batch, seq)
            z = _mm_resid(y, w_o, res, gt_m, alpha, seq, "rwkv_out")
        stats, u = _layer_norm(z, g_m, b_m, seq, ("mod", sc_f, sh_f))
        res = (z, stats, g_m, b_m)
        hid, w_down = _mm_swiglu(u, ffn_w_in, ffn_w_out, layer, hidden)
        z = _mm_resid(hid, w_down, res, gt_f, alpha, seq, "ffn_out")
        g_f, b_f = row(ffn_ln_g[layer]), row(ffn_ln_b[layer])
        if layer + 1 == depth:
            return _layer_norm(z, g_f, b_f, seq).reshape(batch, seq, d)
        nsh, nsc = mod[layer + 1, :, 0], mod[layer + 1, :, 1]
        if (layer + 1) % n_mixers == 0:
            stats, u = _layer_norm(z, g_f, b_f, seq, ("mod", nsc, nsh))
        else:
            out = _layer_norm(z, g_f, b_f, seq, ("shift", nsc, nsh, rw_mu[(layer + 1) // n_mixers]))
            stats, u = out[0], out[1:]
        res = (z, stats, g_f, b_f)
```

```python
import functools

import jax
import jax.numpy as jnp
from jax import lax
from jax.experimental import pallas as pl
from jax.experimental.pallas import tpu as pltpu

F32 = jnp.float32
BF16 = jnp.bfloat16

ML_HEADS = 8
ML_CHUNK = 128
ML_GATE_CAP = 15.0
ML_NORM_EPS = 1e-6
RW_HEAD = 64
RW_CHUNK = 64
RW_GN_EPS = 64e-5
LN_EPS = 1e-5
LANES = 128
SUBLANES = 8
VMEM_LIMIT_BYTES = 56 * 1024 * 1024
TILE_VMEM_BUDGET_BYTES = 52 * 1024 * 1024
WIDE_VMEM_LIMIT_BYTES = 60 * 1024 * 1024
SHORT_K = 4096


def _params(*semantics, vmem=VMEM_LIMIT_BYTES):
    return pltpu.CompilerParams(dimension_semantics=semantics, vmem_limit_bytes=vmem)


def _pick(n, candidates):
    for c in candidates:
        if n % c == 0:
            return c
    return n


def _dot(a, b):
    return jnp.dot(a, b, preferred_element_type=F32)


def _dot_nt(a, b):
    return lax.dot_general(a, b, (((1,), (1,)), ((), ())), preferred_element_type=F32)


def _dot_tn(a, b):
    return lax.dot_general(a, b, (((0,), (0,)), ((), ())), preferred_element_type=F32)


def _dot_01(m, x):
    hi = x.astype(BF16)
    lo = (x - hi.astype(F32)).astype(BF16)
    return _dot(m, hi) + _dot(m, lo)


def _sigmoid(x):
    return 1.0 / (1.0 + jnp.exp(-x))


def _softplus(x):
    return jnp.maximum(x, 0.0) + jnp.log1p(jnp.exp(-jnp.abs(x)))


def _ada_kernel(c_ref, w_ref, b_ref, o_ref):
    c = c_ref[...]
    c_act = (c * _sigmoid(c)).astype(BF16)
    o_ref[0] = _dot(c_act, w_ref[0].astype(BF16)) + b_ref[0]


def _ada(c, ada_w, ada_b):
    depth, d, n = ada_w.shape
    b = c.shape[0]
    tn = _pick(n, (1024, 512, 256, 128))
    return pl.pallas_call(
        _ada_kernel,
        out_shape=jax.ShapeDtypeStruct((depth, b, n), F32),
        grid=(depth, n // tn),
        in_specs=[pl.BlockSpec((b, d), lambda l, j: (0, 0)),
                  pl.BlockSpec((1, d, tn), lambda l, j: (l, 0, j)),
                  pl.BlockSpec((1, 1, tn), lambda l, j: (l, 0, j))],
        out_specs=pl.BlockSpec((1, b, tn), lambda l, j: (l, 0, j)),
        compiler_params=_params("parallel", "parallel"),
        name="ada_mod",
    )(c, ada_w, ada_b.reshape(depth, 1, n))


def _modulate_kernel(x_ref, sc_ref, sh_ref, w_ref, b_ref, o_ref, g_ref, *, act):
    u = (x_ref[...] * (1.0 + sc_ref[0]) + sh_ref[0]).astype(o_ref.dtype)
    o_ref[...] = u
    g_ref[...] = act(_dot(u, w_ref[0].astype(BF16)) + b_ref[...])


def _modulate(x2, sc, sh, seq, w, bias, act):
    t, d = x2.shape
    n = w.shape[2]
    ts = _pick(seq, (512, 256, 128))
    vec = pl.BlockSpec((1, 1, d), lambda i: ((i * ts) // seq, 0, 0))
    return pl.pallas_call(
        functools.partial(_modulate_kernel, act=act),
        out_shape=(jax.ShapeDtypeStruct((t, d), BF16), jax.ShapeDtypeStruct((t, n), F32)),
        grid=(t // ts,),
        in_specs=[pl.BlockSpec((ts, d), lambda i: (i, 0)), vec, vec,
                  pl.BlockSpec((1, d, n), lambda i: (0, 0, 0)), pl.BlockSpec((1, n), lambda i: (0, 0))],
        out_specs=(pl.BlockSpec((ts, d), lambda i: (i, 0)), pl.BlockSpec((ts, n), lambda i: (i, 0))),
        compiler_params=_params("parallel"),
        name="modulate",
    )(x2, sc, sh, w, bias)


def _mm_kernel(a_ref, w_ref, b_ref, o_ref, *, act, w_is_nk):
    dot = _dot_nt if w_is_nk else _dot
    y = dot(a_ref[...], w_ref[...].astype(BF16)) + b_ref[...]
    o_ref[...] = act(y).astype(o_ref.dtype)


def _mm(a, w, layer, n, bias, act, out_dtype, name, w_is_nk=False):
    t, k = a.shape
    tm = _pick(t, (2048, 1024, 512, 256, 128))
    out_bytes = jnp.dtype(out_dtype).itemsize
    fits = lambda c: 2 * (tm * k * 2 + k * c * 4 + tm * c * out_bytes) <= TILE_VMEM_BUDGET_BYTES
    tn = _pick(n, [c for c in (512, 256, 128) if fits(c)])
    if w_is_nk:
        w_spec = pl.BlockSpec((None, tn, k), lambda i, j: (layer, j, 0))
    else:
        w_spec = pl.BlockSpec((None, k, tn), lambda i, j: (layer, 0, j))
    return pl.pallas_call(
        functools.partial(_mm_kernel, act=act, w_is_nk=w_is_nk),
        out_shape=jax.ShapeDtypeStruct((t, n), out_dtype),
        grid=(t // tm, n // tn),
        in_specs=[pl.BlockSpec((tm, k), lambda i, j: (i, 0)),
                  w_spec,
                  pl.BlockSpec((1, tn), lambda i, j: (0, j))],
        out_specs=pl.BlockSpec((tm, tn), lambda i, j: (i, j)),
        compiler_params=_params("parallel", "parallel"),
        name=name,
    )(a, w, bias)


def _identity(y):
    return y


def _side_cast(w, layer, steps, step_of):
    k, n = w.shape[1:]
    rows = k // steps
    assert rows * steps == k and rows % (2 * SUBLANES) == 0, (w.shape, steps)
    in_spec = pl.BlockSpec((None, rows, n), lambda *g: (layer, step_of(*g), 0))
    out_spec = pl.BlockSpec((rows, n), lambda *g: (step_of(*g), 0))
    return in_spec, out_spec, jax.ShapeDtypeStruct((k, n), BF16)


def _side_cast_step(src_ref, dst_ref):
    dst_ref[...] = src_ref[...].astype(dst_ref.dtype)


def _swiglu_kernel(a_ref, wg_ref, wu_ref, wo_ref, o_ref, wo16_ref):
    a = a_ref[...]
    gate = _dot(a, wg_ref[...].astype(BF16))
    up = _dot(a, wu_ref[...].astype(BF16))
    o_ref[...] = (gate * _sigmoid(gate) * up).astype(o_ref.dtype)
    _side_cast_step(wo_ref, wo16_ref)


def _mm_swiglu(a, w_in, w_out, layer, hidden):
    t, k = a.shape
    tm = _pick(t, (2048, 1024, 512, 256, 128))
    tn = _pick(hidden, (256, 128))
    nj = hidden // tn
    side_in, side_out, side_shape = _side_cast(w_out, layer, (t // tm) * nj, lambda i, j: i * nj + j)
    return pl.pallas_call(
        _swiglu_kernel,
        out_shape=(jax.ShapeDtypeStruct((t, hidden), BF16), side_shape),
        grid=(t // tm, nj),
        in_specs=[pl.BlockSpec((tm, k), lambda i, j: (i, 0)),
                  pl.BlockSpec((None, k, tn), lambda i, j: (layer, 0, j)),
                  pl.BlockSpec((None, k, tn), lambda i, j: (layer, 0, j + nj)),
                  side_in],
        out_specs=(pl.BlockSpec((tm, tn), lambda i, j: (i, j)), side_out),
        compiler_params=_params("parallel", "parallel", vmem=WIDE_VMEM_LIMIT_BYTES),
        name="ffn_in_swiglu",
    )(a, w_in, w_in, w_out)


def _ln_apply(z, mu, rstd, g, b):
    return (z - mu) * rstd * g + b


def _resid_kernel(a_ref, w_ref, x_ref, gt_ref, *rest, alpha, normed):
    if normed:
        st_ref, g_ref, b_ref, o_ref = rest
        st = st_ref[...]
        x = _ln_apply(x_ref[...], st[:, 0:1], st[:, 1:2], g_ref[...], b_ref[...])
    else:
        o_ref, = rest
        x = x_ref[...]
    o_ref[...] = alpha * x + gt_ref[0] * _dot(a_ref[...], w_ref[...])


def _mm_resid(a, w, res, gt, alpha, seq, name):
    t, k = a.shape
    n = w.shape[1]
    normed = len(res) > 1
    sizes = (1024, 512, 256, 128) if k <= SHORT_K else (512, 256, 128)
    tm, tn = _pick(min(t, seq), sizes), _pick(n, sizes)
    in_specs = [pl.BlockSpec((tm, k), lambda i, j: (i, 0)),
                pl.BlockSpec((k, tn), lambda i, j: (0, j)),
                pl.BlockSpec((tm, tn), lambda i, j: (i, j)),
                pl.BlockSpec((1, 1, tn), lambda i, j: ((i * tm) // seq, 0, j))]
    operands = [a, w, res[0], gt]
    if normed:
        in_specs += [pl.BlockSpec((tm, 2), lambda i, j: (i, 0)),
                     pl.BlockSpec((1, tn), lambda i, j: (0, j)),
                     pl.BlockSpec((1, tn), lambda i, j: (0, j))]
        operands += list(res[1:])
    return pl.pallas_call(
        functools.partial(_resid_kernel, alpha=alpha, normed=normed),
        out_shape=jax.ShapeDtypeStruct((t, n), F32),
        grid=(t // tm, n // tn),
        in_specs=in_specs,
        out_specs=pl.BlockSpec((tm, tn), lambda i, j: (i, j)),
        compiler_params=_params("parallel", "parallel", vmem=WIDE_VMEM_LIMIT_BYTES),
        name=name,
    )(*operands)


def _col_chunks(d):
    cw = _pick(d, (512, 256, 128))
    return [slice(c, c + cw) for c in range(0, d, cw)]


def _ln_stats_ref(z_ref):
    d = z_ref.shape[1]
    chunks = _col_chunks(d)
    mu = jnp.sum(sum(z_ref[:, c] for c in chunks), axis=-1, keepdims=True) * (1.0 / d)
    sq = jnp.sum(sum(jnp.square(z_ref[:, c] - mu) for c in chunks), axis=-1, keepdims=True)
    return mu, lax.rsqrt(sq * (1.0 / d) + LN_EPS)


def _ln_stats(z):
    mu = jnp.mean(z, axis=-1, keepdims=True)
    zc = z - mu
    var = jnp.mean(zc * zc, axis=-1, keepdims=True)
    return mu, lax.rsqrt(var + LN_EPS)


def _ln_kernel(z_ref, g_ref, b_ref, x_ref):
    z = z_ref[...]
    mu, rstd = _ln_stats(z)
    x_ref[...] = _ln_apply(z, mu, rstd, g_ref[...], b_ref[...])


def _ln_mod_kernel(z_ref, g_ref, b_ref, sc_ref, sh_ref, st_ref, u_ref):
    z = z_ref[...]
    mu, rstd = _ln_stats(z)
    st_ref[:, 0:1] = mu
    st_ref[:, 1:2] = rstd
    scale = 1.0 + sc_ref[0]
    u_ref[...] = _ln_apply(z, mu, rstd, g_ref[...] * scale, b_ref[...] * scale + sh_ref[0]).astype(u_ref.dtype)


def _ln_shift_kernel(z_ref, zp_ref, g_ref, b_ref, sc_ref, sh_ref, mu_ref, st_ref, *mix_refs, ts, seq):
    mu, rstd = _ln_stats_ref(z_ref)
    st_ref[:, 0:1] = mu
    st_ref[:, 1:2] = rstd
    mu_p, rstd_p = _ln_stats_ref(zp_ref)
    first = (pl.program_id(0) * ts) % seq == 0
    row = lax.broadcasted_iota(jnp.int32, (ts, 1), 0)
    for c in _col_chunks(z_ref.shape[1]):
        scale = 1.0 + sc_ref[0, :, c]
        g, b = g_ref[:, c] * scale, b_ref[:, c] * scale + sh_ref[0, :, c]
        u = _ln_apply(z_ref[:, c], mu, rstd, g, b)
        prev = _ln_apply(zp_ref[:, c], mu_p, rstd_p, g, b)[SUBLANES - 1:SUBLANES, :]
        prev = jnp.where(first, 0.0, prev)
        u_prev = jnp.where(row == 0, prev, pltpu.roll(u, shift=1, axis=0))
        xx = (u_prev - u).astype(BF16)
        u = u.astype(BF16)
        for j, ref in enumerate(mix_refs):
            ref[:, c] = u + xx * mu_ref[j:j + 1, c].astype(BF16)


def _layer_norm(z, g, b, seq, nxt=None):
    t, d = z.shape
    row = pl.BlockSpec((1, d), lambda i: (0, 0))
    stats = jax.ShapeDtypeStruct((t, 2), F32)
    if nxt is None:
        ts = _pick(seq, (256, 128))
        tile = pl.BlockSpec((ts, d), lambda i: (i, 0))
        return pl.pallas_call(
            _ln_kernel, out_shape=jax.ShapeDtypeStruct((t, d), F32), grid=(t // ts,),
            in_specs=[tile, row, row], out_specs=tile,
            compiler_params=_params("parallel"), name="layer_norm",
        )(z, g, b)
    if nxt[0] == "mod":
        ts = _pick(seq, (256, 128))
        tile = pl.BlockSpec((ts, d), lambda i: (i, 0))
        vec = pl.BlockSpec((1, 1, d), lambda i: ((i * ts) // seq, 0, 0))
        return pl.pallas_call(
            _ln_mod_kernel,
            out_shape=(stats, jax.ShapeDtypeStruct((t, d), BF16)),
            grid=(t // ts,),
            in_specs=[tile, row, row, vec, vec],
            out_specs=(pl.BlockSpec((ts, 2), lambda i: (i, 0)), tile),
            compiler_params=_params("parallel"), name="layer_norm_mod",
        )(z, g, b, nxt[1], nxt[2])
    _, sc, sh, mu = nxt
    nmix = mu.shape[0]
    ts = _pick(seq, (128,))
    tile = pl.BlockSpec((ts, d), lambda i: (i, 0))
    prev = pl.BlockSpec((SUBLANES, d), lambda i: (jnp.maximum(i * (ts // SUBLANES) - 1, 0), 0))
    vec = pl.BlockSpec((1, 1, d), lambda i: ((i * ts) // seq, 0, 0))
    return pl.pallas_call(
        functools.partial(_ln_shift_kernel, ts=ts, seq=seq),
        out_shape=(stats,) + (jax.ShapeDtypeStruct((t, d), BF16),) * nmix,
        grid=(t // ts,),
        in_specs=[tile, prev, row, row, vec, vec, pl.BlockSpec((nmix, d), lambda i: (0, 0))],
        out_specs=(pl.BlockSpec((ts, 2), lambda i: (i, 0)),) + (tile,) * nmix,
        compiler_params=_params("parallel"), name="layer_norm_shift",
    )(z, z, g, b, sc, sh, mu)


def _ml_gate_act(y):
    lane = lax.broadcasted_iota(jnp.int32, y.shape, 1)
    i_pre = ML_GATE_CAP * jnp.tanh(y / ML_GATE_CAP)
    log_f = -_softplus(-y)
    return jnp.where(lane < ML_HEADS, i_pre, log_f)


def _mlstm_kernel(q_ref, k_ref, v_ref, o_ref, gate_ref, irow_ref, frow_ref, ng_ref, wside_ref,
                  out_ref, wside16_ref, c_sc, n_sc, m_sc, *, heads):
    L = ML_CHUNK
    _side_cast_step(wside_ref, wside16_ref)

    @pl.when(pl.program_id(2) == 0)
    def _():
        c_sc[...] = jnp.zeros_like(c_sc)
        n_sc[...] = jnp.zeros_like(n_sc)
        m_sc[...] = jnp.zeros_like(m_sc)

    dqk = q_ref.shape[1] // heads
    dv = v_ref.shape[1] // heads
    k_scale = dqk ** -0.5
    r_idx = lax.broadcasted_iota(jnp.int32, (L, L), 0)
    c_idx = lax.broadcasted_iota(jnp.int32, (L, L), 1)
    causal = r_idx >= c_idx
    anti = r_idx <= c_idx

    hs = range(heads)
    q = [q_ref[:, h * dqk:(h + 1) * dqk] for h in hs]
    k = [k_ref[:, h * dqk:(h + 1) * dqk] for h in hs]
    v = [v_ref[:, h * dv:(h + 1) * dv] for h in hs]
    i_col = [gate_ref[:, h:h + 1] for h in hs]
    f_col = [gate_ref[:, heads + h:heads + h + 1] for h in hs]
    i_row, f_row = [irow_ref[h] for h in hs], [frow_ref[h] for h in hs]
    c_st, n_st, m_st = [c_sc[h] for h in hs], [n_sc[h] for h in hs], [m_sc[h] for h in hs]

    qk = [_dot_nt(q[h], k[h]) for h in hs]
    q_c = [_dot(q[h], c_st[h].astype(BF16)) for h in hs]
    g_col = [jnp.sum(jnp.where(causal, f_row[h], 0.0), axis=1, keepdims=True) for h in hs]
    g_row = [jnp.sum(jnp.where(anti, f_col[h], 0.0), axis=0, keepdims=True) for h in hs]
    g_last = [jnp.sum(f_row[h], axis=1, keepdims=True) for h in hs]
    log_d = [jnp.where(causal, g_col[h] - g_row[h] + i_row[h], -jnp.inf) for h in hs]
    log_inter = [g_col[h] + m_st[h] for h in hs]
    m_row = [jnp.maximum(jnp.max(log_d[h], axis=1, keepdims=True), log_inter[h]) for h in hs]
    scores = [qk[h] * k_scale * jnp.exp(log_d[h] - m_row[h]) for h in hs]
    inter = [jnp.exp(log_inter[h] - m_row[h]) for h in hs]
    num = [_dot(scores[h].astype(BF16), v[h]) + inter[h] * q_c[h] for h in hs]
    q_n = [jnp.sum(q[h].astype(F32) * n_st[h], axis=1, keepdims=True) for h in hs]
    den = [jnp.sum(scores[h], axis=1, keepdims=True) + inter[h] * q_n[h] for h in hs]
    for h in hs:
        hid = num[h] / jnp.maximum(jnp.abs(den[h]), jnp.exp(-m_row[h]))
        hid = hid * lax.rsqrt(jnp.mean(hid * hid, axis=1, keepdims=True) + ML_NORM_EPS)
        sl = slice(h * dv, (h + 1) * dv)
        out_ref[:, sl] = (hid * ng_ref[:, sl] * _sigmoid(o_ref[:, sl].astype(F32))).astype(out_ref.dtype)

    log_w = [g_last[h] - g_col[h] + i_col[h] for h in hs]
    m_new = [jnp.maximum(g_last[h] + m_st[h], jnp.max(log_w[h], axis=0, keepdims=True)) for h in hs]
    wk = [k[h].astype(F32) * (k_scale * jnp.exp(log_w[h] - m_new[h])) for h in hs]
    kv = [_dot_tn(wk[h].astype(BF16), v[h]) for h in hs]
    for h in hs:
        decay = jnp.exp(g_last[h] + m_st[h] - m_new[h])
        c_sc[h] = decay * c_st[h] + kv[h]
        n_sc[h] = decay * n_st[h] + jnp.sum(wk[h], axis=0, keepdims=True)
        m_sc[h] = m_new[h]


def _mlstm(proj, gates, norm_g, w_side, layer, batch, seq):
    t = proj.shape[0]
    H, L = ML_HEADS, ML_CHUNK
    G = H
    dv = norm_g.shape[1] // H
    dqk = dv // 2
    nc = seq // L
    ng = H // G
    g = gates[:, :2 * H].reshape(batch, seq, 2 * H).transpose(0, 2, 1)
    i_g, f_g = g[:, :H], g[:, H:]
    rowv = lambda a: a.reshape(batch, H, nc, 1, L)
    col_spec = pl.BlockSpec((L, gates.shape[1]), lambda b, h, c: (b * nc + c, 0))
    row_spec = pl.BlockSpec((None, G, None, 1, L), lambda b, h, c: (b, h, c, 0, 0))
    tok = lambda b, c: b * nc + c
    side_in, side_out, side_shape = _side_cast(w_side, layer, batch * ng * nc, lambda b, h, c: (b * ng + h) * nc + c)
    return pl.pallas_call(
        functools.partial(_mlstm_kernel, heads=G),
        out_shape=(jax.ShapeDtypeStruct((t, H * dv), BF16), side_shape),
        grid=(batch, ng, nc),
        in_specs=[pl.BlockSpec((L, G * dqk), lambda b, h, c: (tok(b, c), h)),
                  pl.BlockSpec((L, G * dqk), lambda b, h, c: (tok(b, c), ng + h)),
                  pl.BlockSpec((L, G * dv), lambda b, h, c: (tok(b, c), ng + h)),
                  pl.BlockSpec((L, G * dv), lambda b, h, c: (tok(b, c), 2 * ng + h)),
                  col_spec, row_spec, row_spec,
                  pl.BlockSpec((1, G * dv), lambda b, h, c: (0, h)),
                  side_in],
        out_specs=(pl.BlockSpec((L, G * dv), lambda b, h, c: (tok(b, c), h)), side_out),
        scratch_shapes=[pltpu.VMEM((G, dqk, dv), F32), pltpu.VMEM((G, 1, dqk), F32), pltpu.VMEM((G, 1, 1), F32)],
        compiler_params=_params("parallel", "parallel", "arbitrary"),
        name="mlstm_chunk",
    )(proj, proj, proj, proj, gates, rowv(i_g), rowv(f_g), norm_g, w_side)


def _rwkv_kernel(r_ref, k_ref, v_ref, lw_ref, la_ref, lgate_ref, w2_ref, a2_ref, g2_ref, w0_ref, a0_ref,
                 kk_ref, ka_ref, rk_ref, lg_ref, lb_ref, wside_ref, o_ref, wside16_ref, s_sc, *, heads):
    L, N = RW_CHUNK, RW_HEAD
    _side_cast_step(wside_ref, wside16_ref)
    gw = heads * N

    @pl.when(pl.program_id(2) == 0)
    def _():
        s_sc[...] = jnp.zeros_like(s_sc)

    P = 2 * L
    pairs = heads // 2

    r, k, v = r_ref[...].astype(F32), k_ref[...].astype(F32), v_ref[...].astype(F32)
    w_pre = w0_ref[...] + _dot(lw_ref[...], w2_ref[...])
    a_pre = a0_ref[...] + _dot(la_ref[...], a2_ref[...])
    gate = _dot(lgate_ref[...], g2_ref[...])
    log_decay = -jnp.exp(-_softplus(-w_pre) - 0.5)
    a = _sigmoid(a_pre)

    sw = min(gw, 256)
    li = lax.broadcasted_iota(jnp.int32, (sw, sw), 0) // N
    lj = lax.broadcasted_iota(jnp.int32, (sw, sw), 1) // N
    head_ones = jnp.where(li == lj, 1.0, 0.0).astype(BF16)

    def head_sum(x, split):
        parts = split(x)
        cols = []
        for c0 in range(0, gw, sw):
            acc = None
            for p in parts:
                d = _dot(p[:, c0:c0 + sw], head_ones)
                acc = d if acc is None else acc + d
            cols.append(acc)
        return cols[0] if len(cols) == 1 else jnp.concatenate(cols, axis=1)

    one_pass = lambda x: (x.astype(BF16),)

    t_i = lax.broadcasted_iota(jnp.int32, (L, L), 0)
    t_j = lax.broadcasted_iota(jnp.int32, (L, L), 1)
    tri = jnp.where(t_i >= t_j, 1.0, 0.0).astype(BF16)

    kkr = k * kk_ref[...]
    k = k * (1.0 + (a - 1.0) * ka_ref[...])
    sums = head_sum(jnp.concatenate([kkr * kkr, r * k * rk_ref[...]], axis=0), one_pass)
    kk = kkr * lax.rsqrt(jnp.maximum(sums[:L], 1e-24))
    bonus = sums[L:] * v

    cl = _dot_01(tri, log_decay)
    cl_last = cl[L - 1:L, :]
    gam = jnp.exp(cl)
    inv_gam = jnp.exp(-cl)
    gam_end = jnp.exp(cl_last)
    to_end = gam_end * inv_gam
    kka = kk * a
    r_hat = r * gam
    a_hat = -kk * jnp.exp(cl - log_decay)
    b_hat = kka * inv_gam
    k_hat = k * inv_gam
    b_end = kka * to_end
    k_end = k * to_end

    p_i = lax.broadcasted_iota(jnp.int32, (P, P), 0)
    p_j = lax.broadcasted_iota(jnp.int32, (P, P), 1)
    same = p_i // L == p_j // L
    strict = jnp.logical_and(same, p_i > p_j)
    incl = jnp.logical_and(same, p_i >= p_j)
    eye = jnp.where(p_i == p_j, 1.0, 0.0)
    levels = []
    s = 2
    while s < L:
        levels.append((s, jnp.logical_and(p_i // (2 * s) == p_j // (2 * s),
                                          jnp.logical_and(p_i % (2 * s) >= s, p_j % (2 * s) < s))))
        s *= 2
    own = (lax.broadcasted_iota(jnp.int32, (P, 2 * N), 0) // L
           == lax.broadcasted_iota(jnp.int32, (P, 2 * N), 1) // N)

    def stack(x):
        return jnp.where(own, jnp.concatenate([x, x], axis=0), 0.0).astype(BF16)

    rng = range(pairs)
    sls = [slice(p * 2 * N, (p + 1) * 2 * N) for p in rng]
    st = [s_sc[p] for p in rng]
    ar_s = [jnp.concatenate([stack(a_hat[:, sl]), stack(r_hat[:, sl])], axis=0) for sl in sls]
    v_s = [stack(v[:, sl]) for sl in sls]
    m4 = [_dot_nt(ar_s[p], jnp.concatenate([stack(b_hat[:, sls[p]]), stack(k_hat[:, sls[p]])], axis=0))
          for p in rng]
    a_ab = [jnp.where(strict, m[:P, :P], 0.0) for m in m4]
    a_ak = [jnp.where(strict, m[:P, P:], 0.0).astype(BF16) for m in m4]
    a_r = [jnp.concatenate([jnp.where(incl, m[P:, :P], 0.0), jnp.where(incl, m[P:, P:], 0.0)], axis=1).astype(BF16)
           for m in m4]
    from_state = [_dot_nt(ar_s[p], st[p].astype(BF16)) for p in rng]
    rhs_u = [from_state[p][:P] + _dot(a_ak[p], v_s[p]) for p in rng]
    x = [eye + jnp.where(p_i // 2 == p_j // 2, m, 0.0) for m in a_ab]
    for s, blk in levels:
        xb = [xp.astype(BF16) for xp in x]
        m_s = [jnp.where(blk, a_ab[p], 0.0).astype(BF16) for p in rng]
        if s % SUBLANES:
            half = [_dot(xb[p], m_s[p]).astype(BF16) for p in rng]
            x = [x[p] + _dot(half[p], xb[p]) for p in rng]
        else:
            lower = [jnp.concatenate([xp[b + s:b + 2 * s] for b in range(0, P, 2 * s)], axis=0) for xp in x]
            half = [_dot(lower[p].astype(BF16), m_s[p]).astype(BF16) for p in rng]
            lower = [lower[p] + _dot(half[p], xb[p]) for p in rng]
            x = [jnp.concatenate(
                [blk_rows for i, b in enumerate(range(0, P, 2 * s))
                 for blk_rows in (x[p][b:b + s], lower[p][i * s:(i + 1) * s])], axis=0) for p in rng]
    su = [_dot(x[p].astype(BF16), rhs_u[p].astype(BF16)) for p in rng]
    su_v = [jnp.concatenate([su[p].astype(BF16), v_s[p]], axis=0) for p in rng]
    y_s = [from_state[p][P:] + _dot(a_r[p], su_v[p]) for p in rng]
    for p in rng:
        be = jnp.concatenate([stack(b_end[:, sls[p]]), stack(k_end[:, sls[p]])], axis=0)
        s_sc[p] = st[p] * gam_end[:, sls[p]] + _dot_tn(su_v[p], be)
    ys = [yp[:L] + yp[L:] for yp in y_s]
    y = ys[0] if pairs == 1 else jnp.concatenate(ys, axis=1)

    inv_n = 1.0 / N
    mean = head_sum(y, one_pass) * inv_n
    yc = y - mean
    var = head_sum(yc * yc, one_pass) * inv_n
    yn = yc * lax.rsqrt(var + RW_GN_EPS) * lg_ref[...] + lb_ref[...]
    o_ref[...] = ((yn + bonus) * gate).astype(o_ref.dtype)


def _rwkv(r, k, v, lw, la, lgate, w2, a2, g2, layer, w0, a0, k_k, k_a, r_k, lnx_g, lnx_b, w_side, batch, seq):
    t, d = r.shape
    L = RW_CHUNK
    heads = min(64, d // RW_HEAD)
    gw = heads * RW_HEAD
    nc = seq // L
    nh = d // gw
    tile = pl.BlockSpec((L, gw), lambda b, h, c: (b * nc + c, h))
    row = pl.BlockSpec((1, gw), lambda b, h, c: (0, h))
    lora = lambda f: pl.BlockSpec((L, f.shape[1]), lambda b, h, c: (b * nc + c, 0))
    lora_w = lambda w: pl.BlockSpec((None, w.shape[1], gw), lambda b, h, c: (layer, 0, h))
    side_in, side_out, side_shape = _side_cast(w_side, layer, batch * nh * nc, lambda b, h, c: (b * nh + h) * nc + c)
    return pl.pallas_call(
        functools.partial(_rwkv_kernel, heads=heads),
        out_shape=(jax.ShapeDtypeStruct((t, d), BF16), side_shape),
        grid=(batch, nh, nc),
        in_specs=([tile] * 3 + [lora(lw), lora(la), lora(lgate), lora_w(w2), lora_w(a2), lora_w(g2)] + [row] * 7
                  + [side_in]),
        out_specs=(tile, side_out),
        scratch_shapes=[pltpu.VMEM((heads // 2, 2 * RW_HEAD, 2 * RW_HEAD), F32)],
        compiler_params=_params("parallel", "parallel", "arbitrary"),
        name="rwkv7_chunk",
    )(r, k, v, lw, la, lgate, w2, a2, g2, w0, a0, k_k, k_a, r_k, lnx_g, lnx_b, w_side)


def kernel(x, c, ada_w, ada_b, mix_ln_g, mix_ln_b, ffn_ln_g, ffn_ln_b, ffn_w_in, ffn_w_out, ml_w_in, ml_b_i, ml_b_f, ml_norm_g, ml_w_out, rw_mu, rw_w_r, rw_w_k, rw_w_v, rw_w0, rw_w1, rw_w2, rw_a0, rw_a1, rw_a2, rw_g1, rw_g2, rw_k_k, rw_k_a, rw_r_k, rw_lnx_g, rw_lnx_b, rw_w_o):
    batch, seq, d = x.shape
    depth = ada_w.shape[0]
    t = batch * seq
    alpha = (2 * depth) ** 0.25
    hidden = ffn_w_out.shape[1]
    n_mixers = 2

    mod = _ada(c, ada_w, ada_b).reshape(depth, batch, 6, 1, d)
    row = lambda p: p.reshape(1, -1)
    zero_bias = lambda n: jnp.zeros((1, n), F32)

    res = (x.reshape(t, d),)
    u = None
    for layer in range(depth):
        sh_m, sc_m, gt_m, sh_f, sc_f, gt_f = [mod[layer, :, i] for i in range(6)]
        j = layer // n_mixers
        g_m, b_m = row(mix_ln_g[layer]), row(mix_ln_b[layer])
        if layer % n_mixers == 0:
            qkvo = ml_w_in.shape[2] - 2 * ML_HEADS
            w_gate = jnp.pad(ml_w_in[j:j + 1, :, qkvo:], ((0, 0), (0, 0), (0, LANES - 2 * ML_HEADS)))
            b_gate = jnp.pad(jnp.concatenate([ml_b_i[j], ml_b_f[j]]), (0, LANES - 2 * ML_HEADS)).reshape(1, LANES)
            if u is None:
                u, gates = _modulate(res[0], sc_m, sh_m, seq, w_gate, b_gate, _ml_gate_act)
            else:
                gates = _mm(u, w_gate, 0, LANES, b_gate, _ml_gate_act, F32, "mlstm_gates")
            proj = _mm(u, jnp.swapaxes(ml_w_in, 1, 2), j, qkvo, zero_bias(qkvo), _identity, BF16, "mlstm_in",
                       w_is_nk=True)
            y, w_o = _mlstm(proj, gates, row(ml_norm_g[j]), ml_w_out, j, batch, seq)
            z = _mm_resid(y, w_o, res, gt_m, alpha, seq, "mlstm_out")
        else:
            xr, xw, xk, xv, xa, xg = u
            nb = zero_bias
            r = _mm(xr, rw_w_r, j, d, nb(d), _identity, BF16, "rwkv_r")
            k = _mm(xk, rw_w_k, j, d, nb(d), _identity, BF16, "rwkv_k")
            v = _mm(xv, rw_w_v, j, d, nb(d), _identity, BF16, "rwkv_v")
            n_w, n_a, n_g = rw_w1.shape[2], rw_a1.shape[2], rw_g1.shape[2]
            lw = _mm(xw, rw_w1, j, n_w, nb(n_w), jnp.tanh, BF16, "rwkv_w1")
            la = _mm(xa, rw_a1, j, n_a, nb(n_a), _identity, BF16, "rwkv_a1")
            lg = _mm(xg, rw_g1, j, n_g, nb(n_g), _sigmoid, BF16, "rwkv_g1")
            y, w_o = _rwkv(r, k, v, lw, la, lg, rw_w2.astype(BF16), rw_a2.astype(BF16), rw_g2.astype(BF16), j,
                           row(rw_w0[j]), row(rw_a0[j]), row(rw_k_k[j]), row(rw_k_a[j]), row(rw_r_k[j]),
                           row(rw_lnx_g[j]), row(rw_lnx_b[j]), rw_w_o, batch, seq)
            z = _mm_resid(y, w_o, res, gt_m, alpha, seq, "rwkv_out")
        stats, u = _layer_norm(z, g_m, b_m, seq, ("mod", sc_f, sh_f))
        res = (z, stats, g_m, b_m)
        hid, w_down = _mm_swiglu(u, ffn_w_in, ffn_w_out, layer, hidden)
        z = _mm_resid(hid, w_down, res, gt_f, alpha, seq, "ffn_out")
        g_f, b_f = row(ffn_ln_g[layer]), row(ffn_ln_b[layer])
        if layer + 1 == depth:
            return _layer_norm(z, g_f, b_f, seq).reshape(batch, seq, d)
        nsh, nsc = mod[layer + 1, :, 0], mod[layer + 1, :, 1]
        if (layer + 1) % n_mixers == 0:
            stats, u = _layer_norm(z, g_f, b_f, seq, ("mod", nsc, nsh))
        else:
            out = _layer_norm(z, g_f, b_f, seq, ("shift", nsc, nsh, rw_mu[(layer + 1) // n_mixers]))
            stats, u = out[0], out[1:]
        res = (z, stats, g_f, b_f)
```
